```python
import math
import jax, jax.numpy as jnp
from jax import lax
import numpy as np

D_MODEL = 1024
BATCH = 16
SEQ = 256
DEPTH = 4
DEC_BATCH = 8
DEC_SEQ = 1024
PAST_LEN = 256

GRID_W = 64
N_MIXERS = 3
N_ATTN_LAYERS = (DEPTH + 2) // 3
N_RWKV_LAYERS = (DEPTH + 1) // 3
N_CONV_LAYERS = DEPTH // 3
D_INNER = D_MODEL
NORM_EPS = 1e-6
ATTN_HEAD_DIM = 64
ATTN_HEADS = D_INNER // (2 * ATTN_HEAD_DIM)
ATTN_V_DIM = 2 * ATTN_HEAD_DIM
ROPE_BASE = 10000.0
Q_BLOCK = 128
RWKV_HEAD_DIM = 64
RWKV_HEADS = D_INNER // RWKV_HEAD_DIM
DECAY_LORA = 64
ICLR_LORA = 64
RWKV_GN_EPS = RWKV_HEAD_DIM * 1e-5
N_SHIFT_MIX = 6
CONV_WIDTH = 31

kernel_name = 'hybrid_diffattn_rwkv7_conformer_dit_step'

F32 = jnp.float32


def rms_norm(x, w, eps=NORM_EPS):
    xf = x.astype(F32)
    y = xf * lax.rsqrt(jnp.mean(xf * xf, axis=-1, keepdims=True) + eps)
    return (y * w.astype(F32)).astype(x.dtype)


def layer_norm(x, w, b, eps=1e-5):
    xf = x.astype(F32)
    m = jnp.mean(xf, axis=-1, keepdims=True)
    var = jnp.mean(jnp.square(xf - m), axis=-1, keepdims=True)
    y = (xf - m) * lax.rsqrt(var + eps)
    return (y * w.astype(F32) + b.astype(F32)).astype(x.dtype)


def adaln(cond, w, b):
    m = jnp.einsum('...d,de->...e', jax.nn.silu(cond), w) + b
    return jnp.split(m, 3, axis=-1)


def axial_rope_tables(n_tok):
    rows = n_tok // GRID_W
    rr, cc = jnp.meshgrid(jnp.arange(rows, dtype=F32), jnp.arange(GRID_W, dtype=F32), indexing='ij')
    row, col = rr.reshape(-1), cc.reshape(-1)
    axis_dim = ATTN_HEAD_DIM // 2
    inv_freq = ROPE_BASE ** (-jnp.arange(0, axis_dim, 2, dtype=F32) / axis_dim)
    ang_r = row[:, None] * inv_freq
    ang_c = col[:, None] * inv_freq
    shp = (n_tok, 1, 1, axis_dim // 2)
    return (jnp.cos(ang_r).reshape(shp), jnp.sin(ang_r).reshape(shp),
            jnp.cos(ang_c).reshape(shp), jnp.sin(ang_c).reshape(shp))


def _rotate(x, cos, sin):
    x1, x2 = jnp.split(x, 2, axis=-1)
    return jnp.concatenate([x1 * cos - x2 * sin, x1 * sin + x2 * cos], axis=-1)


def axial_rope(x, tables):
    cr, sr, cc, sc = tables
    xr, xc = jnp.split(x, 2, axis=-1)
    return jnp.concatenate([_rotate(xr, cr, sr), _rotate(xc, cc, sc)], axis=-1).astype(x.dtype)


def diff_softmax_attend(q, k, v, lam):
    B, H, Tq, _ = q.shape
    qb = Q_BLOCK if Tq % Q_BLOCK == 0 else Tq
    nb = Tq // qb
    k1, k2 = jnp.split(k, 2, axis=-1)
    scale = ATTN_HEAD_DIM ** -0.5

    def block(q_blk):
        q1, q2 = jnp.split(q_blk, 2, axis=-1)
        s1 = jnp.einsum('bhqd,bhkd->bhqk', q1, k1).astype(F32) * scale
        s2 = jnp.einsum('bhqd,bhkd->bhqk', q2, k2).astype(F32) * scale
        p = jax.nn.softmax(s1, axis=-1) - lam * jax.nn.softmax(s2, axis=-1)
        return jnp.einsum('bhqk,bhkd->bhqd', p.astype(v.dtype), v)

    q_blocks = jnp.moveaxis(q.reshape(B, H, nb, qb, q.shape[-1]), 2, 0)
    out = lax.map(block, q_blocks)
    return jnp.moveaxis(out, 0, 2).reshape(B, H, Tq, v.shape[-1])


def diff_attn_mixer(h, layer_idx, w_in, lam_vecs, subln_w, w_out, rope=None, ctx_k=None, ctx_v=None):
    B, T, _ = h.shape
    q, k, v, g = jnp.split(h @ w_in, 4, axis=-1)
    q = q.reshape(B, T, ATTN_HEADS, 2, ATTN_HEAD_DIM)
    k = k.reshape(B, T, ATTN_HEADS, 2, ATTN_HEAD_DIM)
    if rope is not None:
        q = axial_rope(q, rope)
        k = axial_rope(k, rope)
    heads = lambda t: t.reshape(B, T, ATTN_HEADS, -1).transpose(0, 2, 1, 3)
    q, k, v = heads(q), heads(k), heads(v)
    own_k, own_v = k, v
    if ctx_k is not None:
        k = jnp.concatenate([k, ctx_k.astype(k.dtype)], axis=2)
        v = jnp.concatenate([v, ctx_v.astype(v.dtype)], axis=2)
    lam_init = 0.8 - 0.6 * math.exp(-0.3 * layer_idx)
    lf = lam_vecs.astype(F32)
    lam = jnp.exp(jnp.sum(lf[0] * lf[1])) - jnp.exp(jnp.sum(lf[2] * lf[3])) + lam_init
    o = diff_softmax_attend(q, k, v, lam)
    o = rms_norm(o, subln_w, eps=1e-5) * (1.0 - lam_init)
    o = o.transpose(0, 2, 1, 3).reshape(B, T, D_INNER)
    return (o * jax.nn.silu(g)) @ w_out, own_k, own_v


def rwkv7_mixer(h, mu, w_in, w0, w1, w2, a0, a1, a2, k_k, k_a, r_k, ln_w, ln_b, w_out, init_state=None):
    B, T, D = h.shape
    H, N = RWKV_HEADS, RWKV_HEAD_DIM
    hp = jnp.pad(h, ((0, 0), (1, 1), (0, 0)))
    dx = 0.5 * (hp[:, :-2] + hp[:, 2:]) - h
    xs = h[None] + dx[None] * mu[:, None, None, :]
    r, k, v, g = jnp.einsum('nbtd,dne->nbte', xs[:4], w_in.reshape(D, 4, D_INNER))
    lw = jnp.einsum('zbtr,zre->zbte', jnp.tanh(jnp.einsum('btd,zdr->zbtr', xs[4], w1)), w2)
    w_log = -jax.nn.softplus(-(w0[:, None, None, :] + lw).astype(F32)) - 0.5
    decay = jnp.exp(-jnp.exp(w_log))
    la = jnp.einsum('zbtr,zre->zbte', jnp.einsum('btd,zdr->zbtr', xs[5], a1), a2)
    a = jax.nn.sigmoid((a0[:, None, None, :] + la).astype(F32))
    r32, k32, v32 = r.astype(F32), k.astype(F32), v.astype(F32)
    kk = (k32 * k_k.astype(F32)).reshape(B, T, H, N)
    kk = kk * lax.rsqrt(jnp.maximum(jnp.sum(kk * kk, axis=-1, keepdims=True), 1e-24))
    kk = kk.reshape(B, T, D_INNER)
    k_dir = k32[None] * (1.0 + (a - 1.0) * k_a.astype(F32))

    both = lambda x2: jnp.stack([x2[0], jnp.flip(x2[1], axis=1)])
    shared = lambda x: jnp.stack([x, jnp.flip(x, axis=1)])
    to_scan = lambda x2: jnp.moveaxis(x2.reshape(2, B, T, H, N), 2, 0)
    seq = (to_scan(shared(r32)), to_scan(both(decay)), to_scan(both(k_dir)),
           to_scan(shared(v32)), to_scan(shared(kk)), to_scan(both(a)))

    def step(S, inp):
        r_t, w_t, k_t, v_t, kk_t, a_t = inp
        sa = jnp.einsum('zbhij,zbhj->zbhi', S, -kk_t)
        S = S * w_t[..., None, :] + sa[..., None] * (kk_t * a_t)[..., None, :] + v_t[..., None] * k_t[..., None, :]
        return S, jnp.einsum('zbhij,zbhj->zbhi', S, r_t)

    if init_state is None:
        S0 = jnp.zeros((2, B, H, N, N), F32)
    else:
        S0 = jnp.moveaxis(init_state.astype(F32), 1, 0)
    S_final, ys = lax.scan(step, S0, seq)
    ys = jnp.moveaxis(ys, 0, 2)
    y = ys[0] + jnp.flip(ys[1], axis=1)
    m = jnp.mean(y, axis=-1, keepdims=True)
    var = jnp.mean(jnp.square(y - m), axis=-1, keepdims=True)
    y = ((y - m) * lax.rsqrt(var + RWKV_GN_EPS)).reshape(B, T, D_INNER)
    y = y * ln_w.astype(F32) + ln_b.astype(F32)
    rk = jnp.sum((r32[None] * k_dir * r_k.reshape(-1).astype(F32)).reshape(2, B, T, H, N), axis=-1, keepdims=True)
    bonus = jnp.sum(rk * v32.reshape(B, T, H, N)[None], axis=0).reshape(B, T, D_INNER)
    y = (y + bonus).astype(h.dtype)
    return (y * jax.nn.silu(g)) @ w_out, jnp.moveaxis(S_final, 0, 1)


def conformer_conv_mixer(h, w_in, dw_w, dw_b, ln_w, ln_b, w_out):
    a, b, g = jnp.split(h @ w_in, 3, axis=-1)
    z = a * jax.nn.sigmoid(b)
    pad = CONV_WIDTH // 2
    z = lax.conv_general_dilated(z, dw_w[:, None, :].astype(z.dtype), window_strides=(1,),
                                 padding=[(pad, pad)], dimension_numbers=('NWC', 'WIO', 'NWC'),
                                 feature_group_count=D_INNER) + dw_b
    z = jax.nn.silu(layer_norm(z, ln_w, ln_b))
    return (z * jax.nn.silu(g)) @ w_out


def setup_inputs(seed: int = 0) -> dict:
    key = jax.random.key(seed)
    ks = iter(jax.random.split(key, 40))
    nrm = lambda shape, s: jax.random.normal(next(ks), shape, F32) * s
    D, E = D_MODEL, D_INNER
    NA, NB, NC = N_ATTN_LAYERS, N_RWKV_LAYERS, N_CONV_LAYERS
    return {
        'x_prompt': nrm((BATCH, SEQ, D), 1.0),
        'x_sample': nrm((DEC_BATCH, DEC_SEQ, D), 1.0),
        'cache_attn_k': nrm((DEC_BATCH, NA, ATTN_HEADS, PAST_LEN, 2 * ATTN_HEAD_DIM), 1.0),
        'cache_attn_v': nrm((DEC_BATCH, NA, ATTN_HEADS, PAST_LEN, ATTN_V_DIM), 1.0),
        'state_rwkv': nrm((DEC_BATCH, NB, 2, RWKV_HEADS, RWKV_HEAD_DIM, RWKV_HEAD_DIM), 0.3),
        'c': nrm((DEC_BATCH, D), 1.0),
        'c_ctx': nrm((D,), 1.0),
        'norm_w': 1.0 + nrm((DEPTH, D), 0.02),
        'ada_w': nrm((DEPTH, D, 3 * D), 0.5 * D ** -0.5),
        'ada_b': nrm((DEPTH, 3 * D), 0.01),
        'attn_w_in': nrm((NA, D, 4 * E), D ** -0.5),
        'attn_lambda': nrm((NA, 4, ATTN_HEAD_DIM), 0.1),
        'attn_subln_w': 1.0 + nrm((NA, ATTN_V_DIM), 0.02),
        'attn_w_out': nrm((NA, E, D), E ** -0.5),
        'rwkv_mu': jax.random.uniform(next(ks), (NB, N_SHIFT_MIX, D), F32),
        'rwkv_w_in': nrm((NB, D, 4 * E), D ** -0.5),
        'rwkv_w0': nrm((NB, 2, E), 0.5),
        'rwkv_w1': nrm((NB, 2, D, DECAY_LORA), D ** -0.5),
        'rwkv_w2': nrm((NB, 2, DECAY_LORA, E), 0.1 * DECAY_LORA ** -0.5),
        'rwkv_a0': nrm((NB, 2, E), 0.1),
        'rwkv_a1': nrm((NB, 2, D, ICLR_LORA), D ** -0.5),
        'rwkv_a2': nrm((NB, 2, ICLR_LORA, E), 0.5 * ICLR_LORA ** -0.5),
        'rwkv_k_k': 0.85 + nrm((NB, E), 0.05),
        'rwkv_k_a': 1.0 + nrm((NB, E), 0.05),
        'rwkv_r_k': nrm((NB, RWKV_HEADS, RWKV_HEAD_DIM), 0.1),
        'rwkv_ln_w': 1.0 + nrm((NB, E), 0.02),
        'rwkv_ln_b': nrm((NB, E), 0.01),
        'rwkv_w_out': nrm((NB, E, D), E ** -0.5),
        'conv_w_in': nrm((NC, D, 3 * E), D ** -0.5),
        'conv_dw_w': nrm((NC, CONV_WIDTH, E), CONV_WIDTH ** -0.5),
        'conv_dw_b': nrm((NC, E), 0.01),
        'conv_ln_w': 1.0 + nrm((NC, E), 0.02),
        'conv_ln_b': nrm((NC, E), 0.01),
        'conv_w_out': nrm((NC, E, D), E ** -0.5),
        'final_norm_w': 1.0 + nrm((D,), 0.02),
    }


def reference(x_prompt, x_sample, cache_attn_k, cache_attn_v, state_rwkv, c, c_ctx,
              norm_w, ada_w, ada_b, attn_w_in, attn_lambda, attn_subln_w, attn_w_out,
              rwkv_mu, rwkv_w_in, rwkv_w0, rwkv_w1, rwkv_w2, rwkv_a0, rwkv_a1, rwkv_a2,
              rwkv_k_k, rwkv_k_a, rwkv_r_k, rwkv_ln_w, rwkv_ln_b, rwkv_w_out,
              conv_w_in, conv_dw_w, conv_dw_b, conv_ln_w, conv_ln_b, conv_w_out, final_norm_w):
    rope = axial_rope_tables(x_sample.shape[1])
    cond_ctx = c_ctx[None, None, :]
    cond_lat = c[:, None, :]
    xp, xs = x_prompt, x_sample
    new_k, new_v, new_s = [], [], []
    for i in range(DEPTH):
        kind, j = i % N_MIXERS, i // N_MIXERS
        sh_p, sc_p, gt_p = adaln(cond_ctx, ada_w[i], ada_b[i])
        sh_s, sc_s, gt_s = adaln(cond_lat, ada_w[i], ada_b[i])
        hp = rms_norm(xp, norm_w[i]) * (1 + sc_p) + sh_p
        hs = rms_norm(xs, norm_w[i]) * (1 + sc_s) + sh_s
        if kind == 0:
            op, kc, vc = diff_attn_mixer(hp, i, attn_w_in[j], attn_lambda[j], attn_subln_w[j], attn_w_out[j])
            os_, _, _ = diff_attn_mixer(hs, i, attn_w_in[j], attn_lambda[j], attn_subln_w[j], attn_w_out[j],
                                        rope=rope, ctx_k=cache_attn_k[:, j], ctx_v=cache_attn_v[:, j])
            new_k.append(kc)
            new_v.append(vc)
        elif kind == 1:
            rw = (rwkv_mu[j], rwkv_w_in[j], rwkv_w0[j], rwkv_w1[j], rwkv_w2[j], rwkv_a0[j], rwkv_a1[j],
                  rwkv_a2[j], rwkv_k_k[j], rwkv_k_a[j], rwkv_r_k[j], rwkv_ln_w[j], rwkv_ln_b[j], rwkv_w_out[j])
            op, st = rwkv7_mixer(hp, *rw)
            os_, _ = rwkv7_mixer(hs, *rw, init_state=state_rwkv[:, j])
            new_s.append(st)
        else:
            cw = (conv_w_in[j], conv_dw_w[j], conv_dw_b[j], conv_ln_w[j], conv_ln_b[j], conv_w_out[j])
            op = conformer_conv_mixer(hp, *cw)
            os_ = conformer_conv_mixer(hs, *cw)
        xp = xp + gt_p * op
        xs = xs + gt_s * os_
    y_prompt = rms_norm(xp, final_norm_w)
    y_sample = rms_norm(xs, final_norm_w)
    new_attn_k = jnp.stack(new_k, axis=1)
    new_attn_v = jnp.stack(new_v, axis=1)
    new_rwkv_state = jnp.stack(new_s, axis=1)
    return (y_prompt, y_sample, new_attn_k, new_attn_v, new_rwkv_state)
```

```python
import functools
import math

import jax
import jax.numpy as jnp
from jax import lax
from jax.experimental import pallas as pl
from jax.experimental.pallas import tpu as pltpu

F32 = jnp.float32
BF16 = jnp.bfloat16

D_MODEL = 1024
D_INNER = 1024
DEPTH = 4
N_MIXERS = 3
GRID_W = 64
NORM_EPS = 1e-6
ATTN_HEAD_DIM = 64
ATTN_HEADS = 8
ATTN_V_DIM = 128
ROPE_BASE = 10000.0
RWKV_HEAD_DIM = 64
RWKV_HEADS = 16
RWKV_GN_EPS = RWKV_HEAD_DIM * 1e-5
LORA = 64
CONV_WIDTH = 31
CONV_PAD = CONV_WIDTH // 2

LANES = 128
SUBLANES = 8
MXU_DIM = 256
VMEM_LIMIT_BYTES = 56 * 1024 * 1024

ROW_TILE = 256
RWKV_ROW_TILE = 128
N_CHUNK = 512
SCAN_CHUNK = 64
SCAN_BLOCK = 256
HALO = 16
CONV_ROWS = 64
COND_ROWS = 16
DN = (((1,), (0,)), ((), ()))
DN_NT = (((1,), (1,)), ((), ()))


def _cparams(sem):
    return pltpu.CompilerParams(dimension_semantics=sem, vmem_limit_bytes=VMEM_LIMIT_BYTES)


def _mm(a, b, dn=DN):
    return lax.dot_general(a, b, dn, preferred_element_type=F32)


def _dot(a, b):
    return _mm(a.astype(BF16), b.astype(BF16))


def _split2(x):
    hi = x.astype(BF16)
    lo = (x - hi.astype(F32)).astype(BF16)
    return hi, lo


def _split3(x):
    hi = x.astype(BF16)
    r1 = x - hi.astype(F32)
    mid = r1.astype(BF16)
    lo = (r1 - mid.astype(F32)).astype(BF16)
    return hi, mid, lo


def _dot3(a, b):
    a_hi, a_lo = _split2(a)
    b_hi, b_lo = _split2(b)
    return _mm(a_hi, b_hi) + (_mm(a_lo, b_hi) + _mm(a_hi, b_lo))


def _dot_exact_lhs(m_bf16, x):
    hi, mid, lo = _split3(x)
    return _mm(m_bf16, hi) + (_mm(m_bf16, mid) + _mm(m_bf16, lo))


def _silu(x):
    return x * jax.nn.sigmoid(x)


def _normmod(x, nw, mod):
    ms = jnp.mean(x * x, axis=-1, keepdims=True)
    y = x * lax.rsqrt(ms + NORM_EPS) * nw
    return y * (1.0 + mod[:, D_MODEL:2 * D_MODEL]) + mod[:, 0:D_MODEL]


def _headsum(x, seg_ref):
    cols = []
    for c in range(0, x.shape[-1], MXU_DIM):
        hi, lo = _split2(x[:, c:c + MXU_DIM])
        cols.append(_mm(hi, seg_ref[...]) + _mm(lo, seg_ref[...]))
    return jnp.concatenate(cols, axis=-1)


def _mod_spec(layer, row_of):
    return pl.BlockSpec((None, None, 1, 3 * D_MODEL), lambda b, t: (layer, row_of(b), 0, 0))


def _const_spec(shape):
    return pl.BlockSpec(shape, lambda b, t: (0,) * len(shape))


def _ada_kernel(cond_ref, w_ref, b_ref, o_ref):
    o_ref[0, :, 0, :] = _dot3(_silu(cond_ref[...]), w_ref[0]) + b_ref[0]


def _ada(cond, ada_w, ada_b):
    depth, d, n3 = ada_w.shape
    tn = 1024
    return pl.pallas_call(
        _ada_kernel,
        grid=(depth, n3 // tn),
        in_specs=[
            pl.BlockSpec((COND_ROWS, d), lambda i, j: (0, 0)),
            pl.BlockSpec((1, d, tn), lambda i, j: (i, 0, j)),
            pl.BlockSpec((1, 1, tn), lambda i, j: (i, 0, j)),
        ],
        out_specs=pl.BlockSpec((1, COND_ROWS, 1, tn), lambda i, j: (i, 0, 0, j)),
        out_shape=jax.ShapeDtypeStruct((depth, COND_ROWS, 1, n3), F32),
        compiler_params=_cparams(("parallel", "parallel")),
        name="ada_mod",
    )(cond, ada_w, ada_b.reshape(depth, 1, n3))


def _attn_in_kernel(x_ref, nw_ref, mod_ref, w_ref, o_ref):
    hb = _normmod(x_ref[0], nw_ref[...], mod_ref[...]).astype(BF16)
    for n in range(0, 4 * D_INNER, N_CHUNK):
        o_ref[0, :, n:n + N_CHUNK] = _mm(hb, w_ref[:, n:n + N_CHUNK])


def _attn_in(x, nw, mods, layer, row_of, w_bf16):
    B, T, D = x.shape
    N = w_bf16.shape[1]
    tm = min(ROW_TILE, T)
    return pl.pallas_call(
        _attn_in_kernel,
        grid=(B, T // tm),
        in_specs=[
            pl.BlockSpec((1, tm, D), lambda b, t: (b, t, 0)),
            _const_spec((1, D)),
            _mod_spec(layer, row_of),
            _const_spec((D, N)),
        ],
        out_specs=pl.BlockSpec((1, tm, N), lambda b, t: (b, t, 0)),
        out_shape=jax.ShapeDtypeStruct((B, T, N), F32),
        compiler_params=_cparams(("parallel", "parallel")),
        name="attn_in",
    )(x, nw.reshape(1, D), mods, w_bf16)


def _rope(x, cos, sin):
    lane = lax.broadcasted_iota(jnp.int32, x.shape, 1)
    first = (lane % 32) < 16
    partner = jnp.where(first, pltpu.roll(x, LANES - 16, axis=1), pltpu.roll(x, 16, axis=1))
    return x * cos + partner * sin


def _softmax_pieces(pieces):
    m = None
    for s in pieces:
        pm = jnp.max(s, axis=-1, keepdims=True)
        m = pm if m is None else jnp.maximum(m, pm)
    es = [jnp.exp(s - m) for s in pieces]
    den = None
    for e in es:
        ps = jnp.sum(e, axis=-1, keepdims=True)
        den = ps if den is None else den + ps
    inv = 1.0 / den
    return [e * inv for e in es]


def _attn_core_kernel(*refs, lam_init, has_ctx, emit_cache, tq):
    it = iter(refs)
    lam_ref, q_ref, k_ref, v_ref, g_ref, subw_ref = (next(it) for _ in range(6))
    if has_ctx:
        ck_ref, cv_ref, cos_ref, sin_ref = (next(it) for _ in range(4))
    o_ref = next(it)
    if emit_cache:
        ko_ref, vo_ref = next(it), next(it)

    lv = lam_ref[...]
    lam = (jnp.exp(jnp.sum(lv[0:1] * lv[1:2], axis=-1, keepdims=True))
           - jnp.exp(jnp.sum(lv[2:3] * lv[3:4], axis=-1, keepdims=True)) + lam_init)

    k = k_ref[0]
    v = v_ref[0]
    if emit_cache:
        ko_ref[...] = k
        vo_ref[...] = v
    if has_ctx:
        k = _rope(k, cos_ref[...], sin_ref[...])
    keys = [k.astype(BF16)]
    vals = [v.astype(BF16)]
    if has_ctx:
        keys.append(ck_ref[...].astype(BF16))
        vals.append(cv_ref[...].astype(BF16))

    T = q_ref.shape[1]
    scale = ATTN_HEAD_DIM ** -0.5
    for q0 in range(0, T, tq):
        q = q_ref[0, q0:q0 + tq, :]
        if has_ctx:
            q = _rope(q, cos_ref[q0:q0 + tq, :], sin_ref[q0:q0 + tq, :])
        q = q * scale
        lane = lax.broadcasted_iota(jnp.int32, q.shape, 1)
        q1 = jnp.where(lane < ATTN_HEAD_DIM, q, 0.0).astype(BF16)
        q2 = jnp.where(lane >= ATTN_HEAD_DIM, q, 0.0).astype(BF16)
        p1 = _softmax_pieces([_mm(q1, kk, DN_NT) for kk in keys])
        p2 = _softmax_pieces([_mm(q2, kk, DN_NT) for kk in keys])
        o = None
        for a1, a2, vv in zip(p1, p2, vals):
            po = _mm((a1 - lam * a2).astype(BF16), vv)
            o = po if o is None else o + po
        ms = jnp.mean(o * o, axis=-1, keepdims=True)
        o = o * lax.rsqrt(ms + 1e-5) * subw_ref[...] * (1.0 - lam_init)
        o_ref[0, q0:q0 + tq, :] = o * _silu(g_ref[0, q0:q0 + tq, :])


def _attn_core(qkvg, lam_vecs, subw, layer_idx, ctx=None, emit_cache=False):
    B, T, _ = qkvg.shape
    H, DV = ATTN_HEADS, ATTN_V_DIM
    lam_init = 0.8 - 0.6 * math.exp(-0.3 * layer_idx)
    has_ctx = ctx is not None
    tq = min(ROW_TILE, T)

    def col(off):
        return pl.BlockSpec((1, T, DV), lambda b, h: (b, 0, off + h))

    in_specs = [pl.BlockSpec((4, ATTN_HEAD_DIM), lambda b, h: (0, 0)),
                col(0), col(H), col(2 * H), col(3 * H),
                pl.BlockSpec((1, DV), lambda b, h: (0, 0))]
    args = [lam_vecs, qkvg, qkvg, qkvg, qkvg, subw.reshape(1, DV)]
    if has_ctx:
        ck, cv, j, cos, sin = ctx
        P = ck.shape[3]
        cache_spec = pl.BlockSpec((None, None, None, P, DV), lambda b, h: (b, j, h, 0, 0))
        in_specs += [cache_spec, cache_spec,
                     pl.BlockSpec((T, DV), lambda b, h: (0, 0)),
                     pl.BlockSpec((T, DV), lambda b, h: (0, 0))]
        args += [ck, cv, cos, sin]
    out_specs = [pl.BlockSpec((1, T, DV), lambda b, h: (b, 0, h))]
    out_shape = [jax.ShapeDtypeStruct((B, T, D_INNER), F32)]
    if emit_cache:
        out_specs += [pl.BlockSpec((None, None, T, DV), lambda b, h: (b, h, 0, 0))] * 2
        out_shape += [jax.ShapeDtypeStruct((B, H, T, DV), F32)] * 2
    return pl.pallas_call(
        functools.partial(_attn_core_kernel, lam_init=lam_init, has_ctx=has_ctx,
                          emit_cache=emit_cache, tq=tq),
        grid=(B, H),
        in_specs=in_specs,
        out_specs=out_specs,
        out_shape=out_shape,
        compiler_params=_cparams(("parallel", "parallel")),
        name="attn_core",
    )(*args)


def _gated_residual(o, w_ref, x_ref, mod_ref):
    return x_ref[0] + mod_ref[:, 2 * D_MODEL:3 * D_MODEL] * _dot(o, w_ref[...])


def _out_proj_kernel(o_ref, w_ref, x_ref, mod_ref, *rest, final_norm):
    xn = _gated_residual(o_ref[0], w_ref, x_ref, mod_ref)
    if final_norm:
        fw_ref, y_ref = rest
        ms = jnp.mean(xn * xn, axis=-1, keepdims=True)
        xn = xn * lax.rsqrt(ms + NORM_EPS) * fw_ref[...]
    else:
        (y_ref,) = rest
    y_ref[0] = xn


def _out_proj(o, w_bf16, x, mods, layer, row_of, final_w=None):
    B, T, D = x.shape
    tm = min(ROW_TILE, T)
    tile = pl.BlockSpec((1, tm, D), lambda b, t: (b, t, 0))
    in_specs = [tile, _const_spec((D, D)), tile, _mod_spec(layer, row_of)]
    args = [o, w_bf16, x, mods]
    if final_w is not None:
        in_specs.append(_const_spec((1, D)))
        args.append(final_w.reshape(1, D))
    return pl.pallas_call(
        functools.partial(_out_proj_kernel, final_norm=final_w is not None),
        grid=(B, T // tm),
        in_specs=in_specs,
        out_specs=tile,
        out_shape=jax.ShapeDtypeStruct((B, T, D), F32),
        compiler_params=_cparams(("parallel", "parallel")),
        name="out_proj",
    )(*args)


def _rwkv_in_kernel(x_ref, xp_ref, xn_ref, nw_ref, mod_ref, mu_ref, w_ref, l1_ref, w2_ref, a2_ref,
                    w0_ref, a0_ref, kk_ref, ka_ref, rk_ref, seg_ref,
                    r_out, kk_out, v_out, sg_out, bonus_out, lw_out, a_out, kz_out):
    t = pl.program_id(1)
    nt = pl.num_programs(1)
    nw = nw_ref[...]
    mod = mod_ref[...]
    h = _normmod(x_ref[0], nw, mod)
    tm = h.shape[0]
    hp = _normmod(xp_ref[0], nw, mod)[SUBLANES - 1:SUBLANES, :]
    hn = _normmod(xn_ref[0], nw, mod)[0:1, :]
    hp = jnp.where(t > 0, hp, 0.0)
    hn = jnp.where(t < nt - 1, hn, 0.0)
    row = lax.broadcasted_iota(jnp.int32, h.shape, 0)
    h_prev = jnp.where(row == 0, hp, pltpu.roll(h, 1, axis=0))
    h_next = jnp.where(row == tm - 1, hn, pltpu.roll(h, tm - 1, axis=0))
    dx = 0.5 * (h_prev + h_next) - h

    def mix(n):
        return (h + dx * mu_ref[n:n + 1, :]).astype(BF16)

    E = D_INNER
    r = _mm(mix(0), w_ref[:, 0:E])
    k = _mm(mix(1), w_ref[:, E:2 * E])
    v = _mm(mix(2), w_ref[:, 2 * E:3 * E])
    g = _mm(mix(3), w_ref[:, 3 * E:4 * E])
    lw1 = jnp.tanh(_mm(mix(4), l1_ref[:, 0:2 * LORA]))
    la1 = _mm(mix(5), l1_ref[:, 2 * LORA:4 * LORA])
    lane = lax.broadcasted_iota(jnp.int32, lw1.shape, 1)

    kk = k * kk_ref[...]
    kk = kk * lax.rsqrt(jnp.maximum(_headsum(kk * kk, seg_ref), 1e-24))
    r_out[0] = r
    kk_out[0] = kk
    v_out[0] = v
    sg_out[0] = _silu(g)

    ksum = None
    for z in range(2):
        sel = (lane >= z * LORA) & (lane < (z + 1) * LORA)
        lw = _dot(jnp.where(sel, lw1, 0.0), w2_ref[...])
        la = _dot(jnp.where(sel, la1, 0.0), a2_ref[...])
        u = -(w0_ref[z:z + 1, :] + lw)
        softplus = jnp.maximum(u, 0.0) + jnp.log(1.0 + jnp.exp(-jnp.abs(u)))
        lw_out[z, 0] = -jnp.exp(-softplus - 0.5)
        a = jax.nn.sigmoid(a0_ref[z:z + 1, :] + la)
        a_out[z, 0] = a
        kz = k * (1.0 + (a - 1.0) * ka_ref[...])
        kz_out[z, 0] = kz
        ksum = kz if ksum is None else ksum + kz
    bonus_out[0] = _headsum(r * rk_ref[...] * ksum, seg_ref) * v


def _rwkv_in(x, nw, mods, layer, row_of, p):
    B, T, D = x.shape
    E = D_INNER
    tm = min(RWKV_ROW_TILE, T)
    g8 = tm // SUBLANES
    last8 = T // SUBLANES - 1
    tile = pl.BlockSpec((1, tm, D), lambda b, t: (b, t, 0))
    prev8 = pl.BlockSpec((1, SUBLANES, D), lambda b, t: (b, jnp.maximum(t * g8 - 1, 0), 0))
    next8 = pl.BlockSpec((1, SUBLANES, D), lambda b, t: (b, jnp.minimum((t + 1) * g8, last8), 0))
    otile = pl.BlockSpec((1, tm, E), lambda b, t: (b, t, 0))
    ztile = pl.BlockSpec((2, 1, tm, E), lambda b, t: (0, b, t, 0))
    one = jax.ShapeDtypeStruct((B, T, E), F32)
    two = jax.ShapeDtypeStruct((2, B, T, E), F32)
    return pl.pallas_call(
        _rwkv_in_kernel,
        grid=(B, T // tm),
        in_specs=[tile, prev8, next8, _const_spec((1, D)), _mod_spec(layer, row_of),
                  _const_spec((6, D)), _const_spec((D, 4 * E)), _const_spec((D, 4 * LORA)),
                  _const_spec((2 * LORA, E)), _const_spec((2 * LORA, E)),
                  _const_spec((2, E)), _const_spec((2, E)),
                  _const_spec((1, E)), _const_spec((1, E)), _const_spec((1, E)),
                  _const_spec((MXU_DIM, MXU_DIM))],
        out_specs=[otile] * 5 + [ztile] * 3,
        out_shape=[one] * 5 + [two] * 3,
        compiler_params=_cparams(("parallel", "parallel")),
        name="rwkv_in",
    )(x, x, x, nw.reshape(1, D), mods, p["mu"], p["w_in"], p["l1"], p["w2"], p["a2"],
      p["w0"], p["a0"], p["k_k"], p["k_a"], p["r_k"], p["seg"])


def _scan_kernel(*refs, reverse, has_init):
    if has_init:
        r_ref, kk_ref, v_ref, lw_ref, a_ref, kz_ref, s0_ref, y_ref, sf_ref, s_scr = refs
    else:
        r_ref, kk_ref, v_ref, lw_ref, a_ref, kz_ref, y_ref, sf_ref, s_scr = refs
    C = SCAN_CHUNK
    PW = 2 * C
    HD = RWKV_HEAD_DIM

    @pl.when(pl.program_id(2) == 0)
    def _():
        s_scr[...] = s0_ref[...] if has_init else jnp.zeros_like(s_scr)

    ri = lax.broadcasted_iota(jnp.int32, (PW, PW), 0)
    ci = lax.broadcasted_iota(jnp.int32, (PW, PW), 1)
    same_blk = (ri // C) == (ci // C)
    tr, tc = ri % C, ci % C
    before = (tc > tr) if reverse else (tc < tr)
    strict = same_blk & before
    incl = same_blk & (before | (tc == tr))
    same_head = (ri // HD) == (ci // HD)
    diag = ri == ci
    li = lax.broadcasted_iota(jnp.int32, (C, C), 0)
    lj = lax.broadcasted_iota(jnp.int32, (C, C), 1)
    cum_mat = jnp.where((lj >= li) if reverse else (lj <= li), 1.0, 0.0).astype(BF16)
    lo = lax.broadcasted_iota(jnp.int32, (C, LANES), 1) < HD
    zeros_c = jnp.zeros((C, LANES), F32)

    def stack2(x):
        return jnp.concatenate([x, x], axis=0)

    def unstack(x2):
        return jnp.where(lo, x2[0:C], x2[C:PW])

    S = s_scr[...]
    nchunks = SCAN_BLOCK // C
    for c in (reversed(range(nchunks)) if reverse else range(nchunks)):
        rows = slice(c * C, (c + 1) * C)
        lw = lw_ref[0, 0, rows, :]
        kk = kk_ref[0, rows, :]
        a = a_ref[0, 0, rows, :]
        kz = kz_ref[0, 0, rows, :]
        v = v_ref[0, rows, :]
        r = r_ref[0, rows, :]

        cum = _dot_exact_lhs(cum_mat, lw)
        tot = cum[0:1] if reverse else cum[C - 1:C]
        e_neg = jnp.exp(-cum)
        e_tot = jnp.exp(tot - cum)
        abar = -kk * jnp.exp(cum - lw)
        rbar = r * jnp.exp(cum)
        b = kk * a
        bt, kt = b * e_neg, kz * e_neg
        bh, kh = b * e_tot, kz * e_tot

        lhs = jnp.concatenate([jnp.where(lo, abar, 0.0), jnp.where(lo, 0.0, abar),
                               jnp.where(lo, rbar, 0.0), jnp.where(lo, 0.0, rbar)], axis=0)
        rhs = jnp.concatenate([bt, bt, kt, kt], axis=0)
        res = _mm(lhs.astype(BF16), rhs.astype(BF16), DN_NT)
        a_ab = jnp.where(strict, res[0:PW, 0:PW], 0.0)
        a_ak = jnp.where(strict, res[0:PW, PW:2 * PW], 0.0)
        a_rb = jnp.where(incl, res[PW:2 * PW, 0:PW], 0.0)
        a_rk = jnp.where(incl, res[PW:2 * PW, PW:2 * PW], 0.0)

        v2 = stack2(v)
        x = jnp.concatenate([stack2(abar), _dot(a_ak, v2)], axis=1)
        pw = a_ab
        n = 1
        while n < C:
            x = x + _dot(pw, x)
            n *= 2
            if n < C:
                pw = _dot(pw, pw)
        ahat = unstack(x[:, 0:LANES])
        u0 = unstack(x[:, LANES:2 * LANES])

        top = jnp.concatenate([stack2(ahat), stack2(u0)], axis=1)
        bot = jnp.concatenate([jnp.zeros((PW, LANES), F32), v2], axis=1)
        ry = _dot(jnp.concatenate([a_rb, a_rk], axis=1), jnp.concatenate([top, bot], axis=0))
        rhat = rbar + unstack(ry[:, 0:LANES])
        y0 = unstack(ry[:, LANES:2 * LANES])

        bk_t = jnp.concatenate([bh, kh], axis=0).T
        mn = _dot(bk_t, jnp.concatenate([jnp.concatenate([ahat, u0], axis=1),
                                         jnp.concatenate([zeros_c, v], axis=1)], axis=0))
        m_c = jnp.where(same_head, mn[:, 0:LANES], 0.0) + jnp.where(diag, jnp.exp(tot), 0.0)
        n_c = jnp.where(same_head, mn[:, LANES:2 * LANES], 0.0)

        y_ref[0, rows, :] = _dot3(rhat, S) + y0
        S = _dot3(m_c, S) + n_c
    s_scr[...] = S
    sf_ref[...] = S


def _rwkv_scan(r, kk, v, lw, a, kz, z, s0=None):
    B, T, E = r.shape
    NP = E // LANES
    TB = SCAN_BLOCK
    nb = T // TB
    reverse = z == 1

    def tok(b, p, t):
        return (nb - 1 - t) if reverse else t

    one = pl.BlockSpec((1, TB, LANES), lambda b, p, t: (b, tok(b, p, t), p))
    two = pl.BlockSpec((1, 1, TB, LANES), lambda b, p, t: (z, b, tok(b, p, t), p))
    state = pl.BlockSpec((None, None, LANES, LANES), lambda b, p, t: (b, p, 0, 0))
    in_specs = [one, one, one, two, two, two]
    args = [r, kk, v, lw, a, kz]
    if s0 is not None:
        in_specs.append(state)
        args.append(s0)
    return pl.pallas_call(
        functools.partial(_scan_kernel, reverse=reverse, has_init=s0 is not None),
        grid=(B, NP, nb),
        in_specs=in_specs,
        out_specs=[one, state],
        out_shape=[jax.ShapeDtypeStruct((B, T, E), F32),
                   jax.ShapeDtypeStruct((B, NP, LANES, LANES), F32)],
        scratch_shapes=[pltpu.VMEM((LANES, LANES), F32)],
        compiler_params=_cparams(("parallel", "parallel", "arbitrary")),
        name="rwkv_scan",
    )(*args)


def _rwkv_out_kernel(y0_ref, y1_ref, bonus_ref, sg_ref, lnw_ref, lnb_ref, seg_ref, w_ref, x_ref, mod_ref, o_ref):
    y = y0_ref[0] + y1_ref[0]
    inv_n = 1.0 / RWKV_HEAD_DIM
    d = y - _headsum(y, seg_ref) * inv_n
    var = _headsum(d * d, seg_ref) * inv_n
    yn = d * lax.rsqrt(var + RWKV_GN_EPS) * lnw_ref[...] + lnb_ref[...] + bonus_ref[0]
    o_ref[0] = _gated_residual(yn * sg_ref[0], w_ref, x_ref, mod_ref)


def _rwkv_out(y0, y1, bonus, sg, p, x, mods, layer, row_of):
    B, T, D = x.shape
    tm = min(ROW_TILE, T)
    tile = pl.BlockSpec((1, tm, D), lambda b, t: (b, t, 0))
    return pl.pallas_call(
        _rwkv_out_kernel,
        grid=(B, T // tm),
        in_specs=[tile, tile, tile, tile, _const_spec((1, D)), _const_spec((1, D)),
                  _const_spec((MXU_DIM, MXU_DIM)), _const_spec((D, D)), tile, _mod_spec(layer, row_of)],
        out_specs=tile,
        out_shape=jax.ShapeDtypeStruct((B, T, D), F32),
        compiler_params=_cparams(("parallel", "parallel")),
        name="rwkv_out",
    )(y0, y1, bonus, sg, p["ln_w"], p["ln_b"], p["seg"], p["w_out"], x, mods)


def _conv_in_kernel(x_ref, nw_ref, mod_ref, w_ref, z_ref, sg_ref):
    hb = _normmod(x_ref[0], nw_ref[...], mod_ref[...]).astype(BF16)
    E = D_INNER
    for n in range(0, E, N_CHUNK):
        a = _mm(hb, w_ref[:, n:n + N_CHUNK])
        b = _mm(hb, w_ref[:, E + n:E + n + N_CHUNK])
        g = _mm(hb, w_ref[:, 2 * E + n:2 * E + n + N_CHUNK])
        z_ref[0, :, n:n + N_CHUNK] = a * jax.nn.sigmoid(b)
        sg_ref[0, :, n:n + N_CHUNK] = _silu(g)


def _conv_in(x, nw, mods, layer, row_of, w_bf16):
    B, T, D = x.shape
    E = D_INNER
    tm = min(ROW_TILE, T)
    tile = pl.BlockSpec((1, tm, D), lambda b, t: (b, t, 0))
    return pl.pallas_call(
        _conv_in_kernel,
        grid=(B, T // tm),
        in_specs=[tile, _const_spec((1, D)), _mod_spec(layer, row_of), _const_spec((D, 3 * E))],
        out_specs=[tile, tile],
        out_shape=[jax.ShapeDtypeStruct((B, T, E), F32)] * 2,
        compiler_params=_cparams(("parallel", "parallel")),
        name="conv_in",
    )(x, nw.reshape(1, D), mods, w_bf16)


def _conv_out_kernel(z_ref, zp_ref, zn_ref, sg_ref, dw_ref, dwb_ref, lnw_ref, lnb_ref, w_ref, x_ref, mod_ref,
                     o_ref, zpad, cbuf):
    t = pl.program_id(1)
    nt = pl.num_programs(1)
    tm = z_ref.shape[1]
    zpad[0:HALO, :] = jnp.where(t > 0, zp_ref[0], 0.0)
    zpad[HALO:HALO + tm, :] = z_ref[0]
    zpad[HALO + tm:HALO + tm + HALO, :] = jnp.where(t < nt - 1, zn_ref[0], 0.0)
    off = HALO - CONV_PAD
    for c0 in range(0, D_INNER, LANES):
        for r0 in range(0, tm, CONV_ROWS):
            acc = None
            for k in range(CONV_WIDTH):
                term = dw_ref[k:k + 1, c0:c0 + LANES] * zpad[r0 + off + k:r0 + off + k + CONV_ROWS, c0:c0 + LANES]
                acc = term if acc is None else acc + term
            cbuf[r0:r0 + CONV_ROWS, c0:c0 + LANES] = acc
    c = cbuf[...] + dwb_ref[...]
    m = jnp.mean(c, axis=-1, keepdims=True)
    d = c - m
    var = jnp.mean(d * d, axis=-1, keepdims=True)
    y = d * lax.rsqrt(var + 1e-5) * lnw_ref[...] + lnb_ref[...]
    o_ref[0] = _gated_residual(_silu(y) * sg_ref[0], w_ref, x_ref, mod_ref)


def _conv_out(z, sg, p, x, mods, layer, row_of):
    B, T, D = x.shape
    tm = min(ROW_TILE, T)
    gh = tm // HALO
    lasth = T // HALO - 1
    tile = pl.BlockSpec((1, tm, D), lambda b, t: (b, t, 0))
    prev = pl.BlockSpec((1, HALO, D), lambda b, t: (b, jnp.maximum(t * gh - 1, 0), 0))
    nxt = pl.BlockSpec((1, HALO, D), lambda b, t: (b, jnp.minimum((t + 1) * gh, lasth), 0))
    return pl.pallas_call(
        _conv_out_kernel,
        grid=(B, T // tm),
        in_specs=[tile, prev, nxt, tile, _const_spec((CONV_WIDTH, D)), _const_spec((1, D)),
                  _const_spec((1, D)), _const_spec((1, D)), _const_spec((D, D)), tile,
                  _mod_spec(layer, row_of)],
        out_specs=tile,
        out_shape=jax.ShapeDtypeStruct((B, T, D), F32),
        scratch_shapes=[pltpu.VMEM((tm + 2 * HALO, D), F32), pltpu.VMEM((tm, D), F32)],
        compiler_params=_cparams(("parallel", "parallel")),
        name="conv_out",
    )(z, z, z, sg, p["dw_w"], p["dw_b"], p["ln_w"], p["ln_b"], p["w_out"], x, mods)


def _rope_tables(n_tok):
    t = jnp.arange(n_tok, dtype=jnp.int32)
    pos = jnp.stack([(t // GRID_W).astype(F32), (t % GRID_W).astype(F32)], axis=1)
    lane = jnp.arange(LANES, dtype=jnp.int32)
    d = lane % ATTN_HEAD_DIM
    axis = d // 32
    second_half = (d % 32) // 16
    axis_dim = ATTN_HEAD_DIM // 2
    inv_freq = ROPE_BASE ** (-(2.0 * (d % 16).astype(F32)) / axis_dim)
    ang = pos[:, axis] * inv_freq[None, :]
    sign = jnp.where(second_half == 1, 1.0, -1.0).astype(F32)
    return jnp.cos(ang), jnp.sin(ang) * sign[None, :]


def _pair_blockdiag(s):
    B, H, N, _ = s.shape
    st = jnp.swapaxes(s, -1, -2).reshape(B, H // 2, 2, N, N)
    z = jnp.zeros_like(st[:, :, 0])
    top = jnp.concatenate([st[:, :, 0], z], axis=-1)
    bot = jnp.concatenate([z, st[:, :, 1]], axis=-1)
    return jnp.concatenate([top, bot], axis=-2)


def _pair_unblock(sp):
    B, NP, _, _ = sp.shape
    N = RWKV_HEAD_DIM
    blocks = jnp.stack([sp[:, :, 0:N, 0:N], sp[:, :, N:2 * N, N:2 * N]], axis=2)
    return jnp.swapaxes(blocks, -1, -2).reshape(B, 2 * NP, N, N)


def kernel(x_prompt, x_sample, cache_attn_k, cache_attn_v, state_rwkv, c, c_ctx, norm_w, ada_w, ada_b, attn_w_in, attn_lambda, attn_subln_w, attn_w_out, rwkv_mu, rwkv_w_in, rwkv_w0, rwkv_w1, rwkv_w2, rwkv_a0, rwkv_a1, rwkv_a2, rwkv_k_k, rwkv_k_a, rwkv_r_k, rwkv_ln_w, rwkv_ln_b, rwkv_w_out, conv_w_in, conv_dw_w, conv_dw_b, conv_ln_w, conv_ln_b, conv_w_out, final_norm_w):
    D, E = D_MODEL, D_INNER
    dec_batch = x_sample.shape[0]
    assert dec_batch < COND_ROWS
    cond = jnp.concatenate([c, c_ctx[None, :], jnp.zeros((COND_ROWS - dec_batch - 1, D), F32)], axis=0)
    mods = _ada(cond, ada_w, ada_b)
    ctx_row = dec_batch
    streams = [(x_prompt, lambda b: ctx_row), (x_sample, lambda b: b)]
    cos, sin = _rope_tables(x_sample.shape[1])
    seg = (jnp.arange(MXU_DIM)[:, None] // RWKV_HEAD_DIM == jnp.arange(MXU_DIM)[None, :] // RWKV_HEAD_DIM).astype(BF16)

    xs = [x_prompt, x_sample]
    new_k, new_v, new_s = [], [], []
    for i in range(DEPTH):
        kind, j = i % N_MIXERS, i // N_MIXERS
        last = i == DEPTH - 1
        fw = final_norm_w if last else None
        if kind == 0:
            w_in = attn_w_in[j].astype(BF16)
            w_out = attn_w_out[j].astype(BF16)
            for s, (_, row_of) in enumerate(streams):
                qkvg = _attn_in(xs[s], norm_w[i], mods, i, row_of, w_in)
                if s == 0:
                    o, kc, vc = _attn_core(qkvg, attn_lambda[j], attn_subln_w[j], i, emit_cache=True)
                    new_k.append(kc)
                    new_v.append(vc)
                else:
                    (o,) = _attn_core(qkvg, attn_lambda[j], attn_subln_w[j], i,
                                      ctx=(cache_attn_k, cache_attn_v, j, cos, sin))
                xs[s] = _out_proj(o, w_out, xs[s], mods, i, row_of, final_w=fw)
        elif kind == 1:
            p = dict(
                mu=rwkv_mu[j],
                w_in=rwkv_w_in[j].astype(BF16),
                l1=jnp.concatenate([rwkv_w1[j, 0], rwkv_w1[j, 1], rwkv_a1[j, 0], rwkv_a1[j, 1]], axis=1).astype(BF16),
                w2=rwkv_w2[j].reshape(2 * LORA, E).astype(BF16),
                a2=rwkv_a2[j].reshape(2 * LORA, E).astype(BF16),
                w0=rwkv_w0[j], a0=rwkv_a0[j],
                k_k=rwkv_k_k[j].reshape(1, E), k_a=rwkv_k_a[j].reshape(1, E), r_k=rwkv_r_k[j].reshape(1, E),
                ln_w=rwkv_ln_w[j].reshape(1, E), ln_b=rwkv_ln_b[j].reshape(1, E),
                w_out=rwkv_w_out[j].astype(BF16), seg=seg)
            for s, (_, row_of) in enumerate(streams):
                r, kk, v, sg, bonus, lw, a, kz = _rwkv_in(xs[s], norm_w[i], mods, i, row_of, p)
                ys, finals = [], []
                for z in range(2):
                    s0 = _pair_blockdiag(state_rwkv[:, j, z]) if s == 1 else None
                    y, sf = _rwkv_scan(r, kk, v, lw, a, kz, z, s0)
                    ys.append(y)
                    finals.append(_pair_unblock(sf))
                if s == 0:
                    new_s.append(jnp.stack(finals, axis=1))
                xs[s] = _rwkv_out(ys[0], ys[1], bonus, sg, p, xs[s], mods, i, row_of)
        else:
            p = dict(dw_w=conv_dw_w[j], dw_b=conv_dw_b[j].reshape(1, E), ln_w=conv_ln_w[j].reshape(1, E),
                     ln_b=conv_ln_b[j].reshape(1, E), w_out=conv_w_out[j].astype(BF16))
            w_in = conv_w_in[j].astype(BF16)
            for s, (_, row_of) in enumerate(streams):
                zz, sg = _conv_in(xs[s], norm_w[i], mods, i, row_of, w_in)
                xs[s] = _conv_out(zz, sg, p, xs[s], mods, i, row_of)
    return (xs[0], xs[1], jnp.stack(new_k, axis=1), jnp.stack(new_v, axis=1), jnp.stack(new_s, axis=1))
```

```python
import functools
import math

import jax
import jax.numpy as jnp
from jax import lax
from jax.experimental import pallas as pl
from jax.experimental.pallas import tpu as pltpu

F32 = jnp.float32
BF16 = jnp.bfloat16

D_MODEL = 1024
D_INNER = 1024
DEPTH = 4
N_MIXERS = 3
GRID_W = 64
NORM_EPS = 1e-6
ATTN_HEAD_DIM = 64
ATTN_HEADS = 8
ATTN_V_DIM = 128
ROPE_BASE = 10000.0
RWKV_HEAD_DIM = 64
RWKV_HEADS = 16
RWKV_GN_EPS = RWKV_HEAD_DIM * 1e-5
LORA = 64
CONV_WIDTH = 31
CONV_PAD = CONV_WIDTH // 2

LANES = 128
SUBLANES = 8
MXU_DIM = 256
VMEM_LIMIT_BYTES = 56 * 1024 * 1024

ROW_TILE = 256
RWKV_ROW_TILE = 128
N_CHUNK = 512
SCAN_CHUNK = 64
HALO = 16
CONV_ROWS = 64
COND_ROWS = 16
DN = (((1,), (0,)), ((), ()))
DN_NT = (((1,), (1,)), ((), ()))


def _cparams(sem):
    return pltpu.CompilerParams(dimension_semantics=sem, vmem_limit_bytes=VMEM_LIMIT_BYTES)


def _mm(a, b, dn=DN):
    return lax.dot_general(a, b, dn, preferred_element_type=F32)


def _dot(a, b):
    return _mm(a.astype(BF16), b.astype(BF16))


def _split2(x):
    hi = x.astype(BF16)
    lo = (x - hi.astype(F32)).astype(BF16)
    return hi, lo


def _split3(x):
    hi = x.astype(BF16)
    r1 = x - hi.astype(F32)
    mid = r1.astype(BF16)
    lo = (r1 - mid.astype(F32)).astype(BF16)
    return hi, mid, lo


def _dot3(a, b):
    a_hi, a_lo = _split2(a)
    b_hi, b_lo = _split2(b)
    return _mm(a_hi, b_hi) + (_mm(a_lo, b_hi) + _mm(a_hi, b_lo))


def _dot_exact_lhs(m_bf16, x):
    hi, mid, lo = _split3(x)
    return _mm(m_bf16, hi) + (_mm(m_bf16, mid) + _mm(m_bf16, lo))


def _silu(x):
    return x * jax.nn.sigmoid(x)


def _normmod(x, nw, mod):
    ms = jnp.mean(x * x, axis=-1, keepdims=True)
    y = x * lax.rsqrt(ms + NORM_EPS) * nw
    return y * (1.0 + mod[:, D_MODEL:2 * D_MODEL]) + mod[:, 0:D_MODEL]


def _headsum(x, seg_ref):
    cols = []
    for c in range(0, x.shape[-1], MXU_DIM):
        hi, lo = _split2(x[:, c:c + MXU_DIM])
        cols.append(_mm(hi, seg_ref[...]) + _mm(lo, seg_ref[...]))
    return jnp.concatenate(cols, axis=-1)


def _mod_spec(layer, row_of):
    return pl.BlockSpec((None, None, 1, 3 * D_MODEL), lambda b, t: (layer, row_of(b), 0, 0))


def _const_spec(shape):
    return pl.BlockSpec(shape, lambda b, t: (0,) * len(shape))


def _ada_kernel(cond_ref, w_ref, b_ref, o_ref):
    o_ref[0, :, 0, :] = _dot3(_silu(cond_ref[...]), w_ref[0]) + b_ref[0]


def _ada(cond, ada_w, ada_b):
    depth, d, n3 = ada_w.shape
    tn = 1024
    return pl.pallas_call(
        _ada_kernel,
        grid=(depth, n3 // tn),
        in_specs=[
            pl.BlockSpec((COND_ROWS, d), lambda i, j: (0, 0)),
            pl.BlockSpec((1, d, tn), lambda i, j: (i, 0, j)),
            pl.BlockSpec((1, 1, tn), lambda i, j: (i, 0, j)),
        ],
        out_specs=pl.BlockSpec((1, COND_ROWS, 1, tn), lambda i, j: (i, 0, 0, j)),
        out_shape=jax.ShapeDtypeStruct((depth, COND_ROWS, 1, n3), F32),
        compiler_params=_cparams(("parallel", "parallel")),
        name="ada_mod",
    )(cond, ada_w, ada_b.reshape(depth, 1, n3))


def _attn_in_kernel(x_ref, nw_ref, mod_ref, w_ref, o_ref):
    hb = _normmod(x_ref[0], nw_ref[...], mod_ref[...]).astype(BF16)
    for n in range(0, 4 * D_INNER, N_CHUNK):
        o_ref[0, :, n:n + N_CHUNK] = _mm(hb, w_ref[:, n:n + N_CHUNK])


def _attn_in(x, nw, mods, layer, row_of, w_bf16):
    B, T, D = x.shape
    N = w_bf16.shape[1]
    tm = min(ROW_TILE, T)
    return pl.pallas_call(
        _attn_in_kernel,
        grid=(B, T // tm),
        in_specs=[
            pl.BlockSpec((1, tm, D), lambda b, t: (b, t, 0)),
            _const_spec((1, D)),
            _mod_spec(layer, row_of),
            _const_spec((D, N)),
        ],
        out_specs=pl.BlockSpec((1, tm, N), lambda b, t: (b, t, 0)),
        out_shape=jax.ShapeDtypeStruct((B, T, N), F32),
        compiler_params=_cparams(("parallel", "parallel")),
        name="attn_in",
    )(x, nw.reshape(1, D), mods, w_bf16)


def _rope(x, cos, sin):
    lane = lax.broadcasted_iota(jnp.int32, x.shape, 1)
    first = (lane % 32) < 16
    partner = jnp.where(first, pltpu.roll(x, LANES - 16, axis=1), pltpu.roll(x, 16, axis=1))
    return x * cos + partner * sin


def _softmax_pieces(pieces):
    m = None
    for s in pieces:
        pm = jnp.max(s, axis=-1, keepdims=True)
        m = pm if m is None else jnp.maximum(m, pm)
    es = [jnp.exp(s - m) for s in pieces]
    den = None
    for e in es:
        ps = jnp.sum(e, axis=-1, keepdims=True)
        den = ps if den is None else den + ps
    inv = 1.0 / den
    return [e * inv for e in es]


def _attn_core_kernel(*refs, lam_init, has_ctx, emit_cache, tq):
    it = iter(refs)
    lam_ref, q_ref, k_ref, v_ref, g_ref, subw_ref = (next(it) for _ in range(6))
    if has_ctx:
        ck_ref, cv_ref, cos_ref, sin_ref = (next(it) for _ in range(4))
    o_ref = next(it)
    if emit_cache:
        ko_ref, vo_ref = next(it), next(it)

    lv = lam_ref[...]
    lam = (jnp.exp(jnp.sum(lv[0:1] * lv[1:2], axis=-1, keepdims=True))
           - jnp.exp(jnp.sum(lv[2:3] * lv[3:4], axis=-1, keepdims=True)) + lam_init)

    k = k_ref[0]
    v = v_ref[0]
    if emit_cache:
        ko_ref[...] = k
        vo_ref[...] = v
    if has_ctx:
        k = _rope(k, cos_ref[...], sin_ref[...])
    keys = [k.astype(BF16)]
    vals = [v.astype(BF16)]
    if has_ctx:
        keys.append(ck_ref[...].astype(BF16))
        vals.append(cv_ref[...].astype(BF16))

    T = q_ref.shape[1]
    scale = ATTN_HEAD_DIM ** -0.5
    for q0 in range(0, T, tq):
        q = q_ref[0, q0:q0 + tq, :]
        if has_ctx:
            q = _rope(q, cos_ref[q0:q0 + tq, :], sin_ref[q0:q0 + tq, :])
        q = q * scale
        lane = lax.broadcasted_iota(jnp.int32, q.shape, 1)
        q1 = jnp.where(lane < ATTN_HEAD_DIM, q, 0.0).astype(BF16)
        q2 = jnp.where(lane >= ATTN_HEAD_DIM, q, 0.0).astype(BF16)
        p1 = _softmax_pieces([_mm(q1, kk, DN_NT) for kk in keys])
        p2 = _softmax_pieces([_mm(q2, kk, DN_NT) for kk in keys])
        o = None
        for a1, a2, vv in zip(p1, p2, vals):
            po = _mm((a1 - lam * a2).astype(BF16), vv)
            o = po if o is None else o + po
        ms = jnp.mean(o * o, axis=-1, keepdims=True)
        o = o * lax.rsqrt(ms + 1e-5) * subw_ref[...] * (1.0 - lam_init)
        o_ref[0, q0:q0 + tq, :] = o * _silu(g_ref[0, q0:q0 + tq, :])


def _attn_core(qkvg, lam_vecs, subw, layer_idx, ctx=None, emit_cache=False):
    B, T, _ = qkvg.shape
    H, DV = ATTN_HEADS, ATTN_V_DIM
    lam_init = 0.8 - 0.6 * math.exp(-0.3 * layer_idx)
    has_ctx = ctx is not None
    tq = min(ROW_TILE, T)

    def col(off):
        return pl.BlockSpec((1, T, DV), lambda b, h: (b, 0, off + h))

    in_specs = [pl.BlockSpec((4, ATTN_HEAD_DIM), lambda b, h: (0, 0)),
                col(0), col(H), col(2 * H), col(3 * H),
                pl.BlockSpec((1, DV), lambda b, h: (0, 0))]
    args = [lam_vecs, qkvg, qkvg, qkvg, qkvg, subw.reshape(1, DV)]
    if has_ctx:
        ck, cv, j, cos, sin = ctx
        P = ck.shape[3]
        cache_spec = pl.BlockSpec((None, None, None, P, DV), lambda b, h: (b, j, h, 0, 0))
        in_specs += [cache_spec, cache_spec,
                     pl.BlockSpec((T, DV), lambda b, h: (0, 0)),
                     pl.BlockSpec((T, DV), lambda b, h: (0, 0))]
        args += [ck, cv, cos, sin]
    out_specs = [pl.BlockSpec((1, T, DV), lambda b, h: (b, 0, h))]
    out_shape = [jax.ShapeDtypeStruct((B, T, D_INNER), F32)]
    if emit_cache:
        out_specs += [pl.BlockSpec((None, None, T, DV), lambda b, h: (b, h, 0, 0))] * 2
        out_shape += [jax.ShapeDtypeStruct((B, H, T, DV), F32)] * 2
    return pl.pallas_call(
        functools.partial(_attn_core_kernel, lam_init=lam_init, has_ctx=has_ctx,
                          emit_cache=emit_cache, tq=tq),
        grid=(B, H),
        in_specs=in_specs,
        out_specs=out_specs,
        out_shape=out_shape,
        compiler_params=_cparams(("parallel", "parallel")),
        name="attn_core",
    )(*args)


def _gated_residual(o, w_ref, x_ref, mod_ref):
    return x_ref[0] + mod_ref[:, 2 * D_MODEL:3 * D_MODEL] * _dot(o, w_ref[...])


def _out_proj_kernel(o_ref, w_ref, x_ref, mod_ref, *rest, final_norm):
    xn = _gated_residual(o_ref[0], w_ref, x_ref, mod_ref)
    if final_norm:
        fw_ref, y_ref = rest
        ms = jnp.mean(xn * xn, axis=-1, keepdims=True)
        xn = xn * lax.rsqrt(ms + NORM_EPS) * fw_ref[...]
    else:
        (y_ref,) = rest
    y_ref[0] = xn


def _out_proj(o, w_bf16, x, mods, layer, row_of, final_w=None):
    B, T, D = x.shape
    tm = min(ROW_TILE, T)
    tile = pl.BlockSpec((1, tm, D), lambda b, t: (b, t, 0))
    in_specs = [tile, _const_spec((D, D)), tile, _mod_spec(layer, row_of)]
    args = [o, w_bf16, x, mods]
    if final_w is not None:
        in_specs.append(_const_spec((1, D)))
        args.append(final_w.reshape(1, D))
    return pl.pallas_call(
        functools.partial(_out_proj_kernel, final_norm=final_w is not None),
        grid=(B, T // tm),
        in_specs=in_specs,
        out_specs=tile,
        out_shape=jax.ShapeDtypeStruct((B, T, D), F32),
        compiler_params=_cparams(("parallel", "parallel")),
        name="out_proj",
    )(*args)


def _rwkv_in_kernel(x_ref, xp_ref, xn_ref, nw_ref, mod_ref, mu_ref, w_ref, l1_ref, w2_ref, a2_ref,
                    w0_ref, a0_ref, kk_ref, ka_ref, rk_ref, seg_ref,
                    r_out, kk_out, v_out, sg_out, bonus_out, lw_out, a_out, kz_out):
    t = pl.program_id(1)
    nt = pl.num_programs(1)
    nw = nw_ref[...]
    mod = mod_ref[...]
    h = _normmod(x_ref[0], nw, mod)
    tm = h.shape[0]
    hp = _normmod(xp_ref[0], nw, mod)[SUBLANES - 1:SUBLANES, :]
    hn = _normmod(xn_ref[0], nw, mod)[0:1, :]
    hp = jnp.where(t > 0, hp, 0.0)
    hn = jnp.where(t < nt - 1, hn, 0.0)
    row = lax.broadcasted_iota(jnp.int32, h.shape, 0)
    h_prev = jnp.where(row == 0, hp, pltpu.roll(h, 1, axis=0))
    h_next = jnp.where(row == tm - 1, hn, pltpu.roll(h, tm - 1, axis=0))
    dx = 0.5 * (h_prev + h_next) - h

    def mix(n):
        return (h + dx * mu_ref[n:n + 1, :]).astype(BF16)

    E = D_INNER
    r = _mm(mix(0), w_ref[:, 0:E])
    k = _mm(mix(1), w_ref[:, E:2 * E])
    v = _mm(mix(2), w_ref[:, 2 * E:3 * E])
    g = _mm(mix(3), w_ref[:, 3 * E:4 * E])
    lw1 = jnp.tanh(_mm(mix(4), l1_ref[:, 0:2 * LORA]))
    la1 = _mm(mix(5), l1_ref[:, 2 * LORA:4 * LORA])
    lane = lax.broadcasted_iota(jnp.int32, lw1.shape, 1)

    kk = k * kk_ref[...]
    kk = kk * lax.rsqrt(jnp.maximum(_headsum(kk * kk, seg_ref), 1e-24))
    r_out[0] = r
    kk_out[0] = kk
    v_out[0] = v
    sg_out[0] = _silu(g)

    ksum = None
    for z in range(2):
        sel = (lane >= z * LORA) & (lane < (z + 1) * LORA)
        lw = _dot(jnp.where(sel, lw1, 0.0), w2_ref[...])
        la = _dot(jnp.where(sel, la1, 0.0), a2_ref[...])
        u = -(w0_ref[z:z + 1, :] + lw)
        softplus = jnp.maximum(u, 0.0) + jnp.log(1.0 + jnp.exp(-jnp.abs(u)))
        lw_out[z, 0] = -jnp.exp(-softplus - 0.5)
        a = jax.nn.sigmoid(a0_ref[z:z + 1, :] + la)
        a_out[z, 0] = a
        kz = k * (1.0 + (a - 1.0) * ka_ref[...])
        kz_out[z, 0] = kz
        ksum = kz if ksum is None else ksum + kz
    bonus_out[0] = _headsum(r * rk_ref[...] * ksum, seg_ref) * v


def _rwkv_in(x, nw, mods, layer, row_of, p):
    B, T, D = x.shape
    E = D_INNER
    tm = min(RWKV_ROW_TILE, T)
    g8 = tm // SUBLANES
    last8 = T // SUBLANES - 1
    tile = pl.BlockSpec((1, tm, D), lambda b, t: (b, t, 0))
    prev8 = pl.BlockSpec((1, SUBLANES, D), lambda b, t: (b, jnp.maximum(t * g8 - 1, 0), 0))
    next8 = pl.BlockSpec((1, SUBLANES, D), lambda b, t: (b, jnp.minimum((t + 1) * g8, last8), 0))
    otile = pl.BlockSpec((1, tm, E), lambda b, t: (b, t, 0))
    ztile = pl.BlockSpec((2, 1, tm, E), lambda b, t: (0, b, t, 0))
    one = jax.ShapeDtypeStruct((B, T, E), F32)
    two = jax.ShapeDtypeStruct((2, B, T, E), F32)
    return pl.pallas_call(
        _rwkv_in_kernel,
        grid=(B, T // tm),
        in_specs=[tile, prev8, next8, _const_spec((1, D)), _mod_spec(layer, row_of),
                  _const_spec((6, D)), _const_spec((D, 4 * E)), _const_spec((D, 4 * LORA)),
                  _const_spec((2 * LORA, E)), _const_spec((2 * LORA, E)),
                  _const_spec((2, E)), _const_spec((2, E)),
                  _const_spec((1, E)), _const_spec((1, E)), _const_spec((1, E)),
                  _const_spec((MXU_DIM, MXU_DIM))],
        out_specs=[otile] * 5 + [ztile] * 3,
        out_shape=[one] * 5 + [two] * 3,
        compiler_params=_cparams(("parallel", "parallel")),
        name="rwkv_in",
    )(x, x, x, nw.reshape(1, D), mods, p["mu"], p["w_in"], p["l1"], p["w2"], p["a2"],
      p["w0"], p["a0"], p["k_k"], p["k_a"], p["r_k"], p["seg"])


def _scan_kernel(*refs, reverse, has_init):
    if has_init:
        r_ref, kk_ref, v_ref, lw_ref, a_ref, kz_ref, s0_ref, y_ref, sf_ref, s_scr = refs
    else:
        r_ref, kk_ref, v_ref, lw_ref, a_ref, kz_ref, y_ref, sf_ref, s_scr = refs
    C = SCAN_CHUNK
    PW = 2 * C
    HD = RWKV_HEAD_DIM
    NP = r_ref.shape[-1] // LANES
    pairs = range(NP)

    @pl.when(pl.program_id(1) == 0)
    def _():
        s_scr[...] = s0_ref[...] if has_init else jnp.zeros_like(s_scr)

    ri = lax.broadcasted_iota(jnp.int32, (PW, PW), 0)
    ci = lax.broadcasted_iota(jnp.int32, (PW, PW), 1)
    same_blk = (ri // C) == (ci // C)
    tr, tc = ri % C, ci % C
    before = (tc > tr) if reverse else (tc < tr)
    strict = same_blk & before
    incl = same_blk & (before | (tc == tr))
    same_head = (ri // HD) == (ci // HD)
    diag = ri == ci
    eye = jnp.where(diag, 1.0, 0.0)
    li = lax.broadcasted_iota(jnp.int32, (C, C), 0)
    lj = lax.broadcasted_iota(jnp.int32, (C, C), 1)
    cum_mat = jnp.where((lj >= li) if reverse else (lj <= li), 1.0, 0.0).astype(BF16)
    lo = lax.broadcasted_iota(jnp.int32, (C, LANES), 1) < HD
    zeros_c = jnp.zeros((C, LANES), F32)
    zeros_p = jnp.zeros((PW, LANES), F32)

    def stack2(x):
        return jnp.concatenate([x, x], axis=0)

    def unstack(x2):
        return jnp.where(lo, x2[0:C], x2[C:PW])

    def cols(ref4, p):
        return ref4[0, 0, :, p * LANES:(p + 1) * LANES]

    lw = [cols(lw_ref, p) for p in pairs]
    cum = [_dot_exact_lhs(cum_mat, lw[p]) for p in pairs]
    tot = [(c[0:1] if reverse else c[C - 1:C]) for c in cum]
    abar, rbar, bh, kh, v, res = [], [], [], [], [], []
    for p in pairs:
        sl = slice(p * LANES, (p + 1) * LANES)
        kk = kk_ref[0, :, sl]
        b = kk * cols(a_ref, p)
        kz = cols(kz_ref, p)
        e_neg = jnp.exp(-cum[p])
        e_tot = jnp.exp(tot[p] - cum[p])
        abar.append(-kk * jnp.exp(cum[p] - lw[p]))
        rbar.append(r_ref[0, :, sl] * jnp.exp(cum[p]))
        bt, kt = b * e_neg, kz * e_neg
        bh.append(b * e_tot)
        kh.append(kz * e_tot)
        v.append(v_ref[0, :, sl])
        lhs = jnp.concatenate([abar[p], rbar[p]], axis=0)
        rhs = jnp.concatenate([jnp.where(lo, bt, 0.0), jnp.where(lo, 0.0, bt),
                               jnp.where(lo, kt, 0.0), jnp.where(lo, 0.0, kt)], axis=0)
        res.append(_mm(lhs.astype(BF16), rhs.astype(BF16), DN_NT))
    a_ab = [jnp.where(strict, stack2(x[0:C, 0:LANES]), 0.0) for x in res]
    a_ak = [jnp.where(strict, stack2(x[0:C, LANES:2 * LANES]), 0.0) for x in res]
    a_rb = [jnp.where(incl, stack2(x[C:PW, 0:LANES]), 0.0) for x in res]
    a_rk = [jnp.where(incl, stack2(x[C:PW, LANES:2 * LANES]), 0.0) for x in res]
    v2 = [stack2(x) for x in v]
    wv = [_dot(a_ak[p], v2[p]) for p in pairs]

    pw = [_dot(x, x) for x in a_ab]
    q = [eye + x for x in a_ab]
    n = 4
    while n < C:
        m = [_dot(pw[p], jnp.concatenate([q[p], pw[p]], axis=1)) for p in pairs]
        q = [q[p] + m[p][:, 0:LANES] for p in pairs]
        pw = [x[:, LANES:2 * LANES] for x in m]
        n *= 2
    q = [q[p] + _dot(pw[p], q[p]) for p in pairs]

    x = [_dot(q[p], jnp.concatenate([stack2(abar[p]), wv[p]], axis=1)) for p in pairs]
    ahat = [unstack(t[:, 0:LANES]) for t in x]
    u0 = [unstack(t[:, LANES:2 * LANES]) for t in x]

    ry = [_dot(jnp.concatenate([a_rb[p], a_rk[p]], axis=1),
               jnp.concatenate([jnp.concatenate([stack2(ahat[p]), stack2(u0[p])], axis=1),
                                jnp.concatenate([zeros_p, v2[p]], axis=1)], axis=0)) for p in pairs]
    rhat = [rbar[p] + unstack(ry[p][:, 0:LANES]) for p in pairs]
    y0 = [unstack(t[:, LANES:2 * LANES]) for t in ry]

    mn = [_dot(jnp.concatenate([bh[p], kh[p]], axis=0).T,
               jnp.concatenate([jnp.concatenate([ahat[p], u0[p]], axis=1),
                                jnp.concatenate([zeros_c, v[p]], axis=1)], axis=0)) for p in pairs]
    for p in pairs:
        m_c = jnp.where(same_head, mn[p][:, 0:LANES], 0.0) + jnp.where(diag, jnp.exp(tot[p]), 0.0)
        n_c = jnp.where(same_head, mn[p][:, LANES:2 * LANES], 0.0)
        s_old = s_scr[p]
        y_ref[0, :, p * LANES:(p + 1) * LANES] = _dot(rhat[p], s_old) + y0[p]
        s_new = _dot(m_c, s_old) + n_c
        s_scr[p] = s_new
        sf_ref[p] = s_new


def _rwkv_scan(r, kk, v, lw, a, kz, z, s0=None):
    B, T, E = r.shape
    NP = E // LANES
    C = SCAN_CHUNK
    nb = T // C
    reverse = z == 1

    def tok(t):
        return (nb - 1 - t) if reverse else t

    one = pl.BlockSpec((1, C, E), lambda b, t: (b, tok(t), 0))
    two = pl.BlockSpec((1, 1, C, E), lambda b, t: (z, b, tok(t), 0))
    state = pl.BlockSpec((None, NP, LANES, LANES), lambda b, t: (b, 0, 0, 0))
    in_specs = [one, one, one, two, two, two]
    args = [r, kk, v, lw, a, kz]
    if s0 is not None:
        in_specs.append(state)
        args.append(s0)
    return pl.pallas_call(
        functools.partial(_scan_kernel, reverse=reverse, has_init=s0 is not None),
        grid=(B, nb),
        in_specs=in_specs,
        out_specs=[one, state],
        out_shape=[jax.ShapeDtypeStruct((B, T, E), F32),
                   jax.ShapeDtypeStruct((B, NP, LANES, LANES), F32)],
        scratch_shapes=[pltpu.VMEM((NP, LANES, LANES), F32)],
        compiler_params=_cparams(("parallel", "arbitrary")),
        name="rwkv_scan",
    )(*args)


def _rwkv_out_kernel(y0_ref, y1_ref, bonus_ref, sg_ref, lnw_ref, lnb_ref, seg_ref, w_ref, x_ref, mod_ref, o_ref):
    y = y0_ref[0] + y1_ref[0]
    inv_n = 1.0 / RWKV_HEAD_DIM
    d = y - _headsum(y, seg_ref) * inv_n
    var = _headsum(d * d, seg_ref) * inv_n
    yn = d * lax.rsqrt(var + RWKV_GN_EPS) * lnw_ref[...] + lnb_ref[...] + bonus_ref[0]
    o_ref[0] = _gated_residual(yn * sg_ref[0], w_ref, x_ref, mod_ref)


def _rwkv_out(y0, y1, bonus, sg, p, x, mods, layer, row_of):
    B, T, D = x.shape
    tm = min(ROW_TILE, T)
    tile = pl.BlockSpec((1, tm, D), lambda b, t: (b, t, 0))
    return pl.pallas_call(
        _rwkv_out_kernel,
        grid=(B, T // tm),
        in_specs=[tile, tile, tile, tile, _const_spec((1, D)), _const_spec((1, D)),
                  _const_spec((MXU_DIM, MXU_DIM)), _const_spec((D, D)), tile, _mod_spec(layer, row_of)],
        out_specs=tile,
        out_shape=jax.ShapeDtypeStruct((B, T, D), F32),
        compiler_params=_cparams(("parallel", "parallel")),
        name="rwkv_out",
    )(y0, y1, bonus, sg, p["ln_w"], p["ln_b"], p["seg"], p["w_out"], x, mods)


def _conv_in_kernel(x_ref, nw_ref, mod_ref, w_ref, z_ref, sg_ref):
    hb = _normmod(x_ref[0], nw_ref[...], mod_ref[...]).astype(BF16)
    E = D_INNER
    for n in range(0, E, N_CHUNK):
        a = _mm(hb, w_ref[:, n:n + N_CHUNK])
        b = _mm(hb, w_ref[:, E + n:E + n + N_CHUNK])
        g = _mm(hb, w_ref[:, 2 * E + n:2 * E + n + N_CHUNK])
        z_ref[0, :, n:n + N_CHUNK] = a * jax.nn.sigmoid(b)
        sg_ref[0, :, n:n + N_CHUNK] = _silu(g)


def _conv_in(x, nw, mods, layer, row_of, w_bf16):
    B, T, D = x.shape
    E = D_INNER
    tm = min(ROW_TILE, T)
    tile = pl.BlockSpec((1, tm, D), lambda b, t: (b, t, 0))
    return pl.pallas_call(
        _conv_in_kernel,
        grid=(B, T // tm),
        in_specs=[tile, _const_spec((1, D)), _mod_spec(layer, row_of), _const_spec((D, 3 * E))],
        out_specs=[tile, tile],
        out_shape=[jax.ShapeDtypeStruct((B, T, E), F32)] * 2,
        compiler_params=_cparams(("parallel", "parallel")),
        name="conv_in",
    )(x, nw.reshape(1, D), mods, w_bf16)


def _conv_out_kernel(z_ref, zp_ref, zn_ref, sg_ref, dw_ref, dwb_ref, lnw_ref, lnb_ref, w_ref, x_ref, mod_ref,
                     o_ref, zpad, cbuf):
    t = pl.program_id(1)
    nt = pl.num_programs(1)
    tm = z_ref.shape[1]
    zpad[0:HALO, :] = jnp.where(t > 0, zp_ref[0], 0.0)
    zpad[HALO:HALO + tm, :] = z_ref[0]
    zpad[HALO + tm:HALO + tm + HALO, :] = jnp.where(t < nt - 1, zn_ref[0], 0.0)
    off = HALO - CONV_PAD
    for c0 in range(0, D_INNER, LANES):
        for r0 in range(0, tm, CONV_ROWS):
            acc = None
            for k in range(CONV_WIDTH):
                term = dw_ref[k:k + 1, c0:c0 + LANES] * zpad[r0 + off + k:r0 + off + k + CONV_ROWS, c0:c0 + LANES]
                acc = term if acc is None else acc + term
            cbuf[r0:r0 + CONV_ROWS, c0:c0 + LANES] = acc
    c = cbuf[...] + dwb_ref[...]
    m = jnp.mean(c, axis=-1, keepdims=True)
    d = c - m
    var = jnp.mean(d * d, axis=-1, keepdims=True)
    y = d * lax.rsqrt(var + 1e-5) * lnw_ref[...] + lnb_ref[...]
    o_ref[0] = _gated_residual(_silu(y) * sg_ref[0], w_ref, x_ref, mod_ref)


def _conv_out(z, sg, p, x, mods, layer, row_of):
    B, T, D = x.shape
    tm = min(ROW_TILE, T)
    gh = tm // HALO
    lasth = T // HALO - 1
    tile = pl.BlockSpec((1, tm, D), lambda b, t: (b, t, 0))
    prev = pl.BlockSpec((1, HALO, D), lambda b, t: (b, jnp.maximum(t * gh - 1, 0), 0))
    nxt = pl.BlockSpec((1, HALO, D), lambda b, t: (b, jnp.minimum((t + 1) * gh, lasth), 0))
    return pl.pallas_call(
        _conv_out_kernel,
        grid=(B, T // tm),
        in_specs=[tile, prev, nxt, tile, _const_spec((CONV_WIDTH, D)), _const_spec((1, D)),
                  _const_spec((1, D)), _const_spec((1, D)), _const_spec((D, D)), tile,
                  _mod_spec(layer, row_of)],
        out_specs=tile,
        out_shape=jax.ShapeDtypeStruct((B, T, D), F32),
        scratch_shapes=[pltpu.VMEM((tm + 2 * HALO, D), F32), pltpu.VMEM((tm, D), F32)],
        compiler_params=_cparams(("parallel", "parallel")),
        name="conv_out",
    )(z, z, z, sg, p["dw_w"], p["dw_b"], p["ln_w"], p["ln_b"], p["w_out"], x, mods)


def _rope_tables(n_tok):
    t = jnp.arange(n_tok, dtype=jnp.int32)
    pos = jnp.stack([(t // GRID_W).astype(F32), (t % GRID_W).astype(F32)], axis=1)
    lane = jnp.arange(LANES, dtype=jnp.int32)
    d = lane % ATTN_HEAD_DIM
    axis = d // 32
    second_half = (d % 32) // 16
    axis_dim = ATTN_HEAD_DIM // 2
    inv_freq = ROPE_BASE ** (-(2.0 * (d % 16).astype(F32)) / axis_dim)
    ang = pos[:, axis] * inv_freq[None, :]
    sign = jnp.where(second_half == 1, 1.0, -1.0).astype(F32)
    return jnp.cos(ang), jnp.sin(ang) * sign[None, :]


def _pair_blockdiag(s):
    B, H, N, _ = s.shape
    st = jnp.swapaxes(s, -1, -2).reshape(B, H // 2, 2, N, N)
    z = jnp.zeros_like(st[:, :, 0])
    top = jnp.concatenate([st[:, :, 0], z], axis=-1)
    bot = jnp.concatenate([z, st[:, :, 1]], axis=-1)
    return jnp.concatenate([top, bot], axis=-2)


def _pair_unblock(sp):
    B, NP, _, _ = sp.shape
    N = RWKV_HEAD_DIM
    blocks = jnp.stack([sp[:, :, 0:N, 0:N], sp[:, :, N:2 * N, N:2 * N]], axis=2)
    return jnp.swapaxes(blocks, -1, -2).reshape(B, 2 * NP, N, N)


def kernel(x_prompt, x_sample, cache_attn_k, cache_attn_v, state_rwkv, c, c_ctx, norm_w, ada_w, ada_b, attn_w_in, attn_lambda, attn_subln_w, attn_w_out, rwkv_mu, rwkv_w_in, rwkv_w0, rwkv_w1, rwkv_w2, rwkv_a0, rwkv_a1, rwkv_a2, rwkv_k_k, rwkv_k_a, rwkv_r_k, rwkv_ln_w, rwkv_ln_b, rwkv_w_out, conv_w_in, conv_dw_w, conv_dw_b, conv_ln_w, conv_ln_b, conv_w_out, final_norm_w):
    D, E = D_MODEL, D_INNER
    dec_batch = x_sample.shape[0]
    assert dec_batch < COND_ROWS
    cond = jnp.concatenate([c, c_ctx[None, :], jnp.zeros((COND_ROWS - dec_batch - 1, D), F32)], axis=0)
    mods = _ada(cond, ada_w, ada_b)
    ctx_row = dec_batch
    streams = [(x_prompt, lambda b: ctx_row), (x_sample, lambda b: b)]
    cos, sin = _rope_tables(x_sample.shape[1])
    seg = (jnp.arange(MXU_DIM)[:, None] // RWKV_HEAD_DIM == jnp.arange(MXU_DIM)[None, :] // RWKV_HEAD_DIM).astype(BF16)

    xs = [x_prompt, x_sample]
    new_k, new_v, new_s = [], [], []
    for i in range(DEPTH):
        kind, j = i % N_MIXERS, i // N_MIXERS
        last = i == DEPTH - 1
        fw = final_norm_w if last else None
        if kind == 0:
            w_in = attn_w_in[j].astype(BF16)
            w_out = attn_w_out[j].astype(BF16)
            for s, (_, row_of) in enumerate(streams):
                qkvg = _attn_in(xs[s], norm_w[i], mods, i, row_of, w_in)
                if s == 0:
                    o, kc, vc = _attn_core(qkvg, attn_lambda[j], attn_subln_w[j], i, emit_cache=True)
                    new_k.append(kc)
                    new_v.append(vc)
                else:
                    (o,) = _attn_core(qkvg, attn_lambda[j], attn_subln_w[j], i,
                                      ctx=(cache_attn_k, cache_attn_v, j, cos, sin))
                xs[s] = _out_proj(o, w_out, xs[s], mods, i, row_of, final_w=fw)
        elif kind == 1:
            p = dict(
                mu=rwkv_mu[j],
                w_in=rwkv_w_in[j].astype(BF16),
                l1=jnp.concatenate([rwkv_w1[j, 0], rwkv_w1[j, 1], rwkv_a1[j, 0], rwkv_a1[j, 1]], axis=1).astype(BF16),
                w2=rwkv_w2[j].reshape(2 * LORA, E).astype(BF16),
                a2=rwkv_a2[j].reshape(2 * LORA, E).astype(BF16),
                w0=rwkv_w0[j], a0=rwkv_a0[j],
                k_k=rwkv_k_k[j].reshape(1, E), k_a=rwkv_k_a[j].reshape(1, E), r_k=rwkv_r_k[j].reshape(1, E),
                ln_w=rwkv_ln_w[j].reshape(1, E), ln_b=rwkv_ln_b[j].reshape(1, E),
                w_out=rwkv_w_out[j].astype(BF16), seg=seg)
            for s, (_, row_of) in enumerate(streams):
                r, kk, v, sg, bonus, lw, a, kz = _rwkv_in(xs[s], norm_w[i], mods, i, row_of, p)
                ys, finals = [], []
                for z in range(2):
                    s0 = _pair_blockdiag(state_rwkv[:, j, z]) if s == 1 else None
                    y, sf = _rwkv_scan(r, kk, v, lw, a, kz, z, s0)
                    ys.append(y)
                    finals.append(_pair_unblock(sf))
                if s == 0:
                    new_s.append(jnp.stack(finals, axis=1))
                xs[s] = _rwkv_out(ys[0], ys[1], bonus, sg, p, xs[s], mods, i, row_of)
        else:
            p = dict(dw_w=conv_dw_w[j], dw_b=conv_dw_b[j].reshape(1, E), ln_w=conv_ln_w[j].reshape(1, E),
                     ln_b=conv_ln_b[j].reshape(1, E), w_out=conv_w_out[j].astype(BF16))
            w_in = conv_w_in[j].astype(BF16)
            for s, (_, row_of) in enumerate(streams):
                zz, sg = _conv_in(xs[s], norm_w[i], mods, i, row_of, w_in)
                xs[s] = _conv_out(zz, sg, p, xs[s], mods, i, row_of)
    return (xs[0], xs[1], jnp.stack(new_k, axis=1), jnp.stack(new_v, axis=1), jnp.stack(new_s, axis=1))
```

```python
import functools
import math

import jax
import jax.numpy as jnp
from jax import lax
from jax.experimental import pallas as pl
from jax.experimental.pallas import tpu as pltpu

F32 = jnp.float32
BF16 = jnp.bfloat16

D_MODEL = 1024
D_INNER = 1024
DEPTH = 4
N_MIXERS = 3
GRID_W = 64
NORM_EPS = 1e-6
ATTN_HEAD_DIM = 64
ATTN_HEADS = 8
ATTN_V_DIM = 128
ROPE_BASE = 10000.0
RWKV_HEAD_DIM = 64
RWKV_HEADS = 16
RWKV_GN_EPS = RWKV_HEAD_DIM * 1e-5
LORA = 64
CONV_WIDTH = 31
CONV_PAD = CONV_WIDTH // 2

LANES = 128
SUBLANES = 8
MXU_DIM = 256
VMEM_LIMIT_BYTES = 56 * 1024 * 1024

ROW_TILE = 256
RWKV_ROW_TILE = 128
ATTN_Q_TILE = 256
ATTN_ROWS_PER_STEP = 2048
N_CHUNK = 512
SCAN_CHUNK = 64
HALO = 16
CONV_ROWS = 64
COND_ROWS = 16
DN = (((1,), (0,)), ((), ()))
DN_NT = (((1,), (1,)), ((), ()))


def _cparams(sem):
    return pltpu.CompilerParams(dimension_semantics=sem, vmem_limit_bytes=VMEM_LIMIT_BYTES)


def _mm(a, b, dn=DN):
    return lax.dot_general(a, b, dn, preferred_element_type=F32)


def _dot(a, b):
    return _mm(a.astype(BF16), b.astype(BF16))


def _split2(x):
    hi = x.astype(BF16)
    lo = (x - hi.astype(F32)).astype(BF16)
    return hi, lo


def _split3(x):
    hi = x.astype(BF16)
    r1 = x - hi.astype(F32)
    mid = r1.astype(BF16)
    lo = (r1 - mid.astype(F32)).astype(BF16)
    return hi, mid, lo


def _dot3(a, b):
    a_hi, a_lo = _split2(a)
    b_hi, b_lo = _split2(b)
    return _mm(a_hi, b_hi) + (_mm(a_lo, b_hi) + _mm(a_hi, b_lo))


def _dot_exact_lhs(m_bf16, x):
    hi, mid, lo = _split3(x)
    return _mm(m_bf16, hi) + (_mm(m_bf16, mid) + _mm(m_bf16, lo))


def _silu(x):
    return x * jax.nn.sigmoid(x)


def _normmod(x, nw, mod):
    ms = jnp.mean(x * x, axis=-1, keepdims=True)
    y = x * lax.rsqrt(ms + NORM_EPS) * nw
    return y * (1.0 + mod[:, D_MODEL:2 * D_MODEL]) + mod[:, 0:D_MODEL]


def _headsum(x, seg_ref):
    cols = []
    for c in range(0, x.shape[-1], MXU_DIM):
        hi, lo = _split2(x[:, c:c + MXU_DIM])
        cols.append(_mm(hi, seg_ref[...]) + _mm(lo, seg_ref[...]))
    return jnp.concatenate(cols, axis=-1)


def _mod_spec(layer, row_of):
    return pl.BlockSpec((None, None, 1, 3 * D_MODEL), lambda b, t: (layer, row_of(b), 0, 0))


def _const_spec(shape):
    return pl.BlockSpec(shape, lambda b, t: (0,) * len(shape))


def _ada_kernel(cond_ref, w_ref, b_ref, o_ref):
    o_ref[0, :, 0, :] = _dot3(_silu(cond_ref[...]), w_ref[0]) + b_ref[0]


def _ada(cond, ada_w, ada_b):
    depth, d, n3 = ada_w.shape
    tn = 1024
    return pl.pallas_call(
        _ada_kernel,
        grid=(depth, n3 // tn),
        in_specs=[
            pl.BlockSpec((COND_ROWS, d), lambda i, j: (0, 0)),
            pl.BlockSpec((1, d, tn), lambda i, j: (i, 0, j)),
            pl.BlockSpec((1, 1, tn), lambda i, j: (i, 0, j)),
        ],
        out_specs=pl.BlockSpec((1, COND_ROWS, 1, tn), lambda i, j: (i, 0, 0, j)),
        out_shape=jax.ShapeDtypeStruct((depth, COND_ROWS, 1, n3), F32),
        compiler_params=_cparams(("parallel", "parallel")),
        name="ada_mod",
    )(cond, ada_w, ada_b.reshape(depth, 1, n3))


def _attn_in_kernel(x_ref, nw_ref, mod_ref, w_ref, *rest, rope):
    if rope:
        cos_ref, sin_ref, o_ref = rest
    else:
        (o_ref,) = rest
    hb = _normmod(x_ref[0], nw_ref[...], mod_ref[...]).astype(BF16)
    for n in range(0, 4 * D_INNER, N_CHUNK):
        y = _mm(hb, w_ref[:, n:n + N_CHUNK])
        if rope and n < 2 * D_INNER:
            y = jnp.concatenate([_rope(y[:, c:c + LANES], cos_ref[...], sin_ref[...])
                                 for c in range(0, N_CHUNK, LANES)], axis=1)
        o_ref[0, :, n:n + N_CHUNK] = y


def _attn_in(x, nw, mods, layer, row_of, w_bf16, rope=None):
    B, T, D = x.shape
    N = w_bf16.shape[1]
    tm = min(ROW_TILE, T)
    in_specs = [pl.BlockSpec((1, tm, D), lambda b, t: (b, t, 0)), _const_spec((1, D)),
                _mod_spec(layer, row_of), _const_spec((D, N))]
    args = [x, nw.reshape(1, D), mods, w_bf16]
    if rope is not None:
        in_specs += [pl.BlockSpec((tm, LANES), lambda b, t: (t, 0))] * 2
        args += list(rope)
    return pl.pallas_call(
        functools.partial(_attn_in_kernel, rope=rope is not None),
        grid=(B, T // tm),
        in_specs=in_specs,
        out_specs=pl.BlockSpec((1, tm, N), lambda b, t: (b, t, 0)),
        out_shape=jax.ShapeDtypeStruct((B, T, N), F32),
        compiler_params=_cparams(("parallel", "parallel")),
        name="attn_in",
    )(*args)


def _rope(x, cos, sin):
    lane = lax.broadcasted_iota(jnp.int32, x.shape, 1)
    first = (lane % 32) < 16
    partner = jnp.where(first, pltpu.roll(x, LANES - 16, axis=1), pltpu.roll(x, 16, axis=1))
    return x * cos + partner * sin


def _attn_core_kernel(*refs, lam_init, has_ctx, emit_cache, tq):
    it = iter(refs)
    lam_ref, q_ref, k_ref, v_ref, g_ref, subw_ref = (next(it) for _ in range(6))
    if has_ctx:
        ck_ref, cv_ref = next(it), next(it)
    if emit_cache:
        pk_ref, pv_ref = next(it), next(it)
    o_ref = next(it)
    if emit_cache is not None:
        ko_ref, vo_ref = next(it), next(it)
        if emit_cache:
            ko_ref[0:emit_cache] = pk_ref[...]
            vo_ref[0:emit_cache] = pv_ref[...]
    T = q_ref.shape[1]
    n_heads = q_ref.shape[2] // LANES
    n_tiles = T // tq
    k_scr = [next(it) for _ in range(n_heads)]
    v_scr = [next(it) for _ in range(n_heads)]
    s_scr = [[next(it), next(it)] for _ in range(2)]
    e_scr = [[next(it), next(it)] for _ in range(2)]
    m_scr = [[next(it), next(it)] for _ in range(2)]

    lv = lam_ref[...]
    lam = (jnp.exp(jnp.sum(lv[0:1] * lv[1:2], axis=-1, keepdims=True))
           - jnp.exp(jnp.sum(lv[2:3] * lv[3:4], axis=-1, keepdims=True)) + lam_init)

    tk = k_scr[0].shape[0]
    ones_col = lax.broadcasted_iota(jnp.int32, (tk, LANES), 1) == 0
    for hh in range(n_heads):
        hl = slice(hh * LANES, (hh + 1) * LANES)
        k = k_ref[0, :, hl]
        v = v_ref[0, :, hl]
        if emit_cache is not None:
            ko_ref[emit_cache, hh] = k
            vo_ref[emit_cache, hh] = v
        if has_ctx:
            k_scr[hh][T:, :] = ck_ref[hh].astype(BF16)
            v_scr[hh][T:, 0:LANES] = cv_ref[hh].astype(BF16)
        k_scr[hh][0:T, :] = k.astype(BF16)
        v_scr[hh][0:T, 0:LANES] = v.astype(BF16)
        v_scr[hh][:, LANES:2 * LANES] = jnp.where(ones_col, 1.0, 0.0).astype(BF16)

    q_scale = ATTN_HEAD_DIM ** -0.5 * math.log2(math.e)
    kblocks = [slice(c, c + MXU_DIM) for c in range(0, tk, MXU_DIM)]
    tasks = [(hh, i) for hh in range(n_heads) for i in range(n_tiles)]

    def halves_max(x):
        return jnp.maximum(x[:, 0:LANES], x[:, LANES:2 * LANES])

    def score_stage(n):
        hh, i = tasks[n]
        q = q_ref[0, i * tq:(i + 1) * tq, hh * LANES:(hh + 1) * LANES] * q_scale
        lane = lax.broadcasted_iota(jnp.int32, q.shape, 1)
        for mp in range(2):
            sel = (lane < ATTN_HEAD_DIM) if mp == 0 else (lane >= ATTN_HEAD_DIM)
            qm = jnp.where(sel, q, 0.0).astype(BF16)
            mx = None
            for kb in kblocks:
                s = _mm(qm, k_scr[hh][kb, :], DN_NT)
                s_scr[n % 2][mp][:, kb] = s
                mx = halves_max(s) if mx is None else jnp.maximum(mx, halves_max(s))
                yield
            m_scr[n % 2][mp][...] = jnp.max(mx, axis=-1, keepdims=True)

    def attend_stage(n):
        hh, i = tasks[n]
        outs = []
        for mp in range(2):
            m = m_scr[n % 2][mp][...]
            for kb in kblocks:
                e_scr[n % 2][mp][:, kb] = jnp.exp2(s_scr[n % 2][mp][:, kb] - m).astype(BF16)
                yield
            ov = _mm(e_scr[n % 2][mp][...], v_scr[hh][...])
            outs.append((ov[:, 0:LANES], ov[:, LANES:LANES + 1]))
            yield
        (o1, d1), (o2, d2) = outs
        o = o1 * (1.0 / d1) - o2 * (lam / d2)
        ms = jnp.mean(o * o, axis=-1, keepdims=True)
        o = o * lax.rsqrt(ms + 1e-5) * subw_ref[...] * (1.0 - lam_init)
        rows, hl = slice(i * tq, (i + 1) * tq), slice(hh * LANES, (hh + 1) * LANES)
        o_ref[0, rows, hl] = o * _silu(g_ref[0, rows, hl])

    for n in range(len(tasks) + 1):
        live = []
        if n < len(tasks):
            live.append(score_stage(n))
        if n >= 1:
            live.append(attend_stage(n - 1))
        while live:
            for gen in list(live):
                if next(gen, "done") == "done":
                    live.remove(gen)


def _attn_core(qkvg, lam_vecs, subw, layer_idx, ctx=None, cache=None):
    B, T, _ = qkvg.shape
    H, DV = ATTN_HEADS, ATTN_V_DIM
    HS = min(H, max(1, ATTN_ROWS_PER_STEP // T))
    lam_init = 0.8 - 0.6 * math.exp(-0.3 * layer_idx)
    has_ctx = ctx is not None
    tq = min(ATTN_Q_TILE, T)
    tk = T + (ctx[0].shape[3] if has_ctx else 0)
    emit_cache = None
    if cache is not None:
        emit_cache = 0 if cache[0] is None else cache[0].shape[1]

    def col(off):
        return pl.BlockSpec((1, T, HS * DV), lambda b, h: (b, 0, off + h))

    nb = H // HS
    in_specs = [pl.BlockSpec((4, ATTN_HEAD_DIM), lambda b, h: (0, 0)),
                col(0), col(nb), col(2 * nb), col(3 * nb),
                pl.BlockSpec((1, DV), lambda b, h: (0, 0))]
    args = [lam_vecs, qkvg, qkvg, qkvg, qkvg, subw.reshape(1, DV)]
    if has_ctx:
        ck, cv, j = ctx
        P = ck.shape[3]
        cache_spec = pl.BlockSpec((None, None, HS, P, DV), lambda b, h: (b, j, h, 0, 0))
        in_specs += [cache_spec, cache_spec]
        args += [ck, cv]
    out_specs = [pl.BlockSpec((1, T, HS * DV), lambda b, h: (b, 0, h))]
    out_shape = [jax.ShapeDtypeStruct((B, T, D_INNER), F32)]
    if emit_cache is not None:
        if emit_cache:
            in_specs += [pl.BlockSpec((None, emit_cache, HS, T, DV), lambda b, h: (b, 0, h, 0, 0))] * 2
            args += list(cache)
        out_specs += [pl.BlockSpec((None, emit_cache + 1, HS, T, DV), lambda b, h: (b, 0, h, 0, 0))] * 2
        out_shape += [jax.ShapeDtypeStruct((B, emit_cache + 1, H, T, DV), F32)] * 2
    return pl.pallas_call(
        functools.partial(_attn_core_kernel, lam_init=lam_init, has_ctx=has_ctx,
                          emit_cache=emit_cache, tq=tq),
        grid=(B, nb),
        in_specs=in_specs,
        out_specs=out_specs,
        out_shape=out_shape,
        scratch_shapes=([pltpu.VMEM((tk, DV), BF16)] * HS + [pltpu.VMEM((tk, 2 * DV), BF16)] * HS
                        + [pltpu.VMEM((tq, tk), F32)] * 4 + [pltpu.VMEM((tq, tk), BF16)] * 4
                        + [pltpu.VMEM((tq, 1), F32)] * 4),
        compiler_params=_cparams(("parallel", "parallel")),
        name="attn_core",
    )(*args)


def _gated_residual(o, w_ref, x_ref, mod_ref):
    return x_ref[0] + mod_ref[:, 2 * D_MODEL:3 * D_MODEL] * _dot(o, w_ref[...])


def _out_proj_kernel(o_ref, w_ref, x_ref, mod_ref, *rest, final_norm):
    xn = _gated_residual(o_ref[0], w_ref, x_ref, mod_ref)
    if final_norm:
        fw_ref, y_ref = rest
        ms = jnp.mean(xn * xn, axis=-1, keepdims=True)
        xn = xn * lax.rsqrt(ms + NORM_EPS) * fw_ref[...]
    else:
        (y_ref,) = rest
    y_ref[0] = xn


def _out_proj(o, w_bf16, x, mods, layer, row_of, final_w=None):
    B, T, D = x.shape
    tm = min(ROW_TILE, T)
    tile = pl.BlockSpec((1, tm, D), lambda b, t: (b, t, 0))
    in_specs = [tile, _const_spec((D, D)), tile, _mod_spec(layer, row_of)]
    args = [o, w_bf16, x, mods]
    if final_w is not None:
        in_specs.append(_const_spec((1, D)))
        args.append(final_w.reshape(1, D))
    return pl.pallas_call(
        functools.partial(_out_proj_kernel, final_norm=final_w is not None),
        grid=(B, T // tm),
        in_specs=in_specs,
        out_specs=tile,
        out_shape=jax.ShapeDtypeStruct((B, T, D), F32),
        compiler_params=_cparams(("parallel", "parallel")),
        name="out_proj",
    )(*args)


def _rwkv_in_kernel(x_ref, xp_ref, xn_ref, nw_ref, mod_ref, mu_ref, w_ref, l1_ref, w2_ref, a2_ref,
                    w0_ref, a0_ref, kk_ref, ka_ref, rk_ref, seg_ref,
                    r_out, kk_out, v_out, sg_out, bonus_out, lw_out, a_out, kz_out):
    t = pl.program_id(1)
    nt = pl.num_programs(1)
    nw = nw_ref[...]
    mod = mod_ref[...]
    h = _normmod(x_ref[0], nw, mod)
    tm = h.shape[0]
    hp = _normmod(xp_ref[0], nw, mod)[SUBLANES - 1:SUBLANES, :]
    hn = _normmod(xn_ref[0], nw, mod)[0:1, :]
    hp = jnp.where(t > 0, hp, 0.0)
    hn = jnp.where(t < nt - 1, hn, 0.0)
    row = lax.broadcasted_iota(jnp.int32, h.shape, 0)
    h_prev = jnp.where(row == 0, hp, pltpu.roll(h, 1, axis=0))
    h_next = jnp.where(row == tm - 1, hn, pltpu.roll(h, tm - 1, axis=0))
    dx = 0.5 * (h_prev + h_next) - h

    def mix(n):
        return (h + dx * mu_ref[n:n + 1, :]).astype(BF16)

    E = D_INNER
    r = _mm(mix(0), w_ref[:, 0:E])
    k = _mm(mix(1), w_ref[:, E:2 * E])
    v = _mm(mix(2), w_ref[:, 2 * E:3 * E])
    g = _mm(mix(3), w_ref[:, 3 * E:4 * E])
    lw1 = jnp.tanh(_mm(mix(4), l1_ref[:, 0:2 * LORA]))
    la1 = _mm(mix(5), l1_ref[:, 2 * LORA:4 * LORA])
    lane = lax.broadcasted_iota(jnp.int32, lw1.shape, 1)

    kk = k * kk_ref[...]
    kk = kk * lax.rsqrt(jnp.maximum(_headsum(kk * kk, seg_ref), 1e-24))
    r_out[0] = r
    kk_out[0] = kk
    v_out[0] = v
    sg_out[0] = _silu(g)

    ksum = None
    for z in range(2):
        sel = (lane >= z * LORA) & (lane < (z + 1) * LORA)
        lw = _dot(jnp.where(sel, lw1, 0.0), w2_ref[...])
        la = _dot(jnp.where(sel, la1, 0.0), a2_ref[...])
        u = -(w0_ref[z:z + 1, :] + lw)
        softplus = jnp.maximum(u, 0.0) + jnp.log(1.0 + jnp.exp(-jnp.abs(u)))
        lw_out[z, 0] = -jnp.exp(-softplus - 0.5)
        a = jax.nn.sigmoid(a0_ref[z:z + 1, :] + la)
        a_out[z, 0] = a
        kz = k * (1.0 + (a - 1.0) * ka_ref[...])
        kz_out[z, 0] = kz
        ksum = kz if ksum is None else ksum + kz
    bonus_out[0] = _headsum(r * rk_ref[...] * ksum, seg_ref) * v


def _rwkv_in(x, nw, mods, layer, row_of, p):
    B, T, D = x.shape
    E = D_INNER
    tm = min(RWKV_ROW_TILE, T)
    g8 = tm // SUBLANES
    last8 = T // SUBLANES - 1
    tile = pl.BlockSpec((1, tm, D), lambda b, t: (b, t, 0))
    prev8 = pl.BlockSpec((1, SUBLANES, D), lambda b, t: (b, jnp.maximum(t * g8 - 1, 0), 0))
    next8 = pl.BlockSpec((1, SUBLANES, D), lambda b, t: (b, jnp.minimum((t + 1) * g8, last8), 0))
    otile = pl.BlockSpec((1, tm, E), lambda b, t: (b, t, 0))
    ztile = pl.BlockSpec((2, 1, tm, E), lambda b, t: (0, b, t, 0))
    one = jax.ShapeDtypeStruct((B, T, E), F32)
    two = jax.ShapeDtypeStruct((2, B, T, E), F32)
    return pl.pallas_call(
        _rwkv_in_kernel,
        grid=(B, T // tm),
        in_specs=[tile, prev8, next8, _const_spec((1, D)), _mod_spec(layer, row_of),
                  _const_spec((6, D)), _const_spec((D, 4 * E)), _const_spec((D, 4 * LORA)),
                  _const_spec((2 * LORA, E)), _const_spec((2 * LORA, E)),
                  _const_spec((2, E)), _const_spec((2, E)),
                  _const_spec((1, E)), _const_spec((1, E)), _const_spec((1, E)),
                  _const_spec((MXU_DIM, MXU_DIM))],
        out_specs=[otile] * 5 + [ztile] * 3,
        out_shape=[one] * 5 + [two] * 3,
        compiler_params=_cparams(("parallel", "parallel")),
        name="rwkv_in",
    )(x, x, x, nw.reshape(1, D), mods, p["mu"], p["w_in"], p["l1"], p["w2"], p["a2"],
      p["w0"], p["a0"], p["k_k"], p["k_a"], p["r_k"], p["seg"])


def _scan_kernel(*refs, reverse, has_init):
    if has_init:
        r_ref, kk_ref, v_ref, lw_ref, a_ref, kz_ref, s0_ref, y_ref, sf_ref, s_scr = refs
    else:
        r_ref, kk_ref, v_ref, lw_ref, a_ref, kz_ref, y_ref, sf_ref, s_scr = refs
    C = SCAN_CHUNK
    PW = 2 * C
    HD = RWKV_HEAD_DIM
    NP = r_ref.shape[-1] // LANES
    pairs = range(NP)

    @pl.when(pl.program_id(1) == 0)
    def _():
        s_scr[...] = s0_ref[...] if has_init else jnp.zeros_like(s_scr)

    ri = lax.broadcasted_iota(jnp.int32, (PW, PW), 0)
    ci = lax.broadcasted_iota(jnp.int32, (PW, PW), 1)
    same_blk = (ri // C) == (ci // C)
    tr, tc = ri % C, ci % C
    before = (tc > tr) if reverse else (tc < tr)
    strict = same_blk & before
    incl = same_blk & (before | (tc == tr))
    same_head = (ri // HD) == (ci // HD)
    diag = ri == ci
    eye = jnp.where(diag, 1.0, 0.0)
    li = lax.broadcasted_iota(jnp.int32, (C, C), 0)
    lj = lax.broadcasted_iota(jnp.int32, (C, C), 1)
    cum_mat = jnp.where((lj >= li) if reverse else (lj <= li), 1.0, 0.0).astype(BF16)
    lo = lax.broadcasted_iota(jnp.int32, (C, LANES), 1) < HD
    zeros_c = jnp.zeros((C, LANES), F32)
    zeros_p = jnp.zeros((PW, LANES), F32)

    def stack2(x):
        return jnp.concatenate([x, x], axis=0)

    def unstack(x2):
        return jnp.where(lo, x2[0:C], x2[C:PW])

    def cols(ref4, p):
        return ref4[0, 0, :, p * LANES:(p + 1) * LANES]

    lw = [cols(lw_ref, p) for p in pairs]
    cum = [_dot_exact_lhs(cum_mat, lw[p]) for p in pairs]
    tot = [(c[0:1] if reverse else c[C - 1:C]) for c in cum]
    abar, rbar, bh, kh, v, res = [], [], [], [], [], []
    for p in pairs:
        sl = slice(p * LANES, (p + 1) * LANES)
        kk = kk_ref[0, :, sl]
        b = kk * cols(a_ref, p)
        kz = cols(kz_ref, p)
        e_neg = jnp.exp(-cum[p])
        e_tot = jnp.exp(tot[p] - cum[p])
        abar.append(-kk * jnp.exp(cum[p] - lw[p]))
        rbar.append(r_ref[0, :, sl] * jnp.exp(cum[p]))
        bt, kt = b * e_neg, kz * e_neg
        bh.append(b * e_tot)
        kh.append(kz * e_tot)
        v.append(v_ref[0, :, sl])
        lhs = jnp.concatenate([abar[p], rbar[p]], axis=0)
        rhs = jnp.concatenate([jnp.where(lo, bt, 0.0), jnp.where(lo, 0.0, bt),
                               jnp.where(lo, kt, 0.0), jnp.where(lo, 0.0, kt)], axis=0)
        res.append(_mm(lhs.astype(BF16), rhs.astype(BF16), DN_NT))
    a_ab = [jnp.where(strict, stack2(x[0:C, 0:LANES]), 0.0) for x in res]
    a_ak = [jnp.where(strict, stack2(x[0:C, LANES:2 * LANES]), 0.0) for x in res]
    a_rb = [jnp.where(incl, stack2(x[C:PW, 0:LANES]), 0.0) for x in res]
    a_rk = [jnp.where(incl, stack2(x[C:PW, LANES:2 * LANES]), 0.0) for x in res]
    v2 = [stack2(x) for x in v]
    wv = [_dot(a_ak[p], v2[p]) for p in pairs]

    pw = [_dot(x, x) for x in a_ab]
    q = [eye + x for x in a_ab]
    n = 4
    while n < C:
        m = [_dot(pw[p], jnp.concatenate([q[p], pw[p]], axis=1)) for p in pairs]
        q = [q[p] + m[p][:, 0:LANES] for p in pairs]
        pw = [x[:, LANES:2 * LANES] for x in m]
        n *= 2
    q = [q[p] + _dot(pw[p], q[p]) for p in pairs]

    x = [_dot(q[p], jnp.concatenate([stack2(abar[p]), wv[p]], axis=1)) for p in pairs]
    ahat = [unstack(t[:, 0:LANES]) for t in x]
    u0 = [unstack(t[:, LANES:2 * LANES]) for t in x]

    ry = [_dot(jnp.concatenate([a_rb[p], a_rk[p]], axis=1),
               jnp.concatenate([jnp.concatenate([stack2(ahat[p]), stack2(u0[p])], axis=1),
                                jnp.concatenate([zeros_p, v2[p]], axis=1)], axis=0)) for p in pairs]
    rhat = [rbar[p] + unstack(ry[p][:, 0:LANES]) for p in pairs]
    y0 = [unstack(t[:, LANES:2 * LANES]) for t in ry]

    mn = [_dot(jnp.concatenate([bh[p], kh[p]], axis=0).T,
               jnp.concatenate([jnp.concatenate([ahat[p], u0[p]], axis=1),
                                jnp.concatenate([zeros_c, v[p]], axis=1)], axis=0)) for p in pairs]
    for p in pairs:
        m_c = jnp.where(same_head, mn[p][:, 0:LANES], 0.0) + jnp.where(diag, jnp.exp(tot[p]), 0.0)
        n_c = jnp.where(same_head, mn[p][:, LANES:2 * LANES], 0.0)
        s_old = s_scr[p]
        y_ref[0, :, p * LANES:(p + 1) * LANES] = _dot(rhat[p], s_old) + y0[p]
        s_new = _dot(m_c, s_old) + n_c
        s_scr[p] = s_new
        sf_ref[p] = s_new


def _rwkv_scan(r, kk, v, lw, a, kz, z, s0=None):
    B, T, E = r.shape
    NP = E // LANES
    C = SCAN_CHUNK
    nb = T // C
    reverse = z == 1

    def tok(t):
        return (nb - 1 - t) if reverse else t

    one = pl.BlockSpec((1, C, E), lambda b, t: (b, tok(t), 0))
    two = pl.BlockSpec((1, 1, C, E), lambda b, t: (z, b, tok(t), 0))
    state = pl.BlockSpec((None, NP, LANES, LANES), lambda b, t: (b, 0, 0, 0))
    in_specs = [one, one, one, two, two, two]
    args = [r, kk, v, lw, a, kz]
    if s0 is not None:
        in_specs.append(state)
        args.append(s0)
    return pl.pallas_call(
        functools.partial(_scan_kernel, reverse=reverse, has_init=s0 is not None),
        grid=(B, nb),
        in_specs=in_specs,
        out_specs=[one, state],
        out_shape=[jax.ShapeDtypeStruct((B, T, E), F32),
                   jax.ShapeDtypeStruct((B, NP, LANES, LANES), F32)],
        scratch_shapes=[pltpu.VMEM((NP, LANES, LANES), F32)],
        compiler_params=_cparams(("parallel", "arbitrary")),
        name="rwkv_scan",
    )(*args)


def _rwkv_out_kernel(y0_ref, y1_ref, bonus_ref, sg_ref, lnw_ref, lnb_ref, seg_ref, w_ref, x_ref, mod_ref, o_ref):
    y = y0_ref[0] + y1_ref[0]
    inv_n = 1.0 / RWKV_HEAD_DIM
    d = y - _headsum(y, seg_ref) * inv_n
    var = _headsum(d * d, seg_ref) * inv_n
    yn = d * lax.rsqrt(var + RWKV_GN_EPS) * lnw_ref[...] + lnb_ref[...] + bonus_ref[0]
    o_ref[0] = _gated_residual(yn * sg_ref[0], w_ref, x_ref, mod_ref)


def _rwkv_out(y0, y1, bonus, sg, p, x, mods, layer, row_of):
    B, T, D = x.shape
    tm = min(ROW_TILE, T)
    tile = pl.BlockSpec((1, tm, D), lambda b, t: (b, t, 0))
    return pl.pallas_call(
        _rwkv_out_kernel,
        grid=(B, T // tm),
        in_specs=[tile, tile, tile, tile, _const_spec((1, D)), _const_spec((1, D)),
                  _const_spec((MXU_DIM, MXU_DIM)), _const_spec((D, D)), tile, _mod_spec(layer, row_of)],
        out_specs=tile,
        out_shape=jax.ShapeDtypeStruct((B, T, D), F32),
        compiler_params=_cparams(("parallel", "parallel")),
        name="rwkv_out",
    )(y0, y1, bonus, sg, p["ln_w"], p["ln_b"], p["seg"], p["w_out"], x, mods)


def _conv_in_kernel(x_ref, nw_ref, mod_ref, w_ref, z_ref, sg_ref):
    hb = _normmod(x_ref[0], nw_ref[...], mod_ref[...]).astype(BF16)
    E = D_INNER
    for n in range(0, E, N_CHUNK):
        a = _mm(hb, w_ref[:, n:n + N_CHUNK])
        b = _mm(hb, w_ref[:, E + n:E + n + N_CHUNK])
        g = _mm(hb, w_ref[:, 2 * E + n:2 * E + n + N_CHUNK])
        z_ref[0, :, n:n + N_CHUNK] = a * jax.nn.sigmoid(b)
        sg_ref[0, :, n:n + N_CHUNK] = _silu(g)


def _conv_in(x, nw, mods, layer, row_of, w_bf16):
    B, T, D = x.shape
    E = D_INNER
    tm = min(ROW_TILE, T)
    tile = pl.BlockSpec((1, tm, D), lambda b, t: (b, t, 0))
    return pl.pallas_call(
        _conv_in_kernel,
        grid=(B, T // tm),
        in_specs=[tile, _const_spec((1, D)), _mod_spec(layer, row_of), _const_spec((D, 3 * E))],
        out_specs=[tile, tile],
        out_shape=[jax.ShapeDtypeStruct((B, T, E), F32)] * 2,
        compiler_params=_cparams(("parallel", "parallel")),
        name="conv_in",
    )(x, nw.reshape(1, D), mods, w_bf16)


def _conv_out_kernel(z_ref, zp_ref, zn_ref, sg_ref, dw_ref, dwb_ref, lnw_ref, lnb_ref, w_ref, x_ref, mod_ref,
                     o_ref, zs, cbuf):
    t = pl.program_id(1)
    nt = pl.num_programs(1)
    tm = z_ref.shape[1]
    rows = tm + 2 * HALO
    zs[0, 0:HALO, :] = jnp.where(t > 0, zp_ref[0], 0.0)
    zs[0, HALO:HALO + tm, :] = z_ref[0]
    zs[0, HALO + tm:rows, :] = jnp.where(t < nt - 1, zn_ref[0], 0.0)
    for s in range(1, SUBLANES):
        zs[s, 0:rows - SUBLANES, :] = zs[0, s:s + rows - SUBLANES, :]
    off = HALO - CONV_PAD
    for c0 in range(0, D_INNER, LANES):
        for r0 in range(0, tm, CONV_ROWS):
            acc = None
            for k in range(CONV_WIDTH):
                q8, s = divmod(off + k, SUBLANES)
                a0 = r0 + q8 * SUBLANES
                term = dw_ref[k:k + 1, c0:c0 + LANES] * zs[s, a0:a0 + CONV_ROWS, c0:c0 + LANES]
                acc = term if acc is None else acc + term
            cbuf[r0:r0 + CONV_ROWS, c0:c0 + LANES] = acc
    c = cbuf[...] + dwb_ref[...]
    m = jnp.mean(c, axis=-1, keepdims=True)
    d = c - m
    var = jnp.mean(d * d, axis=-1, keepdims=True)
    y = d * lax.rsqrt(var + 1e-5) * lnw_ref[...] + lnb_ref[...]
    o_ref[0] = _gated_residual(_silu(y) * sg_ref[0], w_ref, x_ref, mod_ref)


def _conv_out(z, sg, p, x, mods, layer, row_of):
    B, T, D = x.shape
    tm = min(ROW_TILE, T)
    gh = tm // HALO
    lasth = T // HALO - 1
    tile = pl.BlockSpec((1, tm, D), lambda b, t: (b, t, 0))
    prev = pl.BlockSpec((1, HALO, D), lambda b, t: (b, jnp.maximum(t * gh - 1, 0), 0))
    nxt = pl.BlockSpec((1, HALO, D), lambda b, t: (b, jnp.minimum((t + 1) * gh, lasth), 0))
    return pl.pallas_call(
        _conv_out_kernel,
        grid=(B, T // tm),
        in_specs=[tile, prev, nxt, tile, _const_spec((CONV_WIDTH, D)), _const_spec((1, D)),
                  _const_spec((1, D)), _const_spec((1, D)), _const_spec((D, D)), tile,
                  _mod_spec(layer, row_of)],
        out_specs=tile,
        out_shape=jax.ShapeDtypeStruct((B, T, D), F32),
        scratch_shapes=[pltpu.VMEM((SUBLANES, tm + 2 * HALO, D), F32), pltpu.VMEM((tm, D), F32)],
        compiler_params=_cparams(("parallel", "parallel")),
        name="conv_out",
    )(z, z, z, sg, p["dw_w"], p["dw_b"], p["ln_w"], p["ln_b"], p["w_out"], x, mods)


def _rope_tables(n_tok):
    t = jnp.arange(n_tok, dtype=jnp.int32)
    pos = jnp.stack([(t // GRID_W).astype(F32), (t % GRID_W).astype(F32)], axis=1)
    lane = jnp.arange(LANES, dtype=jnp.int32)
    d = lane % ATTN_HEAD_DIM
    axis = d // 32
    second_half = (d % 32) // 16
    axis_dim = ATTN_HEAD_DIM // 2
    inv_freq = ROPE_BASE ** (-(2.0 * (d % 16).astype(F32)) / axis_dim)
    ang = pos[:, axis] * inv_freq[None, :]
    sign = jnp.where(second_half == 1, 1.0, -1.0).astype(F32)
    return jnp.cos(ang), jnp.sin(ang) * sign[None, :]


def _pair_blockdiag(s):
    B, H, N, _ = s.shape
    st = jnp.swapaxes(s, -1, -2).reshape(B, H // 2, 2, N, N)
    z = jnp.zeros_like(st[:, :, 0])
    top = jnp.concatenate([st[:, :, 0], z], axis=-1)
    bot = jnp.concatenate([z, st[:, :, 1]], axis=-1)
    return jnp.concatenate([top, bot], axis=-2)


def _pair_unblock(sp):
    B, NP, _, _ = sp.shape
    N = RWKV_HEAD_DIM
    blocks = jnp.stack([sp[:, :, 0:N, 0:N], sp[:, :, N:2 * N, N:2 * N]], axis=2)
    return jnp.swapaxes(blocks, -1, -2).reshape(B, 2 * NP, N, N)


def kernel(x_prompt, x_sample, cache_attn_k, cache_attn_v, state_rwkv, c, c_ctx, norm_w, ada_w, ada_b, attn_w_in, attn_lambda, attn_subln_w, attn_w_out, rwkv_mu, rwkv_w_in, rwkv_w0, rwkv_w1, rwkv_w2, rwkv_a0, rwkv_a1, rwkv_a2, rwkv_k_k, rwkv_k_a, rwkv_r_k, rwkv_ln_w, rwkv_ln_b, rwkv_w_out, conv_w_in, conv_dw_w, conv_dw_b, conv_ln_w, conv_ln_b, conv_w_out, final_norm_w):
    D, E = D_MODEL, D_INNER
    dec_batch = x_sample.shape[0]
    assert dec_batch < COND_ROWS
    cond = jnp.concatenate([c, c_ctx[None, :], jnp.zeros((COND_ROWS - dec_batch - 1, D), F32)], axis=0)
    mods = _ada(cond, ada_w, ada_b)
    ctx_row = dec_batch
    streams = [(x_prompt, lambda b: ctx_row), (x_sample, lambda b: b)]
    cos, sin = _rope_tables(x_sample.shape[1])
    seg = (jnp.arange(MXU_DIM)[:, None] // RWKV_HEAD_DIM == jnp.arange(MXU_DIM)[None, :] // RWKV_HEAD_DIM).astype(BF16)

    xs = [x_prompt, x_sample]
    new_kv, new_s = (None, None), []
    assert (DEPTH - 1) % N_MIXERS == 0
    for i in range(DEPTH):
        kind, j = i % N_MIXERS, i // N_MIXERS
        last = i == DEPTH - 1
        fw = final_norm_w if last else None
        if kind == 0:
            w_in = attn_w_in[j].astype(BF16)
            w_out = attn_w_out[j].astype(BF16)
            for s, (_, row_of) in enumerate(streams):
                if s == 0:
                    qkvg = _attn_in(xs[s], norm_w[i], mods, i, row_of, w_in)
                    o, *new_kv = _attn_core(qkvg, attn_lambda[j], attn_subln_w[j], i, cache=new_kv)
                else:
                    qkvg = _attn_in(xs[s], norm_w[i], mods, i, row_of, w_in, rope=(cos, sin))
                    (o,) = _attn_core(qkvg, attn_lambda[j], attn_subln_w[j], i,
                                      ctx=(cache_attn_k, cache_attn_v, j))
                xs[s] = _out_proj(o, w_out, xs[s], mods, i, row_of, final_w=fw)
        elif kind == 1:
            p = dict(
                mu=rwkv_mu[j],
                w_in=rwkv_w_in[j].astype(BF16),
                l1=jnp.concatenate([rwkv_w1[j, 0], rwkv_w1[j, 1], rwkv_a1[j, 0], rwkv_a1[j, 1]], axis=1).astype(BF16),
                w2=rwkv_w2[j].reshape(2 * LORA, E).astype(BF16),
                a2=rwkv_a2[j].reshape(2 * LORA, E).astype(BF16),
                w0=rwkv_w0[j], a0=rwkv_a0[j],
                k_k=rwkv_k_k[j].reshape(1, E), k_a=rwkv_k_a[j].reshape(1, E), r_k=rwkv_r_k[j].reshape(1, E),
                ln_w=rwkv_ln_w[j].reshape(1, E), ln_b=rwkv_ln_b[j].reshape(1, E),
                w_out=rwkv_w_out[j].astype(BF16), seg=seg)
            for s, (_, row_of) in enumerate(streams):
                r, kk, v, sg, bonus, lw, a, kz = _rwkv_in(xs[s], norm_w[i], mods, i, row_of, p)
                ys, finals = [], []
                for z in range(2):
                    s0 = _pair_blockdiag(state_rwkv[:, j, z]) if s == 1 else None
                    y, sf = _rwkv_scan(r, kk, v, lw, a, kz, z, s0)
                    ys.append(y)
                    finals.append(_pair_unblock(sf))
                if s == 0:
                    new_s.append(jnp.stack(finals, axis=1))
                xs[s] = _rwkv_out(ys[0], ys[1], bonus, sg, p, xs[s], mods, i, row_of)
        else:
            p = dict(dw_w=conv_dw_w[j], dw_b=conv_dw_b[j].reshape(1, E), ln_w=conv_ln_w[j].reshape(1, E),
                     ln_b=conv_ln_b[j].reshape(1, E), w_out=conv_w_out[j].astype(BF16))
            w_in = conv_w_in[j].astype(BF16)
            for s, (_, row_of) in enumerate(streams):
                zz, sg = _conv_in(xs[s], norm_w[i], mods, i, row_of, w_in)
                xs[s] = _conv_out(zz, sg, p, xs[s], mods, i, row_of)
    return (xs[0], xs[1], new_kv[0], new_kv[1], jnp.stack(new_s, axis=1))
```

```python
import functools
import math

import jax
import jax.numpy as jnp
from jax import lax
from jax.experimental import pallas as pl
from jax.experimental.pallas import tpu as pltpu

F32 = jnp.float32
BF16 = jnp.bfloat16

D_MODEL = 1024
D_INNER = 1024
DEPTH = 4
N_MIXERS = 3
GRID_W = 64
NORM_EPS = 1e-6
ATTN_HEAD_DIM = 64
ATTN_HEADS = 8
ATTN_V_DIM = 128
ROPE_BASE = 10000.0
RWKV_HEAD_DIM = 64
RWKV_HEADS = 16
RWKV_GN_EPS = RWKV_HEAD_DIM * 1e-5
LORA = 64
CONV_WIDTH = 31
CONV_PAD = CONV_WIDTH // 2

LANES = 128
SUBLANES = 8
MXU_DIM = 256
VMEM_LIMIT_BYTES = 56 * 1024 * 1024

ROW_TILE = 256
RWKV_ROW_TILE = 128
ATTN_Q_TILE = 256
ATTN_ROWS_PER_STEP = 2048
N_CHUNK = 512
SCAN_CHUNK = 64
HALO = 16
CONV_ROWS = 64
COND_ROWS = 16
DN = (((1,), (0,)), ((), ()))
DN_NT = (((1,), (1,)), ((), ()))


def _cparams(sem):
    return pltpu.CompilerParams(dimension_semantics=sem, vmem_limit_bytes=VMEM_LIMIT_BYTES)


def _mm(a, b, dn=DN):
    return lax.dot_general(a, b, dn, preferred_element_type=F32)


def _dot(a, b):
    return _mm(a.astype(BF16), b.astype(BF16))


def _split2(x):
    hi = x.astype(BF16)
    lo = (x - hi.astype(F32)).astype(BF16)
    return hi, lo


def _dot3(a, b):
    a_hi, a_lo = _split2(a)
    b_hi, b_lo = _split2(b)
    return _mm(a_hi, b_hi) + (_mm(a_lo, b_hi) + _mm(a_hi, b_lo))


def _silu(x):
    return x * jax.nn.sigmoid(x)


def _normmod(x, nw, mod):
    ms = jnp.mean(x * x, axis=-1, keepdims=True)
    y = x * lax.rsqrt(ms + NORM_EPS) * nw
    return y * (1.0 + mod[:, D_MODEL:2 * D_MODEL]) + mod[:, 0:D_MODEL]


def _headsum(x, seg_ref):
    cols = []
    for c in range(0, x.shape[-1], MXU_DIM):
        hi, lo = _split2(x[:, c:c + MXU_DIM])
        cols.append(_mm(hi, seg_ref[...]) + _mm(lo, seg_ref[...]))
    return jnp.concatenate(cols, axis=-1)


def _mod_spec(layer, row_of):
    return pl.BlockSpec((None, None, 1, 3 * D_MODEL), lambda b, t: (layer, row_of(b), 0, 0))


def _const_spec(shape):
    return pl.BlockSpec(shape, lambda b, t: (0,) * len(shape))


def _ada_kernel(cond_ref, w_ref, b_ref, o_ref):
    o_ref[0, :, 0, :] = _dot3(_silu(cond_ref[...]), w_ref[0]) + b_ref[0]


def _ada(cond, ada_w, ada_b):
    depth, d, n3 = ada_w.shape
    tn = 1024
    return pl.pallas_call(
        _ada_kernel,
        grid=(depth, n3 // tn),
        in_specs=[
            pl.BlockSpec((COND_ROWS, d), lambda i, j: (0, 0)),
            pl.BlockSpec((1, d, tn), lambda i, j: (i, 0, j)),
            pl.BlockSpec((1, 1, tn), lambda i, j: (i, 0, j)),
        ],
        out_specs=pl.BlockSpec((1, COND_ROWS, 1, tn), lambda i, j: (i, 0, 0, j)),
        out_shape=jax.ShapeDtypeStruct((depth, COND_ROWS, 1, n3), F32),
        compiler_params=_cparams(("parallel", "parallel")),
        name="ada_mod",
    )(cond, ada_w, ada_b.reshape(depth, 1, n3))


ATTN_Q_SCALE = ATTN_HEAD_DIM ** -0.5 * math.log2(math.e)


def _attn_in_kernel(x_ref, nw_ref, mod_ref, w_ref, *rest, rope, n_cached):
    rest = list(rest)
    if rope:
        cos_ref, sin_ref = rest.pop(0), rest.pop(0)
    if n_cached:
        pk_ref, pv_ref = rest.pop(0), rest.pop(0)
    o_ref = rest.pop(0)
    if n_cached is not None:
        ko_ref, vo_ref = rest
        if n_cached:
            ko_ref[0:n_cached] = pk_ref[...]
            vo_ref[0:n_cached] = pv_ref[...]
    E = D_INNER
    hb = _normmod(x_ref[0], nw_ref[...], mod_ref[...]).astype(BF16)
    for n in range(0, 4 * E, N_CHUNK):
        y = _mm(hb, w_ref[:, n:n + N_CHUNK])
        if n_cached is not None and E <= n < 3 * E:
            cache_ref = ko_ref if n < 2 * E else vo_ref
            for c in range(0, N_CHUNK, LANES):
                cache_ref[n_cached, (n % E + c) // LANES] = y[:, c:c + LANES]
        if n < E:
            y = y * ATTN_Q_SCALE
        if rope and n < 2 * E:
            y = jnp.concatenate([_rope(y[:, c:c + LANES], cos_ref[...], sin_ref[...])
                                 for c in range(0, N_CHUNK, LANES)], axis=1)
        o_ref[0, :, n:n + N_CHUNK] = y.astype(BF16)


def _attn_in(x, nw, mods, layer, row_of, w_bf16, rope=None, cache=None):
    B, T, D = x.shape
    N = w_bf16.shape[1]
    H, DV = ATTN_HEADS, ATTN_V_DIM
    tm = min(ROW_TILE, T)
    in_specs = [pl.BlockSpec((1, tm, D), lambda b, t: (b, t, 0)), _const_spec((1, D)),
                _mod_spec(layer, row_of), _const_spec((D, N))]
    args = [x, nw.reshape(1, D), mods, w_bf16]
    if rope is not None:
        in_specs += [pl.BlockSpec((tm, LANES), lambda b, t: (t, 0))] * 2
        args += list(rope)
    out_specs = [pl.BlockSpec((1, tm, N), lambda b, t: (b, t, 0))]
    out_shape = [jax.ShapeDtypeStruct((B, T, N), BF16)]
    n_cached = None
    if cache is not None:
        n_cached = 0 if cache[0] is None else cache[0].shape[1]
        if n_cached:
            in_specs += [pl.BlockSpec((None, n_cached, H, tm, DV), lambda b, t: (b, 0, 0, t, 0))] * 2
            args += list(cache)
        out_specs += [pl.BlockSpec((None, n_cached + 1, H, tm, DV), lambda b, t: (b, 0, 0, t, 0))] * 2
        out_shape += [jax.ShapeDtypeStruct((B, n_cached + 1, H, T, DV), F32)] * 2
    return pl.pallas_call(
        functools.partial(_attn_in_kernel, rope=rope is not None, n_cached=n_cached),
        grid=(B, T // tm),
        in_specs=in_specs,
        out_specs=out_specs,
        out_shape=out_shape,
        compiler_params=_cparams(("parallel", "parallel")),
        name="attn_in",
    )(*args)


def _rope(x, cos, sin):
    lane = lax.broadcasted_iota(jnp.int32, x.shape, 1)
    first = (lane % 32) < 16
    partner = jnp.where(first, pltpu.roll(x, LANES - 16, axis=1), pltpu.roll(x, 16, axis=1))
    return x * cos + partner * sin


def _attn_core_kernel(*refs, lam_init, has_ctx, tq):
    it = iter(refs)
    lam_ref, q_ref, k_ref, v_ref, g_ref, subw_ref = (next(it) for _ in range(6))
    if has_ctx:
        ck_ref, cv_ref = next(it), next(it)
    o_ref = next(it)
    T = q_ref.shape[1]
    n_heads = q_ref.shape[2] // LANES
    n_tiles = T // tq
    k_scr = [next(it) for _ in range(n_heads)]
    v_scr = [next(it) for _ in range(n_heads)]
    s_scr = [[next(it), next(it)] for _ in range(2)]
    e_scr = [[next(it), next(it)] for _ in range(2)]
    m_scr = [[next(it), next(it)] for _ in range(2)]

    lv = lam_ref[...]
    lam = (jnp.exp(jnp.sum(lv[0:1] * lv[1:2], axis=-1, keepdims=True))
           - jnp.exp(jnp.sum(lv[2:3] * lv[3:4], axis=-1, keepdims=True)) + lam_init)

    tk = k_scr[0].shape[0]
    ones_col = lax.broadcasted_iota(jnp.int32, (tk, LANES), 1) == 0
    for hh in range(n_heads):
        hl = slice(hh * LANES, (hh + 1) * LANES)
        if has_ctx:
            k_scr[hh][T:, :] = ck_ref[hh].astype(BF16)
            v_scr[hh][T:, 0:LANES] = cv_ref[hh].astype(BF16)
        k_scr[hh][0:T, :] = k_ref[0, :, hl]
        v_scr[hh][0:T, 0:LANES] = v_ref[0, :, hl]
        v_scr[hh][:, LANES:2 * LANES] = jnp.where(ones_col, 1.0, 0.0).astype(BF16)

    kblocks = [slice(c, c + MXU_DIM) for c in range(0, tk, MXU_DIM)]
    tasks = [(hh, i) for hh in range(n_heads) for i in range(n_tiles)]

    def halves_max(x):
        return jnp.maximum(x[:, 0:LANES], x[:, LANES:2 * LANES])

    def score_stage(n):
        hh, i = tasks[n]
        q = q_ref[0, i * tq:(i + 1) * tq, hh * LANES:(hh + 1) * LANES]
        lane = lax.broadcasted_iota(jnp.int32, q.shape, 1)
        for mp in range(2):
            sel = (lane < ATTN_HEAD_DIM) if mp == 0 else (lane >= ATTN_HEAD_DIM)
            qm = jnp.where(sel, q, jnp.zeros_like(q))
            mx = None
            for kb in kblocks:
                s = _mm(qm, k_scr[hh][kb, :], DN_NT)
                s_scr[n % 2][mp][:, kb] = s
                mx = halves_max(s) if mx is None else jnp.maximum(mx, halves_max(s))
                yield
            m_scr[n % 2][mp][...] = jnp.max(mx, axis=-1, keepdims=True)

    def attend_stage(n):
        hh, i = tasks[n]
        outs = []
        for mp in range(2):
            m = m_scr[n % 2][mp][...]
            for kb in kblocks:
                e_scr[n % 2][mp][:, kb] = jnp.exp2(s_scr[n % 2][mp][:, kb] - m).astype(BF16)
                yield
            ov = _mm(e_scr[n % 2][mp][...], v_scr[hh][...])
            outs.append((ov[:, 0:LANES], ov[:, LANES:LANES + 1]))
            yield
        (o1, d1), (o2, d2) = outs
        o = o1 * (1.0 / d1) - o2 * (lam / d2)
        ms = jnp.mean(o * o, axis=-1, keepdims=True)
        o = o * lax.rsqrt(ms + 1e-5) * subw_ref[...] * (1.0 - lam_init)
        rows, hl = slice(i * tq, (i + 1) * tq), slice(hh * LANES, (hh + 1) * LANES)
        o_ref[0, rows, hl] = (o * _silu(g_ref[0, rows, hl].astype(F32))).astype(BF16)

    for n in range(len(tasks) + 1):
        live = []
        if n < len(tasks):
            live.append(score_stage(n))
        if n >= 1:
            live.append(attend_stage(n - 1))
        while live:
            for gen in list(live):
                if next(gen, "done") == "done":
                    live.remove(gen)


def _attn_core(qkvg, lam_vecs, subw, layer_idx, ctx=None):
    B, T, _ = qkvg.shape
    H, DV = ATTN_HEADS, ATTN_V_DIM
    HS = min(H, max(1, ATTN_ROWS_PER_STEP // T))
    lam_init = 0.8 - 0.6 * math.exp(-0.3 * layer_idx)
    has_ctx = ctx is not None
    tq = min(ATTN_Q_TILE, T)
    tk = T + (ctx[0].shape[3] if has_ctx else 0)

    def col(off):
        return pl.BlockSpec((1, T, HS * DV), lambda b, h: (b, 0, off + h))

    nb = H // HS
    in_specs = [pl.BlockSpec((4, ATTN_HEAD_DIM), lambda b, h: (0, 0)),
                col(0), col(nb), col(2 * nb), col(3 * nb),
                pl.BlockSpec((1, DV), lambda b, h: (0, 0))]
    args = [lam_vecs, qkvg, qkvg, qkvg, qkvg, subw.reshape(1, DV)]
    if has_ctx:
        ck, cv, j = ctx
        P = ck.shape[3]
        cache_spec = pl.BlockSpec((None, None, HS, P, DV), lambda b, h: (b, j, h, 0, 0))
        in_specs += [cache_spec, cache_spec]
        args += [ck, cv]
    return pl.pallas_call(
        functools.partial(_attn_core_kernel, lam_init=lam_init, has_ctx=has_ctx, tq=tq),
        grid=(B, nb),
        in_specs=in_specs,
        out_specs=pl.BlockSpec((1, T, HS * DV), lambda b, h: (b, 0, h)),
        out_shape=jax.ShapeDtypeStruct((B, T, D_INNER), BF16),
        scratch_shapes=([pltpu.VMEM((tk, DV), BF16)] * HS + [pltpu.VMEM((tk, 2 * DV), BF16)] * HS
                        + [pltpu.VMEM((tq, tk), F32)] * 4 + [pltpu.VMEM((tq, tk), BF16)] * 4
                        + [pltpu.VMEM((tq, 1), F32)] * 4),
        compiler_params=_cparams(("parallel", "parallel")),
        name="attn_core",
    )(*args)


def _gated_residual(o, w_ref, x_ref, mod_ref):
    return x_ref[0] + mod_ref[:, 2 * D_MODEL:3 * D_MODEL] * _dot(o, w_ref[...])


def _out_proj_kernel(o_ref, w_ref, x_ref, mod_ref, *rest, final_norm):
    xn = _gated_residual(o_ref[0], w_ref, x_ref, mod_ref)
    if final_norm:
        fw_ref, y_ref = rest
        ms = jnp.mean(xn * xn, axis=-1, keepdims=True)
        xn = xn * lax.rsqrt(ms + NORM_EPS) * fw_ref[...]
    else:
        (y_ref,) = rest
    y_ref[0] = xn


def _out_proj(o, w_bf16, x, mods, layer, row_of, final_w=None):
    B, T, D = x.shape
    tm = min(ROW_TILE, T)
    tile = pl.BlockSpec((1, tm, D), lambda b, t: (b, t, 0))
    in_specs = [tile, _const_spec((D, D)), tile, _mod_spec(layer, row_of)]
    args = [o, w_bf16, x, mods]
    if final_w is not None:
        in_specs.append(_const_spec((1, D)))
        args.append(final_w.reshape(1, D))
    return pl.pallas_call(
        functools.partial(_out_proj_kernel, final_norm=final_w is not None),
        grid=(B, T // tm),
        in_specs=in_specs,
        out_specs=tile,
        out_shape=jax.ShapeDtypeStruct((B, T, D), F32),
        compiler_params=_cparams(("parallel", "parallel")),
        name="out_proj",
    )(*args)


def _rwkv_in_kernel(x_ref, xp_ref, xn_ref, nw_ref, mod_ref, mu_ref, w_ref, l1_ref, w2_ref, a2_ref,
                    w0_ref, a0_ref, kk_ref, ka_ref, rk_ref, seg_ref,
                    r_out, kk_out, v_out, sg_out, bonus_out, lw_out, a_out, kz_out):
    t = pl.program_id(1)
    nt = pl.num_programs(1)
    nw = nw_ref[...]
    mod = mod_ref[...]
    h = _normmod(x_ref[0], nw, mod)
    tm = h.shape[0]
    hp = _normmod(xp_ref[0], nw, mod)[SUBLANES - 1:SUBLANES, :]
    hn = _normmod(xn_ref[0], nw, mod)[0:1, :]
    hp = jnp.where(t > 0, hp, 0.0)
    hn = jnp.where(t < nt - 1, hn, 0.0)
    row = lax.broadcasted_iota(jnp.int32, h.shape, 0)
    h_prev = jnp.where(row == 0, hp, pltpu.roll(h, 1, axis=0))
    h_next = jnp.where(row == tm - 1, hn, pltpu.roll(h, tm - 1, axis=0))
    dx = 0.5 * (h_prev + h_next) - h

    def mix(n):
        return (h + dx * mu_ref[n:n + 1, :]).astype(BF16)

    E = D_INNER
    r = _mm(mix(0), w_ref[:, 0:E])
    k = _mm(mix(1), w_ref[:, E:2 * E])
    v = _mm(mix(2), w_ref[:, 2 * E:3 * E])
    g = _mm(mix(3), w_ref[:, 3 * E:4 * E])
    lw1 = jnp.tanh(_mm(mix(4), l1_ref[:, 0:2 * LORA]))
    la1 = _mm(mix(5), l1_ref[:, 2 * LORA:4 * LORA])
    lane = lax.broadcasted_iota(jnp.int32, lw1.shape, 1)

    kk = k * kk_ref[...]
    kk = kk * lax.rsqrt(jnp.maximum(_headsum(kk * kk, seg_ref), 1e-24))
    r_out[0] = r
    kk_out[0] = kk
    v_out[0] = v
    sg_out[0] = _silu(g).astype(BF16)

    ksum = None
    for z in range(2):
        sel = (lane >= z * LORA) & (lane < (z + 1) * LORA)
        lw = _dot(jnp.where(sel, lw1, 0.0), w2_ref[...])
        la = _dot(jnp.where(sel, la1, 0.0), a2_ref[...])
        u = -(w0_ref[z:z + 1, :] + lw)
        softplus = jnp.maximum(u, 0.0) + jnp.log(1.0 + jnp.exp(-jnp.abs(u)))
        lw_out[z, 0] = -jnp.exp(-softplus - 0.5)
        a = jax.nn.sigmoid(a0_ref[z:z + 1, :] + la)
        a_out[z, 0] = a
        kz = k * (1.0 + (a - 1.0) * ka_ref[...])
        kz_out[z, 0] = kz
        ksum = kz if ksum is None else ksum + kz
    bonus_out[0] = (_headsum(r * rk_ref[...] * ksum, seg_ref) * v).astype(BF16)


def _rwkv_in(x, nw, mods, layer, row_of, p):
    B, T, D = x.shape
    E = D_INNER
    tm = min(RWKV_ROW_TILE, T)
    g8 = tm // SUBLANES
    last8 = T // SUBLANES - 1
    tile = pl.BlockSpec((1, tm, D), lambda b, t: (b, t, 0))
    prev8 = pl.BlockSpec((1, SUBLANES, D), lambda b, t: (b, jnp.maximum(t * g8 - 1, 0), 0))
    next8 = pl.BlockSpec((1, SUBLANES, D), lambda b, t: (b, jnp.minimum((t + 1) * g8, last8), 0))
    otile = pl.BlockSpec((1, tm, E), lambda b, t: (b, t, 0))
    ztile = pl.BlockSpec((2, 1, tm, E), lambda b, t: (0, b, t, 0))
    one = jax.ShapeDtypeStruct((B, T, E), F32)
    two = jax.ShapeDtypeStruct((2, B, T, E), F32)
    return pl.pallas_call(
        _rwkv_in_kernel,
        grid=(B, T // tm),
        in_specs=[tile, prev8, next8, _const_spec((1, D)), _mod_spec(layer, row_of),
                  _const_spec((6, D)), _const_spec((D, 4 * E)), _const_spec((D, 4 * LORA)),
                  _const_spec((2 * LORA, E)), _const_spec((2 * LORA, E)),
                  _const_spec((2, E)), _const_spec((2, E)),
                  _const_spec((1, E)), _const_spec((1, E)), _const_spec((1, E)),
                  _const_spec((MXU_DIM, MXU_DIM))],
        out_specs=[otile] * 5 + [ztile] * 3,
        out_shape=[one] * 3 + [jax.ShapeDtypeStruct((B, T, E), BF16)] * 2 + [two] * 3,
        compiler_params=_cparams(("parallel", "parallel")),
        name="rwkv_in",
    )(x, x, x, nw.reshape(1, D), mods, p["mu"], p["w_in"], p["l1"], p["w2"], p["a2"],
      p["w0"], p["a0"], p["k_k"], p["k_a"], p["r_k"], p["seg"])


def _scan_kernel(*refs, has_init):
    n_in = 6
    ins = [refs[z * n_in:(z + 1) * n_in] for z in range(2)]
    rest = refs[2 * n_in:]
    if has_init:
        s0_refs, rest = rest[0:2], rest[2:]
    y_refs, sf_refs, s_scr = rest[0:2], rest[2:4], rest[4]
    C = SCAN_CHUNK
    PW = 2 * C
    HD = RWKV_HEAD_DIM
    NP = y_refs[0].shape[-1] // LANES
    chains = [(z, p) for z in range(2) for p in range(NP)]
    idx = range(len(chains))
    zs = [z for z, _ in chains]

    @pl.when(pl.program_id(1) == 0)
    def _():
        for z in range(2):
            s_scr[z] = s0_refs[z][...] if has_init else jnp.zeros_like(s_scr[z])

    ri = lax.broadcasted_iota(jnp.int32, (PW, PW), 0)
    ci = lax.broadcasted_iota(jnp.int32, (PW, PW), 1)
    same_blk = (ri // C) == (ci // C)
    tr, tc = ri % C, ci % C
    same_head = (ri // HD) == (ci // HD)
    diag = ri == ci
    eye = jnp.where(diag, 1.0, 0.0)
    li = lax.broadcasted_iota(jnp.int32, (C, C), 0)
    lj = lax.broadcasted_iota(jnp.int32, (C, C), 1)
    strict, incl, cum_mat = [], [], []
    for z in range(2):
        before = (tc > tr) if z == 1 else (tc < tr)
        strict.append(same_blk & before)
        incl.append(same_blk & (before | (tc == tr)))
        cum_mat.append(jnp.where((lj >= li) if z == 1 else (lj <= li), 1.0, 0.0).astype(BF16))
    lo = lax.broadcasted_iota(jnp.int32, (C, LANES), 1) < HD
    zeros_c = jnp.zeros((C, LANES), F32)
    zeros_p = jnp.zeros((PW, LANES), F32)

    def stack2(x):
        return jnp.concatenate([x, x], axis=0)

    def unstack(x2):
        return jnp.where(lo, x2[0:C], x2[C:PW])

    def load(z, which, p):
        ref = ins[z][which]
        sl = slice(p * LANES, (p + 1) * LANES)
        return ref[0, :, sl] if len(ref.shape) == 3 else ref[0, 0, :, sl]

    lw = [load(z, 3, p) for z, p in chains]
    cum = []
    for i in idx:
        hi, lo16 = _split2(lw[i])
        cum.append(_mm(cum_mat[zs[i]], hi) + _mm(cum_mat[zs[i]], lo16))
    tot = [(cum[i][0:1] if zs[i] == 1 else cum[i][C - 1:C]) for i in idx]
    abar, rbar, bh, kh, v, res = [], [], [], [], [], []
    for i, (z, p) in enumerate(chains):
        kk = load(z, 1, p)
        b = kk * load(z, 4, p)
        kz = load(z, 5, p)
        e_neg = jnp.exp(-cum[i])
        e_tot = jnp.exp(tot[i] - cum[i])
        abar.append(-kk * jnp.exp(cum[i] - lw[i]))
        rbar.append(load(z, 0, p) * jnp.exp(cum[i]))
        bt, kt = b * e_neg, kz * e_neg
        bh.append(b * e_tot)
        kh.append(kz * e_tot)
        v.append(load(z, 2, p))
        lhs = jnp.concatenate([abar[i], rbar[i]], axis=0)
        rhs = jnp.concatenate([jnp.where(lo, bt, 0.0), jnp.where(lo, 0.0, bt),
                               jnp.where(lo, kt, 0.0), jnp.where(lo, 0.0, kt)], axis=0)
        res.append(_mm(lhs.astype(BF16), rhs.astype(BF16), DN_NT))
    a_ab = [jnp.where(strict[zs[i]], stack2(res[i][0:C, 0:LANES]), 0.0) for i in idx]
    a_ak = [jnp.where(strict[zs[i]], stack2(res[i][0:C, LANES:2 * LANES]), 0.0) for i in idx]
    a_rb = [jnp.where(incl[zs[i]], stack2(res[i][C:PW, 0:LANES]), 0.0) for i in idx]
    a_rk = [jnp.where(incl[zs[i]], stack2(res[i][C:PW, LANES:2 * LANES]), 0.0) for i in idx]
    v2 = [stack2(x) for x in v]
    wv = [_dot(a_ak[i], v2[i]) for i in idx]

    pw = [_dot(x, x) for x in a_ab]
    q = [eye + x for x in a_ab]
    n = 4
    while n < C:
        m = [_dot(pw[i], jnp.concatenate([q[i], pw[i]], axis=1)) for i in idx]
        q = [q[i] + m[i][:, 0:LANES] for i in idx]
        pw = [x[:, LANES:2 * LANES] for x in m]
        n *= 2
    q = [q[i] + _dot(pw[i], q[i]) for i in idx]

    x = [_dot(q[i], jnp.concatenate([stack2(abar[i]), wv[i]], axis=1)) for i in idx]
    ahat = [unstack(t[:, 0:LANES]) for t in x]
    u0 = [unstack(t[:, LANES:2 * LANES]) for t in x]

    ry = [_dot(jnp.concatenate([a_rb[i], a_rk[i]], axis=1),
               jnp.concatenate([jnp.concatenate([stack2(ahat[i]), stack2(u0[i])], axis=1),
                                jnp.concatenate([zeros_p, v2[i]], axis=1)], axis=0)) for i in idx]
    rhat = [rbar[i] + unstack(ry[i][:, 0:LANES]) for i in idx]
    y0 = [unstack(t[:, LANES:2 * LANES]) for t in ry]

    mn = [_dot(jnp.concatenate([bh[i], kh[i]], axis=0).T,
               jnp.concatenate([jnp.concatenate([ahat[i], u0[i]], axis=1),
                                jnp.concatenate([zeros_c, v[i]], axis=1)], axis=0)) for i in idx]
    for i, (z, p) in enumerate(chains):
        m_c = jnp.where(same_head, mn[i][:, 0:LANES], 0.0) + jnp.where(diag, jnp.exp(tot[i]), 0.0)
        n_c = jnp.where(same_head, mn[i][:, LANES:2 * LANES], 0.0)
        s_old = s_scr[z, p]
        y_refs[z][0, :, p * LANES:(p + 1) * LANES] = _dot(rhat[i], s_old) + y0[i]
        s_new = _dot(m_c, s_old) + n_c
        s_scr[z, p] = s_new
        sf_refs[z][p] = s_new


def _rwkv_scan(r, kk, v, lw, a, kz, s0=None):
    B, T, E = r.shape
    NP = E // LANES
    C = SCAN_CHUNK
    nb = T // C
    state = pl.BlockSpec((None, NP, LANES, LANES), lambda b, t: (b, 0, 0, 0))
    in_specs, args, y_specs = [], [], []
    for z in range(2):
        tok = (lambda t: t) if z == 0 else (lambda t: nb - 1 - t)
        one = pl.BlockSpec((1, C, E), lambda b, t, tok=tok: (b, tok(t), 0))
        two = pl.BlockSpec((1, 1, C, E), lambda b, t, tok=tok, z=z: (z, b, tok(t), 0))
        in_specs += [one, one, one, two, two, two]
        args += [r, kk, v, lw, a, kz]
        y_specs.append(one)
    if s0 is not None:
        in_specs += [state, state]
        args += list(s0)
    return pl.pallas_call(
        functools.partial(_scan_kernel, has_init=s0 is not None),
        grid=(B, nb),
        in_specs=in_specs,
        out_specs=y_specs + [state, state],
        out_shape=[jax.ShapeDtypeStruct((B, T, E), F32)] * 2
        + [jax.ShapeDtypeStruct((B, NP, LANES, LANES), F32)] * 2,
        scratch_shapes=[pltpu.VMEM((2, NP, LANES, LANES), F32)],
        compiler_params=_cparams(("parallel", "arbitrary")),
        name="rwkv_scan",
    )(*args)


def _rwkv_out_kernel(y0_ref, y1_ref, bonus_ref, sg_ref, lnw_ref, lnb_ref, seg_ref, w_ref, x_ref, mod_ref, o_ref):
    y = y0_ref[0] + y1_ref[0]
    inv_n = 1.0 / RWKV_HEAD_DIM
    d = y - _headsum(y, seg_ref) * inv_n
    var = _headsum(d * d, seg_ref) * inv_n
    yn = d * lax.rsqrt(var + RWKV_GN_EPS) * lnw_ref[...] + lnb_ref[...] + bonus_ref[0].astype(F32)
    o_ref[0] = _gated_residual(yn * sg_ref[0].astype(F32), w_ref, x_ref, mod_ref)


def _rwkv_out(y0, y1, bonus, sg, p, x, mods, layer, row_of):
    B, T, D = x.shape
    tm = min(ROW_TILE, T)
    tile = pl.BlockSpec((1, tm, D), lambda b, t: (b, t, 0))
    return pl.pallas_call(
        _rwkv_out_kernel,
        grid=(B, T // tm),
        in_specs=[tile, tile, tile, tile, _const_spec((1, D)), _const_spec((1, D)),
                  _const_spec((MXU_DIM, MXU_DIM)), _const_spec((D, D)), tile, _mod_spec(layer, row_of)],
        out_specs=tile,
        out_shape=jax.ShapeDtypeStruct((B, T, D), F32),
        compiler_params=_cparams(("parallel", "parallel")),
        name="rwkv_out",
    )(y0, y1, bonus, sg, p["ln_w"], p["ln_b"], p["seg"], p["w_out"], x, mods)


def _conv_in_kernel(x_ref, nw_ref, mod_ref, w_ref, z_ref, sg_ref):
    hb = _normmod(x_ref[0], nw_ref[...], mod_ref[...]).astype(BF16)
    E = D_INNER
    for n in range(0, E, N_CHUNK):
        a = _mm(hb, w_ref[:, n:n + N_CHUNK])
        b = _mm(hb, w_ref[:, E + n:E + n + N_CHUNK])
        g = _mm(hb, w_ref[:, 2 * E + n:2 * E + n + N_CHUNK])
        z_ref[0, :, n:n + N_CHUNK] = a * jax.nn.sigmoid(b)
        sg_ref[0, :, n:n + N_CHUNK] = _silu(g).astype(BF16)


def _conv_in(x, nw, mods, layer, row_of, w_bf16):
    B, T, D = x.shape
    E = D_INNER
    tm = min(ROW_TILE, T)
    tile = pl.BlockSpec((1, tm, D), lambda b, t: (b, t, 0))
    return pl.pallas_call(
        _conv_in_kernel,
        grid=(B, T // tm),
        in_specs=[tile, _const_spec((1, D)), _mod_spec(layer, row_of), _const_spec((D, 3 * E))],
        out_specs=[tile, tile],
        out_shape=[jax.ShapeDtypeStruct((B, T, E), F32), jax.ShapeDtypeStruct((B, T, E), BF16)],
        compiler_params=_cparams(("parallel", "parallel")),
        name="conv_in",
    )(x, nw.reshape(1, D), mods, w_bf16)


def _conv_out_kernel(z_ref, zp_ref, zn_ref, sg_ref, dw_ref, dwb_ref, lnw_ref, lnb_ref, w_ref, x_ref, mod_ref,
                     o_ref, zs, cbuf):
    t = pl.program_id(1)
    nt = pl.num_programs(1)
    tm = z_ref.shape[1]
    rows = tm + 2 * HALO
    zs[0, 0:HALO, :] = jnp.where(t > 0, zp_ref[0], 0.0)
    zs[0, HALO:HALO + tm, :] = z_ref[0]
    zs[0, HALO + tm:rows, :] = jnp.where(t < nt - 1, zn_ref[0], 0.0)
    for s in range(1, SUBLANES):
        zs[s, 0:rows - SUBLANES, :] = zs[0, s:s + rows - SUBLANES, :]
    off = HALO - CONV_PAD
    for c0 in range(0, D_INNER, LANES):
        for r0 in range(0, tm, CONV_ROWS):
            acc = None
            for k in range(CONV_WIDTH):
                q8, s = divmod(off + k, SUBLANES)
                a0 = r0 + q8 * SUBLANES
                term = dw_ref[k:k + 1, c0:c0 + LANES] * zs[s, a0:a0 + CONV_ROWS, c0:c0 + LANES]
                acc = term if acc is None else acc + term
            cbuf[r0:r0 + CONV_ROWS, c0:c0 + LANES] = acc
    c = cbuf[...] + dwb_ref[...]
    m = jnp.mean(c, axis=-1, keepdims=True)
    d = c - m
    var = jnp.mean(d * d, axis=-1, keepdims=True)
    y = d * lax.rsqrt(var + 1e-5) * lnw_ref[...] + lnb_ref[...]
    o_ref[0] = _gated_residual(_silu(y) * sg_ref[0].astype(F32), w_ref, x_ref, mod_ref)


def _conv_out(z, sg, p, x, mods, layer, row_of):
    B, T, D = x.shape
    tm = min(ROW_TILE, T)
    gh = tm // HALO
    lasth = T // HALO - 1
    tile = pl.BlockSpec((1, tm, D), lambda b, t: (b, t, 0))
    prev = pl.BlockSpec((1, HALO, D), lambda b, t: (b, jnp.maximum(t * gh - 1, 0), 0))
    nxt = pl.BlockSpec((1, HALO, D), lambda b, t: (b, jnp.minimum((t + 1) * gh, lasth), 0))
    return pl.pallas_call(
        _conv_out_kernel,
        grid=(B, T // tm),
        in_specs=[tile, prev, nxt, tile, _const_spec((CONV_WIDTH, D)), _const_spec((1, D)),
                  _const_spec((1, D)), _const_spec((1, D)), _const_spec((D, D)), tile,
                  _mod_spec(layer, row_of)],
        out_specs=tile,
        out_shape=jax.ShapeDtypeStruct((B, T, D), F32),
        scratch_shapes=[pltpu.VMEM((SUBLANES, tm + 2 * HALO, D), F32), pltpu.VMEM((tm, D), F32)],
        compiler_params=_cparams(("parallel", "parallel")),
        name="conv_out",
    )(z, z, z, sg, p["dw_w"], p["dw_b"], p["ln_w"], p["ln_b"], p["w_out"], x, mods)


def _rope_tables(n_tok):
    t = jnp.arange(n_tok, dtype=jnp.int32)
    pos = jnp.stack([(t // GRID_W).astype(F32), (t % GRID_W).astype(F32)], axis=1)
    lane = jnp.arange(LANES, dtype=jnp.int32)
    d = lane % ATTN_HEAD_DIM
    axis = d // 32
    second_half = (d % 32) // 16
    axis_dim = ATTN_HEAD_DIM // 2
    inv_freq = ROPE_BASE ** (-(2.0 * (d % 16).astype(F32)) / axis_dim)
    ang = pos[:, axis] * inv_freq[None, :]
    sign = jnp.where(second_half == 1, 1.0, -1.0).astype(F32)
    return jnp.cos(ang), jnp.sin(ang) * sign[None, :]


def _pair_blockdiag(s):
    B, H, N, _ = s.shape
    st = jnp.swapaxes(s, -1, -2).reshape(B, H // 2, 2, N, N)
    z = jnp.zeros_like(st[:, :, 0])
    top = jnp.concatenate([st[:, :, 0], z], axis=-1)
    bot = jnp.concatenate([z, st[:, :, 1]], axis=-1)
    return jnp.concatenate([top, bot], axis=-2)


def _pair_unblock(sp):
    B, NP, _, _ = sp.shape
    N = RWKV_HEAD_DIM
    blocks = jnp.stack([sp[:, :, 0:N, 0:N], sp[:, :, N:2 * N, N:2 * N]], axis=2)
    return jnp.swapaxes(blocks, -1, -2).reshape(B, 2 * NP, N, N)


def kernel(x_prompt, x_sample, cache_attn_k, cache_attn_v, state_rwkv, c, c_ctx, norm_w, ada_w, ada_b, attn_w_in, attn_lambda, attn_subln_w, attn_w_out, rwkv_mu, rwkv_w_in, rwkv_w0, rwkv_w1, rwkv_w2, rwkv_a0, rwkv_a1, rwkv_a2, rwkv_k_k, rwkv_k_a, rwkv_r_k, rwkv_ln_w, rwkv_ln_b, rwkv_w_out, conv_w_in, conv_dw_w, conv_dw_b, conv_ln_w, conv_ln_b, conv_w_out, final_norm_w):
    D, E = D_MODEL, D_INNER
    dec_batch = x_sample.shape[0]
    assert dec_batch < COND_ROWS
    cond = jnp.concatenate([c, c_ctx[None, :], jnp.zeros((COND_ROWS - dec_batch - 1, D), F32)], axis=0)
    mods = _ada(cond, ada_w, ada_b)
    ctx_row = dec_batch
    streams = [(x_prompt, lambda b: ctx_row), (x_sample, lambda b: b)]
    cos, sin = _rope_tables(x_sample.shape[1])
    seg = (jnp.arange(MXU_DIM)[:, None] // RWKV_HEAD_DIM == jnp.arange(MXU_DIM)[None, :] // RWKV_HEAD_DIM).astype(BF16)

    xs = [x_prompt, x_sample]
    new_kv, new_s = (None, None), []
    assert (DEPTH - 1) % N_MIXERS == 0
    for i in range(DEPTH):
        kind, j = i % N_MIXERS, i // N_MIXERS
        last = i == DEPTH - 1
        fw = final_norm_w if last else None
        if kind == 0:
            w_in = attn_w_in[j].astype(BF16)
            w_out = attn_w_out[j].astype(BF16)
            for s, (_, row_of) in enumerate(streams):
                if s == 0:
                    qkvg, *new_kv = _attn_in(xs[s], norm_w[i], mods, i, row_of, w_in, cache=new_kv)
                    o = _attn_core(qkvg, attn_lambda[j], attn_subln_w[j], i)
                else:
                    (qkvg,) = _attn_in(xs[s], norm_w[i], mods, i, row_of, w_in, rope=(cos, sin))
                    o = _attn_core(qkvg, attn_lambda[j], attn_subln_w[j], i, ctx=(cache_attn_k, cache_attn_v, j))
                xs[s] = _out_proj(o, w_out, xs[s], mods, i, row_of, final_w=fw)
        elif kind == 1:
            p = dict(
                mu=rwkv_mu[j],
                w_in=rwkv_w_in[j].astype(BF16),
                l1=jnp.concatenate([rwkv_w1[j, 0], rwkv_w1[j, 1], rwkv_a1[j, 0], rwkv_a1[j, 1]], axis=1).astype(BF16),
                w2=rwkv_w2[j].reshape(2 * LORA, E).astype(BF16),
                a2=rwkv_a2[j].reshape(2 * LORA, E).astype(BF16),
                w0=rwkv_w0[j], a0=rwkv_a0[j],
                k_k=rwkv_k_k[j].reshape(1, E), k_a=rwkv_k_a[j].reshape(1, E), r_k=rwkv_r_k[j].reshape(1, E),
                ln_w=rwkv_ln_w[j].reshape(1, E), ln_b=rwkv_ln_b[j].reshape(1, E),
                w_out=rwkv_w_out[j].astype(BF16), seg=seg)
            for s, (_, row_of) in enumerate(streams):
                r, kk, v, sg, bonus, lw, a, kz = _rwkv_in(xs[s], norm_w[i], mods, i, row_of, p)
                s0 = [_pair_blockdiag(state_rwkv[:, j, z]) for z in range(2)] if s == 1 else None
                y_f, y_b, sf_f, sf_b = _rwkv_scan(r, kk, v, lw, a, kz, s0)
                if s == 0:
                    new_s.append(jnp.stack([_pair_unblock(sf_f), _pair_unblock(sf_b)], axis=1))
                xs[s] = _rwkv_out(y_f, y_b, bonus, sg, p, xs[s], mods, i, row_of)
        else:
            p = dict(dw_w=conv_dw_w[j], dw_b=conv_dw_b[j].reshape(1, E), ln_w=conv_ln_w[j].reshape(1, E),
                     ln_b=conv_ln_b[j].reshape(1, E), w_out=conv_w_out[j].astype(BF16))
            w_in = conv_w_in[j].astype(BF16)
            for s, (_, row_of) in enumerate(streams):
                zz, sg = _conv_in(xs[s], norm_w[i], mods, i, row_of, w_in)
                xs[s] = _conv_out(zz, sg, p, xs[s], mods, i, row_of)
    return (xs[0], xs[1], new_kv[0], new_kv[1], jnp.stack(new_s, axis=1))
```

```python
import functools
import math

import jax
import jax.numpy as jnp
from jax import lax
from jax.experimental import pallas as pl
from jax.experimental.pallas import tpu as pltpu

F32 = jnp.float32
BF16 = jnp.bfloat16

D_MODEL = 1024
D_INNER = 1024
DEPTH = 4
N_MIXERS = 3
GRID_W = 64
NORM_EPS = 1e-6
ATTN_HEAD_DIM = 64
ATTN_HEADS = 8
ATTN_V_DIM = 128
ROPE_BASE = 10000.0
RWKV_HEAD_DIM = 64
RWKV_HEADS = 16
RWKV_GN_EPS = RWKV_HEAD_DIM * 1e-5
LORA = 64
CONV_WIDTH = 31
CONV_PAD = CONV_WIDTH // 2

LANES = 128
SUBLANES = 8
MXU_DIM = 256
VMEM_LIMIT_BYTES = 56 * 1024 * 1024

ROW_TILE = 256
RWKV_ROW_TILE = 128
ATTN_Q_TILE = 256
ATTN_ROWS_PER_STEP = 2048
ATTN_TASK_GROUP = 2
N_CHUNK = 512
SCAN_CHUNK = 64
HALO = 16
CONV_ROWS = 64
COND_ROWS = 16
DN = (((1,), (0,)), ((), ()))
DN_NT = (((1,), (1,)), ((), ()))


def _cparams(sem):
    return pltpu.CompilerParams(dimension_semantics=sem, vmem_limit_bytes=VMEM_LIMIT_BYTES)


def _mm(a, b, dn=DN):
    return lax.dot_general(a, b, dn, preferred_element_type=F32)


def _dot(a, b):
    return _mm(a.astype(BF16), b.astype(BF16))


def _split2(x):
    hi = x.astype(BF16)
    lo = (x - hi.astype(F32)).astype(BF16)
    return hi, lo


def _dot3(a, b):
    a_hi, a_lo = _split2(a)
    b_hi, b_lo = _split2(b)
    return _mm(a_hi, b_hi) + (_mm(a_lo, b_hi) + _mm(a_hi, b_lo))


def _silu(x):
    return x * jax.nn.sigmoid(x)


def _normmod(x, nw, mod):
    ms = jnp.mean(x * x, axis=-1, keepdims=True)
    y = x * lax.rsqrt(ms + NORM_EPS) * nw
    return y * (1.0 + mod[:, D_MODEL:2 * D_MODEL]) + mod[:, 0:D_MODEL]


def _headsum(x, seg_ref):
    cols = []
    for c in range(0, x.shape[-1], MXU_DIM):
        hi, lo = _split2(x[:, c:c + MXU_DIM])
        cols.append(_mm(hi, seg_ref[...]) + _mm(lo, seg_ref[...]))
    return jnp.concatenate(cols, axis=-1)


def _mod_spec(layer, row_of):
    return pl.BlockSpec((None, None, 1, 3 * D_MODEL), lambda b, t: (layer, row_of(b), 0, 0))


def _const_spec(shape):
    return pl.BlockSpec(shape, lambda b, t: (0,) * len(shape))


def _ada_kernel(cond_ref, w_ref, b_ref, o_ref):
    o_ref[0, :, 0, :] = _dot3(_silu(cond_ref[...]), w_ref[0]) + b_ref[0]


def _ada(cond, ada_w, ada_b):
    depth, d, n3 = ada_w.shape
    tn = 1024
    return pl.pallas_call(
        _ada_kernel,
        grid=(depth, n3 // tn),
        in_specs=[
            pl.BlockSpec((COND_ROWS, d), lambda i, j: (0, 0)),
            pl.BlockSpec((1, d, tn), lambda i, j: (i, 0, j)),
            pl.BlockSpec((1, 1, tn), lambda i, j: (i, 0, j)),
        ],
        out_specs=pl.BlockSpec((1, COND_ROWS, 1, tn), lambda i, j: (i, 0, 0, j)),
        out_shape=jax.ShapeDtypeStruct((depth, COND_ROWS, 1, n3), F32),
        compiler_params=_cparams(("parallel", "parallel")),
        name="ada_mod",
    )(cond, ada_w, ada_b.reshape(depth, 1, n3))


ATTN_Q_SCALE = ATTN_HEAD_DIM ** -0.5 * math.log2(math.e)


def _attn_in_kernel(x_ref, nw_ref, mod_ref, w_ref, *rest, rope, n_cached):
    rest = list(rest)
    if rope:
        cos_ref, sin_ref = rest.pop(0), rest.pop(0)
    if n_cached:
        pk_ref, pv_ref = rest.pop(0), rest.pop(0)
    o_ref = rest.pop(0)
    if n_cached is not None:
        ko_ref, vo_ref = rest
        if n_cached:
            ko_ref[0:n_cached] = pk_ref[...]
            vo_ref[0:n_cached] = pv_ref[...]
    E = D_INNER
    hb = _normmod(x_ref[0], nw_ref[...], mod_ref[...]).astype(BF16)
    for n in range(0, 4 * E, N_CHUNK):
        y = _mm(hb, w_ref[:, n:n + N_CHUNK])
        if n_cached is not None and E <= n < 3 * E:
            cache_ref = ko_ref if n < 2 * E else vo_ref
            for c in range(0, N_CHUNK, LANES):
                cache_ref[n_cached, (n % E + c) // LANES] = y[:, c:c + LANES]
        if n < E:
            y = y * ATTN_Q_SCALE
        if rope and n < 2 * E:
            y = jnp.concatenate([_rope(y[:, c:c + LANES], cos_ref[...], sin_ref[...])
                                 for c in range(0, N_CHUNK, LANES)], axis=1)
        o_ref[0, :, n:n + N_CHUNK] = y.astype(BF16)


def _attn_in(x, nw, mods, layer, row_of, w_bf16, rope=None, cache=None):
    B, T, D = x.shape
    N = w_bf16.shape[1]
    H, DV = ATTN_HEADS, ATTN_V_DIM
    tm = min(ROW_TILE, T)
    in_specs = [pl.BlockSpec((1, tm, D), lambda b, t: (b, t, 0)), _const_spec((1, D)),
                _mod_spec(layer, row_of), _const_spec((D, N))]
    args = [x, nw.reshape(1, D), mods, w_bf16]
    if rope is not None:
        in_specs += [pl.BlockSpec((tm, LANES), lambda b, t: (t, 0))] * 2
        args += list(rope)
    out_specs = [pl.BlockSpec((1, tm, N), lambda b, t: (b, t, 0))]
    out_shape = [jax.ShapeDtypeStruct((B, T, N), BF16)]
    n_cached = None
    if cache is not None:
        n_cached = 0 if cache[0] is None else cache[0].shape[1]
        if n_cached:
            in_specs += [pl.BlockSpec((None, n_cached, H, tm, DV), lambda b, t: (b, 0, 0, t, 0))] * 2
            args += list(cache)
        out_specs += [pl.BlockSpec((None, n_cached + 1, H, tm, DV), lambda b, t: (b, 0, 0, t, 0))] * 2
        out_shape += [jax.ShapeDtypeStruct((B, n_cached + 1, H, T, DV), F32)] * 2
    return pl.pallas_call(
        functools.partial(_attn_in_kernel, rope=rope is not None, n_cached=n_cached),
        grid=(B, T // tm),
        in_specs=in_specs,
        out_specs=out_specs,
        out_shape=out_shape,
        compiler_params=_cparams(("parallel", "parallel")),
        name="attn_in",
    )(*args)


def _rope(x, cos, sin):
    lane = lax.broadcasted_iota(jnp.int32, x.shape, 1)
    first = (lane % 32) < 16
    partner = jnp.where(first, pltpu.roll(x, LANES - 16, axis=1), pltpu.roll(x, 16, axis=1))
    return x * cos + partner * sin


def _attn_core_kernel(*refs, lam_init, has_ctx, tq, group):
    it = iter(refs)
    lam_ref, q_ref, k_ref, v_ref, g_ref, subw_ref = (next(it) for _ in range(6))
    if has_ctx:
        ck_ref, cv_ref = next(it), next(it)
    o_ref = next(it)
    T = q_ref.shape[1]
    n_heads = q_ref.shape[2] // LANES
    n_tiles = T // tq
    k_scr = [next(it) for _ in range(n_heads)]
    v_scr = [next(it) for _ in range(n_heads)]
    ring = 2 * group
    s_scr = [[next(it), next(it)] for _ in range(ring)]
    e_scr = [[next(it), next(it)] for _ in range(ring)]
    m_scr = [[next(it), next(it)] for _ in range(ring)]

    lv = lam_ref[...]
    lam = (jnp.exp(jnp.sum(lv[0:1] * lv[1:2], axis=-1, keepdims=True))
           - jnp.exp(jnp.sum(lv[2:3] * lv[3:4], axis=-1, keepdims=True)) + lam_init)

    tk = k_scr[0].shape[0]
    for hh in range(n_heads):
        hl = slice(hh * LANES, (hh + 1) * LANES)
        if has_ctx:
            k_scr[hh][T:, :] = ck_ref[hh].astype(BF16)
            v_scr[hh][T:, 0:LANES] = cv_ref[hh].astype(BF16)
        k_scr[hh][0:T, :] = k_ref[0, :, hl]
        v_scr[hh][0:T, 0:LANES] = v_ref[0, :, hl]
        v_scr[hh][:, LANES:2 * LANES] = jnp.ones((tk, LANES), BF16)

    kblocks = [slice(c, c + MXU_DIM) for c in range(0, tk, MXU_DIM)]
    tasks = [(hh, i) for hh in range(n_heads) for i in range(n_tiles)]

    def halves_max(x):
        return jnp.maximum(x[:, 0:LANES], x[:, LANES:2 * LANES])

    def score_stage(n):
        hh, i = tasks[n]
        q = q_ref[0, i * tq:(i + 1) * tq, hh * LANES:(hh + 1) * LANES]
        lane = lax.broadcasted_iota(jnp.int32, q.shape, 1)
        for mp in range(2):
            sel = (lane < ATTN_HEAD_DIM) if mp == 0 else (lane >= ATTN_HEAD_DIM)
            qm = jnp.where(sel, q, jnp.zeros_like(q))
            mx = None
            for kb in kblocks:
                s = _mm(qm, k_scr[hh][kb, :], DN_NT)
                s_scr[n % ring][mp][:, kb] = s
                mx = halves_max(s) if mx is None else jnp.maximum(mx, halves_max(s))
                yield
            m_scr[n % ring][mp][...] = jnp.broadcast_to(jnp.max(mx, axis=-1, keepdims=True), mx.shape)

    def attend_stage(n):
        hh, i = tasks[n]
        outs = []
        for mp in range(2):
            m = m_scr[n % ring][mp][...]
            m = jnp.concatenate([m, m], axis=1)
            for kb in kblocks:
                e_scr[n % ring][mp][:, kb] = jnp.exp2(s_scr[n % ring][mp][:, kb] - m).astype(BF16)
                yield
            ov = _mm(e_scr[n % ring][mp][...], v_scr[hh][...])
            outs.append((ov[:, 0:LANES], ov[:, LANES:2 * LANES]))
            yield
        (o1, d1), (o2, d2) = outs
        o = o1 * (1.0 / d1) - o2 * (lam / d2)
        ms = jnp.mean(o * o, axis=-1, keepdims=True)
        o = o * lax.rsqrt(ms + 1e-5) * subw_ref[...] * (1.0 - lam_init)
        rows, hl = slice(i * tq, (i + 1) * tq), slice(hh * LANES, (hh + 1) * LANES)
        o_ref[0, rows, hl] = (o * _silu(g_ref[0, rows, hl].astype(F32))).astype(BF16)

    n_groups = len(tasks) // group
    for t in range(n_groups + 1):
        live = []
        if t < n_groups:
            live += [score_stage(t * group + u) for u in range(group)]
        if t >= 1:
            live += [attend_stage((t - 1) * group + u) for u in range(group)]
        while live:
            for gen in list(live):
                if next(gen, "done") == "done":
                    live.remove(gen)


def _attn_core(qkvg, lam_vecs, subw, layer_idx, ctx=None):
    B, T, _ = qkvg.shape
    H, DV = ATTN_HEADS, ATTN_V_DIM
    HS = min(H, max(1, ATTN_ROWS_PER_STEP // T))
    lam_init = 0.8 - 0.6 * math.exp(-0.3 * layer_idx)
    has_ctx = ctx is not None
    tq = min(ATTN_Q_TILE, T)
    tk = T + (ctx[0].shape[3] if has_ctx else 0)

    def col(off):
        return pl.BlockSpec((1, T, HS * DV), lambda b, h: (b, 0, off + h))

    nb = H // HS
    in_specs = [pl.BlockSpec((4, ATTN_HEAD_DIM), lambda b, h: (0, 0)),
                col(0), col(nb), col(2 * nb), col(3 * nb),
                pl.BlockSpec((1, DV), lambda b, h: (0, 0))]
    args = [lam_vecs, qkvg, qkvg, qkvg, qkvg, subw.reshape(1, DV)]
    if has_ctx:
        ck, cv, j = ctx
        P = ck.shape[3]
        cache_spec = pl.BlockSpec((None, None, HS, P, DV), lambda b, h: (b, j, h, 0, 0))
        in_specs += [cache_spec, cache_spec]
        args += [ck, cv]
    group = ATTN_TASK_GROUP if tk // MXU_DIM >= 2 * ATTN_TASK_GROUP else 1
    assert (HS * (T // tq)) % group == 0
    n_buf = 2 * group * 2
    return pl.pallas_call(
        functools.partial(_attn_core_kernel, lam_init=lam_init, has_ctx=has_ctx, tq=tq, group=group),
        grid=(B, nb),
        in_specs=in_specs,
        out_specs=pl.BlockSpec((1, T, HS * DV), lambda b, h: (b, 0, h)),
        out_shape=jax.ShapeDtypeStruct((B, T, D_INNER), BF16),
        scratch_shapes=([pltpu.VMEM((tk, DV), BF16)] * HS + [pltpu.VMEM((tk, 2 * DV), BF16)] * HS
                        + [pltpu.VMEM((tq, tk), F32)] * n_buf + [pltpu.VMEM((tq, tk), BF16)] * n_buf
                        + [pltpu.VMEM((tq, LANES), F32)] * n_buf),
        compiler_params=_cparams(("parallel", "parallel")),
        name="attn_core",
    )(*args)


def _gated_residual(o, w_ref, x_ref, mod_ref):
    return x_ref[0] + mod_ref[:, 2 * D_MODEL:3 * D_MODEL] * _dot(o, w_ref[...])


def _out_proj_kernel(o_ref, w_ref, x_ref, mod_ref, *rest, final_norm):
    xn = _gated_residual(o_ref[0], w_ref, x_ref, mod_ref)
    if final_norm:
        fw_ref, y_ref = rest
        ms = jnp.mean(xn * xn, axis=-1, keepdims=True)
        xn = xn * lax.rsqrt(ms + NORM_EPS) * fw_ref[...]
    else:
        (y_ref,) = rest
    y_ref[0] = xn


def _out_proj(o, w_bf16, x, mods, layer, row_of, final_w=None):
    B, T, D = x.shape
    tm = min(ROW_TILE, T)
    tile = pl.BlockSpec((1, tm, D), lambda b, t: (b, t, 0))
    in_specs = [tile, _const_spec((D, D)), tile, _mod_spec(layer, row_of)]
    args = [o, w_bf16, x, mods]
    if final_w is not None:
        in_specs.append(_const_spec((1, D)))
        args.append(final_w.reshape(1, D))
    return pl.pallas_call(
        functools.partial(_out_proj_kernel, final_norm=final_w is not None),
        grid=(B, T // tm),
        in_specs=in_specs,
        out_specs=tile,
        out_shape=jax.ShapeDtypeStruct((B, T, D), F32),
        compiler_params=_cparams(("parallel", "parallel")),
        name="out_proj",
    )(*args)


def _rwkv_in_kernel(x_ref, xp_ref, xn_ref, nw_ref, mod_ref, mu_ref, w_ref, l1_ref, w2_ref, a2_ref,
                    w0_ref, a0_ref, kk_ref, ka_ref, rk_ref, seg_ref,
                    r_out, kk_out, v_out, sg_out, bonus_out, lw_out, a_out, kz_out):
    t = pl.program_id(1)
    nt = pl.num_programs(1)
    nw = nw_ref[...]
    mod = mod_ref[...]
    h = _normmod(x_ref[0], nw, mod)
    tm = h.shape[0]
    hp = _normmod(xp_ref[0], nw, mod)[SUBLANES - 1:SUBLANES, :]
    hn = _normmod(xn_ref[0], nw, mod)[0:1, :]
    hp = jnp.where(t > 0, hp, 0.0)
    hn = jnp.where(t < nt - 1, hn, 0.0)
    row = lax.broadcasted_iota(jnp.int32, h.shape, 0)
    h_prev = jnp.where(row == 0, hp, pltpu.roll(h, 1, axis=0))
    h_next = jnp.where(row == tm - 1, hn, pltpu.roll(h, tm - 1, axis=0))
    dx = 0.5 * (h_prev + h_next) - h

    def mix(n):
        return (h + dx * mu_ref[n:n + 1, :]).astype(BF16)

    E = D_INNER
    r = _mm(mix(0), w_ref[:, 0:E])
    k = _mm(mix(1), w_ref[:, E:2 * E])
    v = _mm(mix(2), w_ref[:, 2 * E:3 * E])
    g = _mm(mix(3), w_ref[:, 3 * E:4 * E])
    lw1 = jnp.tanh(_mm(mix(4), l1_ref[:, 0:2 * LORA]))
    la1 = _mm(mix(5), l1_ref[:, 2 * LORA:4 * LORA])
    lane = lax.broadcasted_iota(jnp.int32, lw1.shape, 1)

    kk = k * kk_ref[...]
    kk = kk * lax.rsqrt(jnp.maximum(_headsum(kk * kk, seg_ref), 1e-24))
    r_out[0] = r
    kk_out[0] = kk
    v_out[0] = v
    sg_out[0] = _silu(g).astype(BF16)

    ksum = None
    for z in range(2):
        sel = (lane >= z * LORA) & (lane < (z + 1) * LORA)
        lw = _dot(jnp.where(sel, lw1, 0.0), w2_ref[...])
        la = _dot(jnp.where(sel, la1, 0.0), a2_ref[...])
        lw_out[z, 0] = -math.exp(-0.5) * jax.nn.sigmoid(w0_ref[z:z + 1, :] + lw)
        a = jax.nn.sigmoid(a0_ref[z:z + 1, :] + la)
        a_out[z, 0] = a
        kz = k * (1.0 + (a - 1.0) * ka_ref[...])
        kz_out[z, 0] = kz
        ksum = kz if ksum is None else ksum + kz
    bonus_out[0] = (_headsum(r * rk_ref[...] * ksum, seg_ref) * v).astype(BF16)


def _rwkv_in(x, nw, mods, layer, row_of, p):
    B, T, D = x.shape
    E = D_INNER
    tm = min(RWKV_ROW_TILE, T)
    g8 = tm // SUBLANES
    last8 = T // SUBLANES - 1
    tile = pl.BlockSpec((1, tm, D), lambda b, t: (b, t, 0))
    prev8 = pl.BlockSpec((1, SUBLANES, D), lambda b, t: (b, jnp.maximum(t * g8 - 1, 0), 0))
    next8 = pl.BlockSpec((1, SUBLANES, D), lambda b, t: (b, jnp.minimum((t + 1) * g8, last8), 0))
    otile = pl.BlockSpec((1, tm, E), lambda b, t: (b, t, 0))
    ztile = pl.BlockSpec((2, 1, tm, E), lambda b, t: (0, b, t, 0))
    one = jax.ShapeDtypeStruct((B, T, E), F32)
    two = jax.ShapeDtypeStruct((2, B, T, E), F32)
    return pl.pallas_call(
        _rwkv_in_kernel,
        grid=(B, T // tm),
        in_specs=[tile, prev8, next8, _const_spec((1, D)), _mod_spec(layer, row_of),
                  _const_spec((6, D)), _const_spec((D, 4 * E)), _const_spec((D, 4 * LORA)),
                  _const_spec((2 * LORA, E)), _const_spec((2 * LORA, E)),
                  _const_spec((2, E)), _const_spec((2, E)),
                  _const_spec((1, E)), _const_spec((1, E)), _const_spec((1, E)),
                  _const_spec((MXU_DIM, MXU_DIM))],
        out_specs=[otile] * 5 + [ztile] * 3,
        out_shape=[one] * 3 + [jax.ShapeDtypeStruct((B, T, E), BF16)] * 2 + [two] * 3,
        compiler_params=_cparams(("parallel", "parallel")),
        name="rwkv_in",
    )(x, x, x, nw.reshape(1, D), mods, p["mu"], p["w_in"], p["l1"], p["w2"], p["a2"],
      p["w0"], p["a0"], p["k_k"], p["k_a"], p["r_k"], p["seg"])


def _scan_kernel(*refs, has_init):
    n_in = 6
    ins = [refs[z * n_in:(z + 1) * n_in] for z in range(2)]
    rest = refs[2 * n_in:]
    if has_init:
        s0_refs, rest = rest[0:2], rest[2:]
    y_refs, sf_refs, s_scr = rest[0:2], rest[2:4], rest[4]
    C = SCAN_CHUNK
    PW = 2 * C
    HD = RWKV_HEAD_DIM
    NP = y_refs[0].shape[-1] // LANES
    chains = [(z, p) for z in range(2) for p in range(NP)]
    idx = range(len(chains))
    zs = [z for z, _ in chains]

    @pl.when(pl.program_id(1) == 0)
    def _():
        for z in range(2):
            s_scr[z] = s0_refs[z][...] if has_init else jnp.zeros_like(s_scr[z])

    ri = lax.broadcasted_iota(jnp.int32, (PW, PW), 0)
    ci = lax.broadcasted_iota(jnp.int32, (PW, PW), 1)
    same_blk = (ri // C) == (ci // C)
    tr, tc = ri % C, ci % C
    same_head = (ri // HD) == (ci // HD)
    diag = ri == ci
    eye = jnp.where(diag, 1.0, 0.0)
    li = lax.broadcasted_iota(jnp.int32, (C, C), 0)
    lj = lax.broadcasted_iota(jnp.int32, (C, C), 1)
    strict, incl, cum_mat = [], [], []
    for z in range(2):
        before = (tc > tr) if z == 1 else (tc < tr)
        strict.append(same_blk & before)
        incl.append(same_blk & (before | (tc == tr)))
        cum_mat.append(jnp.where((lj >= li) if z == 1 else (lj <= li), 1.0, 0.0).astype(BF16))
    lo = lax.broadcasted_iota(jnp.int32, (C, LANES), 1) < HD
    zeros_c = jnp.zeros((C, LANES), F32)
    zeros_p = jnp.zeros((PW, LANES), F32)

    def stack2(x):
        return jnp.concatenate([x, x], axis=0)

    def unstack(x2):
        return jnp.where(lo, x2[0:C], x2[C:PW])

    def load(z, which, p):
        ref = ins[z][which]
        sl = slice(p * LANES, (p + 1) * LANES)
        return ref[0, :, sl] if len(ref.shape) == 3 else ref[0, 0, :, sl]

    lw = [load(z, 3, p) for z, p in chains]
    cum = []
    for i in idx:
        hi, lo16 = _split2(lw[i])
        cum.append(_mm(cum_mat[zs[i]], hi) + _mm(cum_mat[zs[i]], lo16))
    tot = [(cum[i][0:1] if zs[i] == 1 else cum[i][C - 1:C]) for i in idx]
    abar, rbar, bh, kh, v, res = [], [], [], [], [], []
    for i, (z, p) in enumerate(chains):
        kk = load(z, 1, p)
        b = kk * load(z, 4, p)
        kz = load(z, 5, p)
        e_neg = jnp.exp(-cum[i])
        e_tot = jnp.exp(tot[i] - cum[i])
        abar.append(-kk * jnp.exp(cum[i] - lw[i]))
        rbar.append(load(z, 0, p) * jnp.exp(cum[i]))
        bt, kt = b * e_neg, kz * e_neg
        bh.append(b * e_tot)
        kh.append(kz * e_tot)
        v.append(load(z, 2, p))
        lhs = jnp.concatenate([abar[i], rbar[i]], axis=0)
        rhs = jnp.concatenate([jnp.where(lo, bt, 0.0), jnp.where(lo, 0.0, bt),
                               jnp.where(lo, kt, 0.0), jnp.where(lo, 0.0, kt)], axis=0)
        res.append(_mm(lhs.astype(BF16), rhs.astype(BF16), DN_NT))
    a_ab = [jnp.where(strict[zs[i]], stack2(res[i][0:C, 0:LANES]), 0.0) for i in idx]
    a_ak = [jnp.where(strict[zs[i]], stack2(res[i][0:C, LANES:2 * LANES]), 0.0) for i in idx]
    a_rb = [jnp.where(incl[zs[i]], stack2(res[i][C:PW, 0:LANES]), 0.0) for i in idx]
    a_rk = [jnp.where(incl[zs[i]], stack2(res[i][C:PW, LANES:2 * LANES]), 0.0) for i in idx]
    v2 = [stack2(x) for x in v]
    wv = [_dot(a_ak[i], v2[i]) for i in idx]

    pw = [_dot(x, x) for x in a_ab]
    q = [eye + x for x in a_ab]
    n = 4
    while n < C:
        m = [_dot(pw[i], jnp.concatenate([q[i], pw[i]], axis=1)) for i in idx]
        q = [q[i] + m[i][:, 0:LANES] for i in idx]
        pw = [x[:, LANES:2 * LANES] for x in m]
        n *= 2
    q = [q[i] + _dot(pw[i], q[i]) for i in idx]

    x = [_dot(q[i], jnp.concatenate([stack2(abar[i]), wv[i]], axis=1)) for i in idx]
    ahat = [unstack(t[:, 0:LANES]) for t in x]
    u0 = [unstack(t[:, LANES:2 * LANES]) for t in x]

    ry = [_dot(jnp.concatenate([a_rb[i], a_rk[i]], axis=1),
               jnp.concatenate([jnp.concatenate([stack2(ahat[i]), stack2(u0[i])], axis=1),
                                jnp.concatenate([zeros_p, v2[i]], axis=1)], axis=0)) for i in idx]
    rhat = [rbar[i] + unstack(ry[i][:, 0:LANES]) for i in idx]
    y0 = [unstack(t[:, LANES:2 * LANES]) for t in ry]

    mn = [_dot(jnp.concatenate([bh[i], kh[i]], axis=0).T,
               jnp.concatenate([jnp.concatenate([ahat[i], u0[i]], axis=1),
                                jnp.concatenate([zeros_c, v[i]], axis=1)], axis=0)) for i in idx]
    for i, (z, p) in enumerate(chains):
        m_c = jnp.where(same_head, mn[i][:, 0:LANES], 0.0) + jnp.where(diag, jnp.exp(tot[i]), 0.0)
        n_c = jnp.where(same_head, mn[i][:, LANES:2 * LANES], 0.0)
        s_old = s_scr[z, p]
        y_refs[z][0, :, p * LANES:(p + 1) * LANES] = _dot(rhat[i], s_old) + y0[i]
        s_new = _dot(m_c, s_old) + n_c
        s_scr[z, p] = s_new
        sf_refs[z][p] = s_new


def _rwkv_scan(r, kk, v, lw, a, kz, s0=None):
    B, T, E = r.shape
    NP = E // LANES
    C = SCAN_CHUNK
    nb = T // C
    state = pl.BlockSpec((None, NP, LANES, LANES), lambda b, t: (b, 0, 0, 0))
    in_specs, args, y_specs = [], [], []
    for z in range(2):
        tok = (lambda t: t) if z == 0 else (lambda t: nb - 1 - t)
        one = pl.BlockSpec((1, C, E), lambda b, t, tok=tok: (b, tok(t), 0))
        two = pl.BlockSpec((1, 1, C, E), lambda b, t, tok=tok, z=z: (z, b, tok(t), 0))
        in_specs += [one, one, one, two, two, two]
        args += [r, kk, v, lw, a, kz]
        y_specs.append(one)
    if s0 is not None:
        in_specs += [state, state]
        args += list(s0)
    return pl.pallas_call(
        functools.partial(_scan_kernel, has_init=s0 is not None),
        grid=(B, nb),
        in_specs=in_specs,
        out_specs=y_specs + [state, state],
        out_shape=[jax.ShapeDtypeStruct((B, T, E), F32)] * 2
        + [jax.ShapeDtypeStruct((B, NP, LANES, LANES), F32)] * 2,
        scratch_shapes=[pltpu.VMEM((2, NP, LANES, LANES), F32)],
        compiler_params=_cparams(("parallel", "arbitrary")),
        name="rwkv_scan",
    )(*args)


def _rwkv_out_kernel(y0_ref, y1_ref, bonus_ref, sg_ref, lnw_ref, lnb_ref, seg_ref, w_ref, x_ref, mod_ref, o_ref):
    y = y0_ref[0] + y1_ref[0]
    inv_n = 1.0 / RWKV_HEAD_DIM
    d = y - _headsum(y, seg_ref) * inv_n
    var = _headsum(d * d, seg_ref) * inv_n
    yn = d * lax.rsqrt(var + RWKV_GN_EPS) * lnw_ref[...] + lnb_ref[...] + bonus_ref[0].astype(F32)
    o_ref[0] = _gated_residual(yn * sg_ref[0].astype(F32), w_ref, x_ref, mod_ref)


def _rwkv_out(y0, y1, bonus, sg, p, x, mods, layer, row_of):
    B, T, D = x.shape
    tm = min(ROW_TILE, T)
    tile = pl.BlockSpec((1, tm, D), lambda b, t: (b, t, 0))
    return pl.pallas_call(
        _rwkv_out_kernel,
        grid=(B, T // tm),
        in_specs=[tile, tile, tile, tile, _const_spec((1, D)), _const_spec((1, D)),
                  _const_spec((MXU_DIM, MXU_DIM)), _const_spec((D, D)), tile, _mod_spec(layer, row_of)],
        out_specs=tile,
        out_shape=jax.ShapeDtypeStruct((B, T, D), F32),
        compiler_params=_cparams(("parallel", "parallel")),
        name="rwkv_out",
    )(y0, y1, bonus, sg, p["ln_w"], p["ln_b"], p["seg"], p["w_out"], x, mods)


def _conv_in_kernel(x_ref, nw_ref, mod_ref, w_ref, z_ref, sg_ref):
    hb = _normmod(x_ref[0], nw_ref[...], mod_ref[...]).astype(BF16)
    E = D_INNER
    for n in range(0, E, N_CHUNK):
        a = _mm(hb, w_ref[:, n:n + N_CHUNK])
        b = _mm(hb, w_ref[:, E + n:E + n + N_CHUNK])
        g = _mm(hb, w_ref[:, 2 * E + n:2 * E + n + N_CHUNK])
        z_ref[0, :, n:n + N_CHUNK] = a * jax.nn.sigmoid(b)
        sg_ref[0, :, n:n + N_CHUNK] = _silu(g).astype(BF16)


def _conv_in(x, nw, mods, layer, row_of, w_bf16):
    B, T, D = x.shape
    E = D_INNER
    tm = min(ROW_TILE, T)
    tile = pl.BlockSpec((1, tm, D), lambda b, t: (b, t, 0))
    return pl.pallas_call(
        _conv_in_kernel,
        grid=(B, T // tm),
        in_specs=[tile, _const_spec((1, D)), _mod_spec(layer, row_of), _const_spec((D, 3 * E))],
        out_specs=[tile, tile],
        out_shape=[jax.ShapeDtypeStruct((B, T, E), F32), jax.ShapeDtypeStruct((B, T, E), BF16)],
        compiler_params=_cparams(("parallel", "parallel")),
        name="conv_in",
    )(x, nw.reshape(1, D), mods, w_bf16)


def _conv_out_kernel(z_ref, zp_ref, zn_ref, sg_ref, dw_ref, dwb_ref, lnw_ref, lnb_ref, w_ref, x_ref, mod_ref,
                     o_ref, zs, cbuf):
    t = pl.program_id(1)
    nt = pl.num_programs(1)
    tm = z_ref.shape[1]
    rows = tm + 2 * HALO
    zs[0, 0:HALO, :] = jnp.where(t > 0, zp_ref[0], 0.0)
    zs[0, HALO:HALO + tm, :] = z_ref[0]
    zs[0, HALO + tm:rows, :] = jnp.where(t < nt - 1, zn_ref[0], 0.0)
    for s in range(1, SUBLANES):
        zs[s, 0:rows - SUBLANES, :] = zs[0, s:s + rows - SUBLANES, :]
    off = HALO - CONV_PAD
    for c0 in range(0, D_INNER, LANES):
        for r0 in range(0, tm, CONV_ROWS):
            acc = None
            for k in range(CONV_WIDTH):
                q8, s = divmod(off + k, SUBLANES)
                a0 = r0 + q8 * SUBLANES
                term = dw_ref[k:k + 1, c0:c0 + LANES] * zs[s, a0:a0 + CONV_ROWS, c0:c0 + LANES]
                acc = term if acc is None else acc + term
            cbuf[r0:r0 + CONV_ROWS, c0:c0 + LANES] = acc
    c = cbuf[...] + dwb_ref[...]
    m = jnp.mean(c, axis=-1, keepdims=True)
    d = c - m
    var = jnp.mean(d * d, axis=-1, keepdims=True)
    y = d * lax.rsqrt(var + 1e-5) * lnw_ref[...] + lnb_ref[...]
    o_ref[0] = _gated_residual(_silu(y) * sg_ref[0].astype(F32), w_ref, x_ref, mod_ref)


def _conv_out(z, sg, p, x, mods, layer, row_of):
    B, T, D = x.shape
    tm = min(ROW_TILE, T)
    gh = tm // HALO
    lasth = T // HALO - 1
    tile = pl.BlockSpec((1, tm, D), lambda b, t: (b, t, 0))
    prev = pl.BlockSpec((1, HALO, D), lambda b, t: (b, jnp.maximum(t * gh - 1, 0), 0))
    nxt = pl.BlockSpec((1, HALO, D), lambda b, t: (b, jnp.minimum((t + 1) * gh, lasth), 0))
    return pl.pallas_call(
        _conv_out_kernel,
        grid=(B, T // tm),
        in_specs=[tile, prev, nxt, tile, _const_spec((CONV_WIDTH, D)), _const_spec((1, D)),
                  _const_spec((1, D)), _const_spec((1, D)), _const_spec((D, D)), tile,
                  _mod_spec(layer, row_of)],
        out_specs=tile,
        out_shape=jax.ShapeDtypeStruct((B, T, D), F32),
        scratch_shapes=[pltpu.VMEM((SUBLANES, tm + 2 * HALO, D), F32), pltpu.VMEM((tm, D), F32)],
        compiler_params=_cparams(("parallel", "parallel")),
        name="conv_out",
    )(z, z, z, sg, p["dw_w"], p["dw_b"], p["ln_w"], p["ln_b"], p["w_out"], x, mods)


def _rope_tables(n_tok):
    t = jnp.arange(n_tok, dtype=jnp.int32)
    pos = jnp.stack([(t // GRID_W).astype(F32), (t % GRID_W).astype(F32)], axis=1)
    lane = jnp.arange(LANES, dtype=jnp.int32)
    d = lane % ATTN_HEAD_DIM
    axis = d // 32
    second_half = (d % 32) // 16
    axis_dim = ATTN_HEAD_DIM // 2
    inv_freq = ROPE_BASE ** (-(2.0 * (d % 16).astype(F32)) / axis_dim)
    ang = pos[:, axis] * inv_freq[None, :]
    sign = jnp.where(second_half == 1, 1.0, -1.0).astype(F32)
    return jnp.cos(ang), jnp.sin(ang) * sign[None, :]


def _pair_blockdiag(s):
    B, H, N, _ = s.shape
    st = jnp.swapaxes(s, -1, -2).reshape(B, H // 2, 2, N, N)
    z = jnp.zeros_like(st[:, :, 0])
    top = jnp.concatenate([st[:, :, 0], z], axis=-1)
    bot = jnp.concatenate([z, st[:, :, 1]], axis=-1)
    return jnp.concatenate([top, bot], axis=-2)


def _pair_unblock(sp):
    B, NP, _, _ = sp.shape
    N = RWKV_HEAD_DIM
    blocks = jnp.stack([sp[:, :, 0:N, 0:N], sp[:, :, N:2 * N, N:2 * N]], axis=2)
    return jnp.swapaxes(blocks, -1, -2).reshape(B, 2 * NP, N, N)


def kernel(x_prompt, x_sample, cache_attn_k, cache_attn_v, state_rwkv, c, c_ctx, norm_w, ada_w, ada_b, attn_w_in, attn_lambda, attn_subln_w, attn_w_out, rwkv_mu, rwkv_w_in, rwkv_w0, rwkv_w1, rwkv_w2, rwkv_a0, rwkv_a1, rwkv_a2, rwkv_k_k, rwkv_k_a, rwkv_r_k, rwkv_ln_w, rwkv_ln_b, rwkv_w_out, conv_w_in, conv_dw_w, conv_dw_b, conv_ln_w, conv_ln_b, conv_w_out, final_norm_w):
    D, E = D_MODEL, D_INNER
    dec_batch = x_sample.shape[0]
    assert dec_batch < COND_ROWS
    cond = jnp.concatenate([c, c_ctx[None, :], jnp.zeros((COND_ROWS - dec_batch - 1, D), F32)], axis=0)
    mods = _ada(cond, ada_w, ada_b)
    ctx_row = dec_batch
    streams = [(x_prompt, lambda b: ctx_row), (x_sample, lambda b: b)]
    cos, sin = _rope_tables(x_sample.shape[1])
    seg = (jnp.arange(MXU_DIM)[:, None] // RWKV_HEAD_DIM == jnp.arange(MXU_DIM)[None, :] // RWKV_HEAD_DIM).astype(BF16)

    xs = [x_prompt, x_sample]
    new_kv, new_s = (None, None), []
    assert (DEPTH - 1) % N_MIXERS == 0
    for i in range(DEPTH):
        kind, j = i % N_MIXERS, i // N_MIXERS
        last = i == DEPTH - 1
        fw = final_norm_w if last else None
        if kind == 0:
            w_in = attn_w_in[j].astype(BF16)
            w_out = attn_w_out[j].astype(BF16)
            for s, (_, row_of) in enumerate(streams):
                if s == 0:
                    qkvg, *new_kv = _attn_in(xs[s], norm_w[i], mods, i, row_of, w_in, cache=new_kv)
                    o = _attn_core(qkvg, attn_lambda[j], attn_subln_w[j], i)
                else:
                    (qkvg,) = _attn_in(xs[s], norm_w[i], mods, i, row_of, w_in, rope=(cos, sin))
                    o = _attn_core(qkvg, attn_lambda[j], attn_subln_w[j], i, ctx=(cache_attn_k, cache_attn_v, j))
                xs[s] = _out_proj(o, w_out, xs[s], mods, i, row_of, final_w=fw)
        elif kind == 1:
            p = dict(
                mu=rwkv_mu[j],
                w_in=rwkv_w_in[j].astype(BF16),
                l1=jnp.concatenate([rwkv_w1[j, 0], rwkv_w1[j, 1], rwkv_a1[j, 0], rwkv_a1[j, 1]], axis=1).astype(BF16),
                w2=rwkv_w2[j].reshape(2 * LORA, E).astype(BF16),
                a2=rwkv_a2[j].reshape(2 * LORA, E).astype(BF16),
                w0=rwkv_w0[j], a0=rwkv_a0[j],
                k_k=rwkv_k_k[j].reshape(1, E), k_a=rwkv_k_a[j].reshape(1, E), r_k=rwkv_r_k[j].reshape(1, E),
                ln_w=rwkv_ln_w[j].reshape(1, E), ln_b=rwkv_ln_b[j].reshape(1, E),
                w_out=rwkv_w_out[j].astype(BF16), seg=seg)
            for s, (_, row_of) in enumerate(streams):
                r, kk, v, sg, bonus, lw, a, kz = _rwkv_in(xs[s], norm_w[i], mods, i, row_of, p)
                s0 = [_pair_blockdiag(state_rwkv[:, j, z]) for z in range(2)] if s == 1 else None
                y_f, y_b, sf_f, sf_b = _rwkv_scan(r, kk, v, lw, a, kz, s0)
                if s == 0:
                    new_s.append(jnp.stack([_pair_unblock(sf_f), _pair_unblock(sf_b)], axis=1))
                xs[s] = _rwkv_out(y_f, y_b, bonus, sg, p, xs[s], mods, i, row_of)
        else:
            p = dict(dw_w=conv_dw_w[j], dw_b=conv_dw_b[j].reshape(1, E), ln_w=conv_ln_w[j].reshape(1, E),
                     ln_b=conv_ln_b[j].reshape(1, E), w_out=conv_w_out[j].astype(BF16))
            w_in = conv_w_in[j].astype(BF16)
            for s, (_, row_of) in enumerate(streams):
                zz, sg = _conv_in(xs[s], norm_w[i], mods, i, row_of, w_in)
                xs[s] = _conv_out(zz, sg, p, xs[s], mods, i, row_of)
    return (xs[0], xs[1], new_kv[0], new_kv[1], jnp.stack(new_s, axis=1))
```

```python
import functools
import math

import jax
import jax.numpy as jnp
from jax import lax
from jax.experimental import pallas as pl
from jax.experimental.pallas import tpu as pltpu

F32 = jnp.float32
BF16 = jnp.bfloat16

D_MODEL = 1024
D_INNER = 1024
DEPTH = 4
N_MIXERS = 3
GRID_W = 64
NORM_EPS = 1e-6
ATTN_HEAD_DIM = 64
ATTN_HEADS = 8
ATTN_V_DIM = 128
ROPE_BASE = 10000.0
RWKV_HEAD_DIM = 64
RWKV_HEADS = 16
RWKV_GN_EPS = RWKV_HEAD_DIM * 1e-5
LORA = 64
CONV_WIDTH = 31
CONV_PAD = CONV_WIDTH // 2

LANES = 128
SUBLANES = 8
MXU_DIM = 256
VMEM_LIMIT_BYTES = 56 * 1024 * 1024

ROW_TILE = 512
CONV_ROW_TILE = 256
RWKV_ROW_TILE = 128
ATTN_Q_TILE = 256
ATTN_ROWS_PER_STEP = 2048
ATTN_TASK_GROUP = 2
N_CHUNK = 512
SCAN_CHUNK = 64
HALO = 16
CONV_ROWS = 64
COND_ROWS = 16
DN = (((1,), (0,)), ((), ()))
DN_NT = (((1,), (1,)), ((), ()))


def _cparams(sem):
    return pltpu.CompilerParams(dimension_semantics=sem, vmem_limit_bytes=VMEM_LIMIT_BYTES)


def _mm(a, b, dn=DN):
    return lax.dot_general(a, b, dn, preferred_element_type=F32)


def _dot(a, b):
    return _mm(a.astype(BF16), b.astype(BF16))


def _split2(x):
    hi = x.astype(BF16)
    lo = (x - hi.astype(F32)).astype(BF16)
    return hi, lo


def _dot3(a, b):
    a_hi, a_lo = _split2(a)
    b_hi, b_lo = _split2(b)
    return _mm(a_hi, b_hi) + (_mm(a_lo, b_hi) + _mm(a_hi, b_lo))


def _silu(x):
    return x * jax.nn.sigmoid(x)


def _normmod(x, nw, mod):
    ms = jnp.mean(x * x, axis=-1, keepdims=True)
    y = x * lax.rsqrt(ms + NORM_EPS) * nw
    return y * (1.0 + mod[:, D_MODEL:2 * D_MODEL]) + mod[:, 0:D_MODEL]


def _headsum(x, seg_ref):
    cols = []
    for c in range(0, x.shape[-1], MXU_DIM):
        hi, lo = _split2(x[:, c:c + MXU_DIM])
        cols.append(_mm(hi, seg_ref[...]) + _mm(lo, seg_ref[...]))
    return jnp.concatenate(cols, axis=-1)


def _mod_spec(layer, row_of):
    return pl.BlockSpec((None, None, 1, 3 * D_MODEL), lambda b, t: (layer, row_of(b), 0, 0))


def _const_spec(shape):
    return pl.BlockSpec(shape, lambda b, t: (0,) * len(shape))


def _ada_kernel(cond_ref, w_ref, b_ref, o_ref):
    o_ref[0, :, 0, :] = _dot3(_silu(cond_ref[...]), w_ref[0]) + b_ref[0]


def _ada(cond, ada_w, ada_b):
    depth, d, n3 = ada_w.shape
    tn = 1024
    return pl.pallas_call(
        _ada_kernel,
        grid=(depth, n3 // tn),
        in_specs=[
            pl.BlockSpec((COND_ROWS, d), lambda i, j: (0, 0)),
            pl.BlockSpec((1, d, tn), lambda i, j: (i, 0, j)),
            pl.BlockSpec((1, 1, tn), lambda i, j: (i, 0, j)),
        ],
        out_specs=pl.BlockSpec((1, COND_ROWS, 1, tn), lambda i, j: (i, 0, 0, j)),
        out_shape=jax.ShapeDtypeStruct((depth, COND_ROWS, 1, n3), F32),
        compiler_params=_cparams(("parallel", "parallel")),
        name="ada_mod",
    )(cond, ada_w, ada_b.reshape(depth, 1, n3))


ATTN_Q_SCALE = ATTN_HEAD_DIM ** -0.5 * math.log2(math.e)


def _attn_in_kernel(x_ref, nw_ref, mod_ref, w_ref, *rest, rope, n_cached):
    rest = list(rest)
    if rope:
        cos_ref, sin_ref = rest.pop(0), rest.pop(0)
    if n_cached:
        pk_ref, pv_ref = rest.pop(0), rest.pop(0)
    o_ref = rest.pop(0)
    if n_cached is not None:
        ko_ref, vo_ref = rest
        if n_cached:
            ko_ref[0:n_cached] = pk_ref[...]
            vo_ref[0:n_cached] = pv_ref[...]
    E = D_INNER
    hb = _normmod(x_ref[0], nw_ref[...], mod_ref[...]).astype(BF16)
    for n in range(0, 4 * E, N_CHUNK):
        y = _mm(hb, w_ref[:, n:n + N_CHUNK])
        if n_cached is not None and E <= n < 3 * E:
            cache_ref = ko_ref if n < 2 * E else vo_ref
            for c in range(0, N_CHUNK, LANES):
                cache_ref[n_cached, (n % E + c) // LANES] = y[:, c:c + LANES]
        if n < E:
            y = y * ATTN_Q_SCALE
        if rope and n < 2 * E:
            y = jnp.concatenate([_rope(y[:, c:c + LANES], cos_ref[...], sin_ref[...])
                                 for c in range(0, N_CHUNK, LANES)], axis=1)
        o_ref[0, :, n:n + N_CHUNK] = y.astype(BF16)


def _attn_in(x, nw, mods, layer, row_of, w_bf16, rope=None, cache=None):
    B, T, D = x.shape
    N = w_bf16.shape[1]
    H, DV = ATTN_HEADS, ATTN_V_DIM
    tm = min(ROW_TILE, T)
    in_specs = [pl.BlockSpec((1, tm, D), lambda b, t: (b, t, 0)), _const_spec((1, D)),
                _mod_spec(layer, row_of), _const_spec((D, N))]
    args = [x, nw.reshape(1, D), mods, w_bf16]
    if rope is not None:
        in_specs += [pl.BlockSpec((tm, LANES), lambda b, t: (t, 0))] * 2
        args += list(rope)
    out_specs = [pl.BlockSpec((1, tm, N), lambda b, t: (b, t, 0))]
    out_shape = [jax.ShapeDtypeStruct((B, T, N), BF16)]
    n_cached = None
    if cache is not None:
        n_cached = 0 if cache[0] is None else cache[0].shape[1]
        if n_cached:
            in_specs += [pl.BlockSpec((None, n_cached, H, tm, DV), lambda b, t: (b, 0, 0, t, 0))] * 2
            args += list(cache)
        out_specs += [pl.BlockSpec((None, n_cached + 1, H, tm, DV), lambda b, t: (b, 0, 0, t, 0))] * 2
        out_shape += [jax.ShapeDtypeStruct((B, n_cached + 1, H, T, DV), F32)] * 2
    return pl.pallas_call(
        functools.partial(_attn_in_kernel, rope=rope is not None, n_cached=n_cached),
        grid=(B, T // tm),
        in_specs=in_specs,
        out_specs=out_specs,
        out_shape=out_shape,
        compiler_params=_cparams(("parallel", "parallel")),
        name="attn_in",
    )(*args)


def _rope(x, cos, sin):
    lane = lax.broadcasted_iota(jnp.int32, x.shape, 1)
    first = (lane % 32) < 16
    partner = jnp.where(first, pltpu.roll(x, LANES - 16, axis=1), pltpu.roll(x, 16, axis=1))
    return x * cos + partner * sin


def _attn_core_kernel(*refs, lam_init, has_ctx, tq, group):
    it = iter(refs)
    lam_ref, q_ref, k_ref, v_ref, g_ref, subw_ref = (next(it) for _ in range(6))
    if has_ctx:
        ck_ref, cv_ref = next(it), next(it)
    o_ref = next(it)
    T = q_ref.shape[1]
    n_heads = q_ref.shape[2] // LANES
    n_tiles = T // tq
    k_scr = [next(it) for _ in range(n_heads)]
    v_scr = [next(it) for _ in range(n_heads)]
    ring = 2 * group
    s_scr = [[next(it), next(it)] for _ in range(ring)]
    e_scr = [[next(it), next(it)] for _ in range(ring)]
    m_scr = [[next(it), next(it)] for _ in range(ring)]

    lv = lam_ref[...]
    lam = (jnp.exp(jnp.sum(lv[0:1] * lv[1:2], axis=-1, keepdims=True))
           - jnp.exp(jnp.sum(lv[2:3] * lv[3:4], axis=-1, keepdims=True)) + lam_init)

    tk = k_scr[0].shape[0]
    for hh in range(n_heads):
        hl = slice(hh * LANES, (hh + 1) * LANES)
        if has_ctx:
            k_scr[hh][T:, :] = ck_ref[hh].astype(BF16)
            v_scr[hh][T:, 0:LANES] = cv_ref[hh].astype(BF16)
        k_scr[hh][0:T, :] = k_ref[0, :, hl]
        v_scr[hh][0:T, 0:LANES] = v_ref[0, :, hl]
        v_scr[hh][:, LANES:2 * LANES] = jnp.ones((tk, LANES), BF16)

    kblocks = [slice(c, c + MXU_DIM) for c in range(0, tk, MXU_DIM)]
    tasks = [(hh, i) for hh in range(n_heads) for i in range(n_tiles)]

    def halves_max(x):
        return jnp.maximum(x[:, 0:LANES], x[:, LANES:2 * LANES])

    def score_stage(n):
        hh, i = tasks[n]
        q = q_ref[0, i * tq:(i + 1) * tq, hh * LANES:(hh + 1) * LANES]
        lane = lax.broadcasted_iota(jnp.int32, q.shape, 1)
        for mp in range(2):
            sel = (lane < ATTN_HEAD_DIM) if mp == 0 else (lane >= ATTN_HEAD_DIM)
            qm = jnp.where(sel, q, jnp.zeros_like(q))
            mx = None
            for kb in kblocks:
                s = _mm(qm, k_scr[hh][kb, :], DN_NT)
                s_scr[n % ring][mp][:, kb] = s
                mx = halves_max(s) if mx is None else jnp.maximum(mx, halves_max(s))
                yield
            m_scr[n % ring][mp][...] = jnp.broadcast_to(jnp.max(mx, axis=-1, keepdims=True), mx.shape)

    def attend_stage(n):
        hh, i = tasks[n]
        outs = []
        for mp in range(2):
            m = m_scr[n % ring][mp][...]
            m = jnp.concatenate([m, m], axis=1)
            for kb in kblocks:
                e_scr[n % ring][mp][:, kb] = jnp.exp2(s_scr[n % ring][mp][:, kb] - m).astype(BF16)
                yield
            ov = _mm(e_scr[n % ring][mp][...], v_scr[hh][...])
            outs.append((ov[:, 0:LANES], ov[:, LANES:2 * LANES]))
            yield
        (o1, d1), (o2, d2) = outs
        o = o1 * (1.0 / d1) - o2 * (lam / d2)
        ms = jnp.mean(o * o, axis=-1, keepdims=True)
        o = o * lax.rsqrt(ms + 1e-5) * subw_ref[...] * (1.0 - lam_init)
        rows, hl = slice(i * tq, (i + 1) * tq), slice(hh * LANES, (hh + 1) * LANES)
        o_ref[0, rows, hl] = (o * _silu(g_ref[0, rows, hl].astype(F32))).astype(BF16)

    n_groups = len(tasks) // group
    for t in range(n_groups + 1):
        live = []
        if t < n_groups:
            live += [score_stage(t * group + u) for u in range(group)]
        if t >= 1:
            live += [attend_stage((t - 1) * group + u) for u in range(group)]
        while live:
            for gen in list(live):
                if next(gen, "done") == "done":
                    live.remove(gen)


def _attn_core(qkvg, lam_vecs, subw, layer_idx, ctx=None):
    B, T, _ = qkvg.shape
    H, DV = ATTN_HEADS, ATTN_V_DIM
    HS = min(H, max(1, ATTN_ROWS_PER_STEP // T))
    lam_init = 0.8 - 0.6 * math.exp(-0.3 * layer_idx)
    has_ctx = ctx is not None
    tq = min(ATTN_Q_TILE, T)
    tk = T + (ctx[0].shape[3] if has_ctx else 0)

    def col(off):
        return pl.BlockSpec((1, T, HS * DV), lambda b, h: (b, 0, off + h))

    nb = H // HS
    in_specs = [pl.BlockSpec((4, ATTN_HEAD_DIM), lambda b, h: (0, 0)),
                col(0), col(nb), col(2 * nb), col(3 * nb),
                pl.BlockSpec((1, DV), lambda b, h: (0, 0))]
    args = [lam_vecs, qkvg, qkvg, qkvg, qkvg, subw.reshape(1, DV)]
    if has_ctx:
        ck, cv, j = ctx
        P = ck.shape[3]
        cache_spec = pl.BlockSpec((None, None, HS, P, DV), lambda b, h: (b, j, h, 0, 0))
        in_specs += [cache_spec, cache_spec]
        args += [ck, cv]
    group = ATTN_TASK_GROUP if tk // MXU_DIM >= 2 * ATTN_TASK_GROUP else 1
    assert (HS * (T // tq)) % group == 0
    n_buf = 2 * group * 2
    return pl.pallas_call(
        functools.partial(_attn_core_kernel, lam_init=lam_init, has_ctx=has_ctx, tq=tq, group=group),
        grid=(B, nb),
        in_specs=in_specs,
        out_specs=pl.BlockSpec((1, T, HS * DV), lambda b, h: (b, 0, h)),
        out_shape=jax.ShapeDtypeStruct((B, T, D_INNER), BF16),
        scratch_shapes=([pltpu.VMEM((tk, DV), BF16)] * HS + [pltpu.VMEM((tk, 2 * DV), BF16)] * HS
                        + [pltpu.VMEM((tq, tk), F32)] * n_buf + [pltpu.VMEM((tq, tk), BF16)] * n_buf
                        + [pltpu.VMEM((tq, LANES), F32)] * n_buf),
        compiler_params=_cparams(("parallel", "parallel")),
        name="attn_core",
    )(*args)


def _gated_residual(o, w_ref, x_ref, mod_ref):
    return x_ref[0] + mod_ref[:, 2 * D_MODEL:3 * D_MODEL] * _dot(o, w_ref[...])


def _out_proj_kernel(o_ref, w_ref, x_ref, mod_ref, *rest, final_norm):
    xn = _gated_residual(o_ref[0], w_ref, x_ref, mod_ref)
    if final_norm:
        fw_ref, y_ref = rest
        ms = jnp.mean(xn * xn, axis=-1, keepdims=True)
        xn = xn * lax.rsqrt(ms + NORM_EPS) * fw_ref[...]
    else:
        (y_ref,) = rest
    y_ref[0] = xn


def _out_proj(o, w_bf16, x, mods, layer, row_of, final_w=None):
    B, T, D = x.shape
    tm = min(ROW_TILE, T)
    tile = pl.BlockSpec((1, tm, D), lambda b, t: (b, t, 0))
    in_specs = [tile, _const_spec((D, D)), tile, _mod_spec(layer, row_of)]
    args = [o, w_bf16, x, mods]
    if final_w is not None:
        in_specs.append(_const_spec((1, D)))
        args.append(final_w.reshape(1, D))
    return pl.pallas_call(
        functools.partial(_out_proj_kernel, final_norm=final_w is not None),
        grid=(B, T // tm),
        in_specs=in_specs,
        out_specs=tile,
        out_shape=jax.ShapeDtypeStruct((B, T, D), F32),
        compiler_params=_cparams(("parallel", "parallel")),
        name="out_proj",
    )(*args)


def _rwkv_in_kernel(x_ref, xp_ref, xn_ref, nw_ref, mod_ref, mu_ref, w_ref, l1_ref, w2_ref, a2_ref,
                    w0_ref, a0_ref, kk_ref, ka_ref, rk_ref, seg_ref,
                    r_out, kk_out, v_out, sg_out, bonus_out, lw_out, a_out, kz_out):
    t = pl.program_id(1)
    nt = pl.num_programs(1)
    nw = nw_ref[...]
    mod = mod_ref[...]
    h = _normmod(x_ref[0], nw, mod)
    tm = h.shape[0]
    hp = _normmod(xp_ref[0], nw, mod)[SUBLANES - 1:SUBLANES, :]
    hn = _normmod(xn_ref[0], nw, mod)[0:1, :]
    hp = jnp.where(t > 0, hp, 0.0)
    hn = jnp.where(t < nt - 1, hn, 0.0)
    row = lax.broadcasted_iota(jnp.int32, h.shape, 0)
    h_prev = jnp.where(row == 0, hp, pltpu.roll(h, 1, axis=0))
    h_next = jnp.where(row == tm - 1, hn, pltpu.roll(h, tm - 1, axis=0))
    dx = 0.5 * (h_prev + h_next) - h

    def mix(n):
        return (h + dx * mu_ref[n:n + 1, :]).astype(BF16)

    E = D_INNER
    r = _mm(mix(0), w_ref[:, 0:E])
    k = _mm(mix(1), w_ref[:, E:2 * E])
    v = _mm(mix(2), w_ref[:, 2 * E:3 * E])
    g = _mm(mix(3), w_ref[:, 3 * E:4 * E])
    lw1 = jnp.tanh(_mm(mix(4), l1_ref[:, 0:2 * LORA]))
    la1 = _mm(mix(5), l1_ref[:, 2 * LORA:4 * LORA])
    lane = lax.broadcasted_iota(jnp.int32, lw1.shape, 1)

    kk = k * kk_ref[...]
    kk = kk * lax.rsqrt(jnp.maximum(_headsum(kk * kk, seg_ref), 1e-24))
    r_out[0] = r
    kk_out[0] = kk
    v_out[0] = v
    sg_out[0] = _silu(g).astype(BF16)

    ksum = None
    for z in range(2):
        sel = (lane >= z * LORA) & (lane < (z + 1) * LORA)
        lw = _dot(jnp.where(sel, lw1, 0.0), w2_ref[...])
        la = _dot(jnp.where(sel, la1, 0.0), a2_ref[...])
        lw_out[z, 0] = -math.exp(-0.5) * jax.nn.sigmoid(w0_ref[z:z + 1, :] + lw)
        a = jax.nn.sigmoid(a0_ref[z:z + 1, :] + la)
        a_out[z, 0] = a
        kz = k * (1.0 + (a - 1.0) * ka_ref[...])
        kz_out[z, 0] = kz
        ksum = kz if ksum is None else ksum + kz
    bonus_out[0] = (_headsum(r * rk_ref[...] * ksum, seg_ref) * v).astype(BF16)


def _rwkv_in(x, nw, mods, layer, row_of, p):
    B, T, D = x.shape
    E = D_INNER
    tm = min(RWKV_ROW_TILE, T)
    g8 = tm // SUBLANES
    last8 = T // SUBLANES - 1
    tile = pl.BlockSpec((1, tm, D), lambda b, t: (b, t, 0))
    prev8 = pl.BlockSpec((1, SUBLANES, D), lambda b, t: (b, jnp.maximum(t * g8 - 1, 0), 0))
    next8 = pl.BlockSpec((1, SUBLANES, D), lambda b, t: (b, jnp.minimum((t + 1) * g8, last8), 0))
    otile = pl.BlockSpec((1, tm, E), lambda b, t: (b, t, 0))
    ztile = pl.BlockSpec((2, 1, tm, E), lambda b, t: (0, b, t, 0))
    one = jax.ShapeDtypeStruct((B, T, E), F32)
    two = jax.ShapeDtypeStruct((2, B, T, E), F32)
    return pl.pallas_call(
        _rwkv_in_kernel,
        grid=(B, T // tm),
        in_specs=[tile, prev8, next8, _const_spec((1, D)), _mod_spec(layer, row_of),
                  _const_spec((6, D)), _const_spec((D, 4 * E)), _const_spec((D, 4 * LORA)),
                  _const_spec((2 * LORA, E)), _const_spec((2 * LORA, E)),
                  _const_spec((2, E)), _const_spec((2, E)),
                  _const_spec((1, E)), _const_spec((1, E)), _const_spec((1, E)),
                  _const_spec((MXU_DIM, MXU_DIM))],
        out_specs=[otile] * 5 + [ztile] * 3,
        out_shape=[one] * 3 + [jax.ShapeDtypeStruct((B, T, E), BF16)] * 2 + [two] * 3,
        compiler_params=_cparams(("parallel", "parallel")),
        name="rwkv_in",
    )(x, x, x, nw.reshape(1, D), mods, p["mu"], p["w_in"], p["l1"], p["w2"], p["a2"],
      p["w0"], p["a0"], p["k_k"], p["k_a"], p["r_k"], p["seg"])


def _scan_kernel(*refs, has_init):
    n_in = 6
    ins = [refs[z * n_in:(z + 1) * n_in] for z in range(2)]
    rest = refs[2 * n_in:]
    if has_init:
        s0_refs, rest = rest[0:2], rest[2:]
    y_refs, sf_refs, s_scr = rest[0:2], rest[2:4], rest[4]
    C = SCAN_CHUNK
    PW = 2 * C
    HD = RWKV_HEAD_DIM
    NP = y_refs[0].shape[-1] // LANES
    chains = [(z, p) for z in range(2) for p in range(NP)]
    idx = range(len(chains))
    zs = [z for z, _ in chains]

    @pl.when(pl.program_id(1) == 0)
    def _():
        for z in range(2):
            s_scr[z] = s0_refs[z][...] if has_init else jnp.zeros_like(s_scr[z])

    ri = lax.broadcasted_iota(jnp.int32, (PW, PW), 0)
    ci = lax.broadcasted_iota(jnp.int32, (PW, PW), 1)
    same_blk = (ri // C) == (ci // C)
    tr, tc = ri % C, ci % C
    same_head = (ri // HD) == (ci // HD)
    diag = ri == ci
    eye = jnp.where(diag, 1.0, 0.0)
    li = lax.broadcasted_iota(jnp.int32, (C, C), 0)
    lj = lax.broadcasted_iota(jnp.int32, (C, C), 1)
    strict, incl, cum_mat = [], [], []
    for z in range(2):
        before = (tc > tr) if z == 1 else (tc < tr)
        strict.append(same_blk & before)
        incl.append(same_blk & (before | (tc == tr)))
        cum_mat.append(jnp.where((lj >= li) if z == 1 else (lj <= li), 1.0, 0.0).astype(BF16))
    lo = lax.broadcasted_iota(jnp.int32, (C, LANES), 1) < HD
    zeros_c = jnp.zeros((C, LANES), F32)
    zeros_p = jnp.zeros((PW, LANES), F32)

    def stack2(x):
        return jnp.concatenate([x, x], axis=0)

    def unstack(x2):
        return jnp.where(lo, x2[0:C], x2[C:PW])

    def load(z, which, p):
        ref = ins[z][which]
        sl = slice(p * LANES, (p + 1) * LANES)
        return ref[0, :, sl] if len(ref.shape) == 3 else ref[0, 0, :, sl]

    lw = [load(z, 3, p) for z, p in chains]
    cum = []
    for i in idx:
        both = _mm(cum_mat[zs[i]], jnp.concatenate(_split2(lw[i]), axis=1))
        cum.append(both[:, 0:LANES] + both[:, LANES:2 * LANES])
    tot = [(cum[i][0:1] if zs[i] == 1 else cum[i][C - 1:C]) for i in idx]
    abar, rbar, bh, kh, v, res = [], [], [], [], [], []
    for i, (z, p) in enumerate(chains):
        kk = load(z, 1, p)
        b = kk * load(z, 4, p)
        kz = load(z, 5, p)
        e_neg = jnp.exp(-cum[i])
        e_tot = jnp.exp(tot[i] - cum[i])
        abar.append(-kk * jnp.exp(cum[i] - lw[i]))
        rbar.append(load(z, 0, p) * jnp.exp(cum[i]))
        bt, kt = b * e_neg, kz * e_neg
        bh.append(b * e_tot)
        kh.append(kz * e_tot)
        v.append(load(z, 2, p))
        lhs = jnp.concatenate([abar[i], rbar[i]], axis=0)
        rhs = jnp.concatenate([jnp.where(lo, bt, 0.0), jnp.where(lo, 0.0, bt),
                               jnp.where(lo, kt, 0.0), jnp.where(lo, 0.0, kt)], axis=0)
        res.append(_mm(lhs.astype(BF16), rhs.astype(BF16), DN_NT))
    a_ab = [jnp.where(strict[zs[i]], stack2(res[i][0:C, 0:LANES]), 0.0) for i in idx]
    a_ak = [jnp.where(strict[zs[i]], stack2(res[i][0:C, LANES:2 * LANES]), 0.0) for i in idx]
    a_rb = [jnp.where(incl[zs[i]], stack2(res[i][C:PW, 0:LANES]), 0.0) for i in idx]
    a_rk = [jnp.where(incl[zs[i]], stack2(res[i][C:PW, LANES:2 * LANES]), 0.0) for i in idx]
    v2 = [stack2(x) for x in v]
    wv = [_dot(a_ak[i], v2[i]) for i in idx]

    pw = [_dot(x, x) for x in a_ab]
    q = [eye + x for x in a_ab]
    n = 4
    while n < C:
        m = [_dot(pw[i], jnp.concatenate([q[i], pw[i]], axis=1)) for i in idx]
        q = [q[i] + m[i][:, 0:LANES] for i in idx]
        pw = [x[:, LANES:2 * LANES] for x in m]
        n *= 2
    q = [q[i] + _dot(pw[i], q[i]) for i in idx]

    x = [_dot(q[i], jnp.concatenate([stack2(abar[i]), wv[i]], axis=1)) for i in idx]
    ahat = [unstack(t[:, 0:LANES]) for t in x]
    u0 = [unstack(t[:, LANES:2 * LANES]) for t in x]

    ry = [_dot(jnp.concatenate([a_rb[i], a_rk[i]], axis=1),
               jnp.concatenate([jnp.concatenate([stack2(ahat[i]), stack2(u0[i])], axis=1),
                                jnp.concatenate([zeros_p, v2[i]], axis=1)], axis=0)) for i in idx]
    rhat = [rbar[i] + unstack(ry[i][:, 0:LANES]) for i in idx]
    y0 = [unstack(t[:, LANES:2 * LANES]) for t in ry]

    mn = [_dot(jnp.concatenate([bh[i], kh[i]], axis=0).T,
               jnp.concatenate([jnp.concatenate([ahat[i], u0[i]], axis=1),
                                jnp.concatenate([zeros_c, v[i]], axis=1)], axis=0)) for i in idx]
    for i, (z, p) in enumerate(chains):
        m_c = jnp.where(same_head, mn[i][:, 0:LANES], 0.0) + jnp.where(diag, jnp.exp(tot[i]), 0.0)
        n_c = jnp.where(same_head, mn[i][:, LANES:2 * LANES], 0.0)
        s_old = s_scr[z, p]
        y_refs[z][0, :, p * LANES:(p + 1) * LANES] = _dot(rhat[i], s_old) + y0[i]
        s_new = _dot(m_c, s_old) + n_c
        s_scr[z, p] = s_new
        sf_refs[z][p] = s_new


def _rwkv_scan(r, kk, v, lw, a, kz, s0=None):
    B, T, E = r.shape
    NP = E // LANES
    C = SCAN_CHUNK
    nb = T // C
    state = pl.BlockSpec((None, NP, LANES, LANES), lambda b, t: (b, 0, 0, 0))
    in_specs, args, y_specs = [], [], []
    for z in range(2):
        tok = (lambda t: t) if z == 0 else (lambda t: nb - 1 - t)
        one = pl.BlockSpec((1, C, E), lambda b, t, tok=tok: (b, tok(t), 0))
        two = pl.BlockSpec((1, 1, C, E), lambda b, t, tok=tok, z=z: (z, b, tok(t), 0))
        in_specs += [one, one, one, two, two, two]
        args += [r, kk, v, lw, a, kz]
        y_specs.append(one)
    if s0 is not None:
        in_specs += [state, state]
        args += list(s0)
    return pl.pallas_call(
        functools.partial(_scan_kernel, has_init=s0 is not None),
        grid=(B, nb),
        in_specs=in_specs,
        out_specs=y_specs + [state, state],
        out_shape=[jax.ShapeDtypeStruct((B, T, E), F32)] * 2
        + [jax.ShapeDtypeStruct((B, NP, LANES, LANES), F32)] * 2,
        scratch_shapes=[pltpu.VMEM((2, NP, LANES, LANES), F32)],
        compiler_params=_cparams(("parallel", "arbitrary")),
        name="rwkv_scan",
    )(*args)


def _rwkv_out_kernel(y0_ref, y1_ref, bonus_ref, sg_ref, lnw_ref, lnb_ref, seg_ref, w_ref, x_ref, mod_ref, o_ref):
    y = y0_ref[0] + y1_ref[0]
    inv_n = 1.0 / RWKV_HEAD_DIM
    d = y - _headsum(y, seg_ref) * inv_n
    var = _headsum(d * d, seg_ref) * inv_n
    yn = d * lax.rsqrt(var + RWKV_GN_EPS) * lnw_ref[...] + lnb_ref[...] + bonus_ref[0].astype(F32)
    o_ref[0] = _gated_residual(yn * sg_ref[0].astype(F32), w_ref, x_ref, mod_ref)


def _rwkv_out(y0, y1, bonus, sg, p, x, mods, layer, row_of):
    B, T, D = x.shape
    tm = min(ROW_TILE, T)
    tile = pl.BlockSpec((1, tm, D), lambda b, t: (b, t, 0))
    return pl.pallas_call(
        _rwkv_out_kernel,
        grid=(B, T // tm),
        in_specs=[tile, tile, tile, tile, _const_spec((1, D)), _const_spec((1, D)),
                  _const_spec((MXU_DIM, MXU_DIM)), _const_spec((D, D)), tile, _mod_spec(layer, row_of)],
        out_specs=tile,
        out_shape=jax.ShapeDtypeStruct((B, T, D), F32),
        compiler_params=_cparams(("parallel", "parallel")),
        name="rwkv_out",
    )(y0, y1, bonus, sg, p["ln_w"], p["ln_b"], p["seg"], p["w_out"], x, mods)


def _conv_in_kernel(x_ref, nw_ref, mod_ref, w_ref, z_ref, sg_ref):
    hb = _normmod(x_ref[0], nw_ref[...], mod_ref[...]).astype(BF16)
    E = D_INNER
    for n in range(0, E, N_CHUNK):
        a = _mm(hb, w_ref[:, n:n + N_CHUNK])
        b = _mm(hb, w_ref[:, E + n:E + n + N_CHUNK])
        g = _mm(hb, w_ref[:, 2 * E + n:2 * E + n + N_CHUNK])
        z_ref[0, :, n:n + N_CHUNK] = a * jax.nn.sigmoid(b)
        sg_ref[0, :, n:n + N_CHUNK] = _silu(g).astype(BF16)


def _conv_in(x, nw, mods, layer, row_of, w_bf16):
    B, T, D = x.shape
    E = D_INNER
    tm = min(ROW_TILE, T)
    tile = pl.BlockSpec((1, tm, D), lambda b, t: (b, t, 0))
    return pl.pallas_call(
        _conv_in_kernel,
        grid=(B, T // tm),
        in_specs=[tile, _const_spec((1, D)), _mod_spec(layer, row_of), _const_spec((D, 3 * E))],
        out_specs=[tile, tile],
        out_shape=[jax.ShapeDtypeStruct((B, T, E), F32), jax.ShapeDtypeStruct((B, T, E), BF16)],
        compiler_params=_cparams(("parallel", "parallel")),
        name="conv_in",
    )(x, nw.reshape(1, D), mods, w_bf16)


def _conv_out_kernel(z_ref, zp_ref, zn_ref, sg_ref, dw_ref, dwb_ref, lnw_ref, lnb_ref, w_ref, x_ref, mod_ref,
                     o_ref, zs, cbuf):
    t = pl.program_id(1)
    nt = pl.num_programs(1)
    tm = z_ref.shape[1]
    rows = tm + 2 * HALO
    zs[0, 0:HALO, :] = jnp.where(t > 0, zp_ref[0], 0.0)
    zs[0, HALO:HALO + tm, :] = z_ref[0]
    zs[0, HALO + tm:rows, :] = jnp.where(t < nt - 1, zn_ref[0], 0.0)
    for s in range(1, SUBLANES):
        zs[s, 0:rows - SUBLANES, :] = zs[0, s:s + rows - SUBLANES, :]
    off = HALO - CONV_PAD
    for c0 in range(0, D_INNER, LANES):
        for r0 in range(0, tm, CONV_ROWS):
            acc = None
            for k in range(CONV_WIDTH):
                q8, s = divmod(off + k, SUBLANES)
                a0 = r0 + q8 * SUBLANES
                term = dw_ref[k:k + 1, c0:c0 + LANES] * zs[s, a0:a0 + CONV_ROWS, c0:c0 + LANES]
                acc = term if acc is None else acc + term
            cbuf[r0:r0 + CONV_ROWS, c0:c0 + LANES] = acc
    c = cbuf[...] + dwb_ref[...]
    m = jnp.mean(c, axis=-1, keepdims=True)
    d = c - m
    var = jnp.mean(d * d, axis=-1, keepdims=True)
    y = d * lax.rsqrt(var + 1e-5) * lnw_ref[...] + lnb_ref[...]
    o_ref[0] = _gated_residual(_silu(y) * sg_ref[0].astype(F32), w_ref, x_ref, mod_ref)


def _conv_out(z, sg, p, x, mods, layer, row_of):
    B, T, D = x.shape
    tm = min(CONV_ROW_TILE, T)
    gh = tm // HALO
    lasth = T // HALO - 1
    tile = pl.BlockSpec((1, tm, D), lambda b, t: (b, t, 0))
    prev = pl.BlockSpec((1, HALO, D), lambda b, t: (b, jnp.maximum(t * gh - 1, 0), 0))
    nxt = pl.BlockSpec((1, HALO, D), lambda b, t: (b, jnp.minimum((t + 1) * gh, lasth), 0))
    return pl.pallas_call(
        _conv_out_kernel,
        grid=(B, T // tm),
        in_specs=[tile, prev, nxt, tile, _const_spec((CONV_WIDTH, D)), _const_spec((1, D)),
                  _const_spec((1, D)), _const_spec((1, D)), _const_spec((D, D)), tile,
                  _mod_spec(layer, row_of)],
        out_specs=tile,
        out_shape=jax.ShapeDtypeStruct((B, T, D), F32),
        scratch_shapes=[pltpu.VMEM((SUBLANES, tm + 2 * HALO, D), F32), pltpu.VMEM((tm, D), F32)],
        compiler_params=_cparams(("parallel", "parallel")),
        name="conv_out",
    )(z, z, z, sg, p["dw_w"], p["dw_b"], p["ln_w"], p["ln_b"], p["w_out"], x, mods)


def _rope_tables(n_tok):
    t = jnp.arange(n_tok, dtype=jnp.int32)
    pos = jnp.stack([(t // GRID_W).astype(F32), (t % GRID_W).astype(F32)], axis=1)
    lane = jnp.arange(LANES, dtype=jnp.int32)
    d = lane % ATTN_HEAD_DIM
    axis = d // 32
    second_half = (d % 32) // 16
    axis_dim = ATTN_HEAD_DIM // 2
    inv_freq = ROPE_BASE ** (-(2.0 * (d % 16).astype(F32)) / axis_dim)
    ang = pos[:, axis] * inv_freq[None, :]
    sign = jnp.where(second_half == 1, 1.0, -1.0).astype(F32)
    return jnp.cos(ang), jnp.sin(ang) * sign[None, :]


def _pair_blockdiag(s):
    B, H, N, _ = s.shape
    st = jnp.swapaxes(s, -1, -2).reshape(B, H // 2, 2, N, N)
    z = jnp.zeros_like(st[:, :, 0])
    top = jnp.concatenate([st[:, :, 0], z], axis=-1)
    bot = jnp.concatenate([z, st[:, :, 1]], axis=-1)
    return jnp.concatenate([top, bot], axis=-2)


def _pair_unblock(sp):
    B, NP, _, _ = sp.shape
    N = RWKV_HEAD_DIM
    blocks = jnp.stack([sp[:, :, 0:N, 0:N], sp[:, :, N:2 * N, N:2 * N]], axis=2)
    return jnp.swapaxes(blocks, -1, -2).reshape(B, 2 * NP, N, N)


def kernel(x_prompt, x_sample, cache_attn_k, cache_attn_v, state_rwkv, c, c_ctx, norm_w, ada_w, ada_b, attn_w_in, attn_lambda, attn_subln_w, attn_w_out, rwkv_mu, rwkv_w_in, rwkv_w0, rwkv_w1, rwkv_w2, rwkv_a0, rwkv_a1, rwkv_a2, rwkv_k_k, rwkv_k_a, rwkv_r_k, rwkv_ln_w, rwkv_ln_b, rwkv_w_out, conv_w_in, conv_dw_w, conv_dw_b, conv_ln_w, conv_ln_b, conv_w_out, final_norm_w):
    D, E = D_MODEL, D_INNER
    dec_batch = x_sample.shape[0]
    assert dec_batch < COND_ROWS
    cond = jnp.concatenate([c, c_ctx[None, :], jnp.zeros((COND_ROWS - dec_batch - 1, D), F32)], axis=0)
    mods = _ada(cond, ada_w, ada_b)
    ctx_row = dec_batch
    streams = [(x_prompt, lambda b: ctx_row), (x_sample, lambda b: b)]
    cos, sin = _rope_tables(x_sample.shape[1])
    seg = (jnp.arange(MXU_DIM)[:, None] // RWKV_HEAD_DIM == jnp.arange(MXU_DIM)[None, :] // RWKV_HEAD_DIM).astype(BF16)

    xs = [x_prompt, x_sample]

    def rows(a, s):
        B, T, D = a.shape
        if s == 0 and T < ROW_TILE and (B * T) % ROW_TILE == 0:
            return a.reshape(B * T // ROW_TILE, ROW_TILE, D)
        return a
    new_kv, new_s = (None, None), []
    assert (DEPTH - 1) % N_MIXERS == 0
    for i in range(DEPTH):
        kind, j = i % N_MIXERS, i // N_MIXERS
        last = i == DEPTH - 1
        fw = final_norm_w if last else None
        if kind == 0:
            w_in = attn_w_in[j].astype(BF16)
            w_out = attn_w_out[j].astype(BF16)
            for s, (_, row_of) in enumerate(streams):
                if s == 0:
                    qkvg, *new_kv = _attn_in(xs[s], norm_w[i], mods, i, row_of, w_in, cache=new_kv)
                    o = _attn_core(qkvg, attn_lambda[j], attn_subln_w[j], i)
                else:
                    (qkvg,) = _attn_in(xs[s], norm_w[i], mods, i, row_of, w_in, rope=(cos, sin))
                    o = _attn_core(qkvg, attn_lambda[j], attn_subln_w[j], i, ctx=(cache_attn_k, cache_attn_v, j))
                xs[s] = _out_proj(rows(o, s), w_out, rows(xs[s], s), mods, i, row_of, final_w=fw).reshape(xs[s].shape)
        elif kind == 1:
            p = dict(
                mu=rwkv_mu[j],
                w_in=rwkv_w_in[j].astype(BF16),
                l1=jnp.concatenate([rwkv_w1[j, 0], rwkv_w1[j, 1], rwkv_a1[j, 0], rwkv_a1[j, 1]], axis=1).astype(BF16),
                w2=rwkv_w2[j].reshape(2 * LORA, E).astype(BF16),
                a2=rwkv_a2[j].reshape(2 * LORA, E).astype(BF16),
                w0=rwkv_w0[j], a0=rwkv_a0[j],
                k_k=rwkv_k_k[j].reshape(1, E), k_a=rwkv_k_a[j].reshape(1, E), r_k=rwkv_r_k[j].reshape(1, E),
                ln_w=rwkv_ln_w[j].reshape(1, E), ln_b=rwkv_ln_b[j].reshape(1, E),
                w_out=rwkv_w_out[j].astype(BF16), seg=seg)
            for s, (_, row_of) in enumerate(streams):
                r, kk, v, sg, bonus, lw, a, kz = _rwkv_in(xs[s], norm_w[i], mods, i, row_of, p)
                s0 = [_pair_blockdiag(state_rwkv[:, j, z]) for z in range(2)] if s == 1 else None
                y_f, y_b, sf_f, sf_b = _rwkv_scan(r, kk, v, lw, a, kz, s0)
                if s == 0:
                    new_s.append(jnp.stack([_pair_unblock(sf_f), _pair_unblock(sf_b)], axis=1))
                xs[s] = _rwkv_out(rows(y_f, s), rows(y_b, s), rows(bonus, s), rows(sg, s), p, rows(xs[s], s),
                                  mods, i, row_of).reshape(xs[s].shape)
        else:
            p = dict(dw_w=conv_dw_w[j], dw_b=conv_dw_b[j].reshape(1, E), ln_w=conv_ln_w[j].reshape(1, E),
                     ln_b=conv_ln_b[j].reshape(1, E), w_out=conv_w_out[j].astype(BF16))
            w_in = conv_w_in[j].astype(BF16)
            for s, (_, row_of) in enumerate(streams):
                zz, sg = _conv_in(rows(xs[s], s), norm_w[i], mods, i, row_of, w_in)
                xs[s] = _conv_out(zz.reshape(xs[s].shape), sg.reshape(xs[s].shape), p, xs[s], mods, i, row_of)
    return (xs[0], xs[1], new_kv[0], new_kv[1], jnp.stack(new_s, axis=1))
```

```python
import functools
import math

import jax
import jax.numpy as jnp
from jax import lax
from jax.experimental import pallas as pl
from jax.experimental.pallas import tpu as pltpu

F32 = jnp.float32
BF16 = jnp.bfloat16

D_MODEL = 1024
D_INNER = 1024
DEPTH = 4
N_MIXERS = 3
GRID_W = 64
NORM_EPS = 1e-6
ATTN_HEAD_DIM = 64
ATTN_HEADS = 8
ATTN_V_DIM = 128
ROPE_BASE = 10000.0
RWKV_HEAD_DIM = 64
RWKV_HEADS = 16
RWKV_GN_EPS = RWKV_HEAD_DIM * 1e-5
LORA = 64
CONV_WIDTH = 31
CONV_PAD = CONV_WIDTH // 2

LANES = 128
SUBLANES = 8
MXU_DIM = 256
VMEM_LIMIT_BYTES = 56 * 1024 * 1024

ROW_TILE = 512
CONV_ROW_TILE = 256
RWKV_ROW_TILE = 128
ATTN_Q_TILE = 256
ATTN_ROWS_PER_STEP = 4096
ATTN_TASK_GROUP = 2
N_CHUNK = 512
SCAN_CHUNK = 64
HALO = 16
CONV_ROWS = 64
COND_ROWS = 16
DN = (((1,), (0,)), ((), ()))
DN_NT = (((1,), (1,)), ((), ()))


def _cparams(sem):
    return pltpu.CompilerParams(dimension_semantics=sem, vmem_limit_bytes=VMEM_LIMIT_BYTES)


def _mm(a, b, dn=DN):
    return lax.dot_general(a, b, dn, preferred_element_type=F32)


def _dot(a, b):
    return _mm(a.astype(BF16), b.astype(BF16))


def _split2(x):
    hi = x.astype(BF16)
    lo = (x - hi.astype(F32)).astype(BF16)
    return hi, lo


def _dot3(a, b):
    a_hi, a_lo = _split2(a)
    b_hi, b_lo = _split2(b)
    return _mm(a_hi, b_hi) + (_mm(a_lo, b_hi) + _mm(a_hi, b_lo))


def _silu(x):
    return x * jax.nn.sigmoid(x)


def _normmod(x, nw, mod):
    ms = jnp.mean(x * x, axis=-1, keepdims=True)
    y = x * lax.rsqrt(ms + NORM_EPS) * nw
    return y * (1.0 + mod[:, D_MODEL:2 * D_MODEL]) + mod[:, 0:D_MODEL]


def _headsum(x, seg_ref):
    cols = []
    for c in range(0, x.shape[-1], MXU_DIM):
        hi, lo = _split2(x[:, c:c + MXU_DIM])
        cols.append(_mm(hi, seg_ref[...]) + _mm(lo, seg_ref[...]))
    return jnp.concatenate(cols, axis=-1)


def _mod_spec(layer, row_of):
    return pl.BlockSpec((None, None, 1, 3 * D_MODEL), lambda b, t: (layer, row_of(b), 0, 0))


def _const_spec(shape):
    return pl.BlockSpec(shape, lambda b, t: (0,) * len(shape))


def _ada_kernel(cond_ref, w_ref, b_ref, o_ref):
    o_ref[0, :, 0, :] = _dot3(_silu(cond_ref[...]), w_ref[0]) + b_ref[0]


def _ada(cond, ada_w, ada_b):
    depth, d, n3 = ada_w.shape
    tn = 1024
    return pl.pallas_call(
        _ada_kernel,
        grid=(depth, n3 // tn),
        in_specs=[
            pl.BlockSpec((COND_ROWS, d), lambda i, j: (0, 0)),
            pl.BlockSpec((1, d, tn), lambda i, j: (i, 0, j)),
            pl.BlockSpec((1, 1, tn), lambda i, j: (i, 0, j)),
        ],
        out_specs=pl.BlockSpec((1, COND_ROWS, 1, tn), lambda i, j: (i, 0, 0, j)),
        out_shape=jax.ShapeDtypeStruct((depth, COND_ROWS, 1, n3), F32),
        compiler_params=_cparams(("parallel", "parallel")),
        name="ada_mod",
    )(cond, ada_w, ada_b.reshape(depth, 1, n3))


ATTN_Q_SCALE = ATTN_HEAD_DIM ** -0.5 * math.log2(math.e)


def _attn_in_kernel(x_ref, nw_ref, mod_ref, w_ref, *rest, rope, n_cached):
    rest = list(rest)
    if rope:
        cos_ref, sin_ref = rest.pop(0), rest.pop(0)
    if n_cached:
        pk_ref, pv_ref = rest.pop(0), rest.pop(0)
    o_ref = rest.pop(0)
    if n_cached is not None:
        ko_ref, vo_ref = rest
        if n_cached:
            ko_ref[0:n_cached] = pk_ref[...]
            vo_ref[0:n_cached] = pv_ref[...]
    E = D_INNER
    hb = _normmod(x_ref[0], nw_ref[...], mod_ref[...]).astype(BF16)
    for n in range(0, 4 * E, N_CHUNK):
        y = _mm(hb, w_ref[:, n:n + N_CHUNK])
        if n_cached is not None and E <= n < 3 * E:
            cache_ref = ko_ref if n < 2 * E else vo_ref
            for c in range(0, N_CHUNK, LANES):
                cache_ref[n_cached, (n % E + c) // LANES] = y[:, c:c + LANES]
        if n < E:
            y = y * ATTN_Q_SCALE
        if rope and n < 2 * E:
            y = jnp.concatenate([_rope(y[:, c:c + LANES], cos_ref[...], sin_ref[...])
                                 for c in range(0, N_CHUNK, LANES)], axis=1)
        o_ref[0, :, n:n + N_CHUNK] = y.astype(BF16)


def _attn_in(x, nw, mods, layer, row_of, w_bf16, rope=None, cache=None):
    B, T, D = x.shape
    N = w_bf16.shape[1]
    H, DV = ATTN_HEADS, ATTN_V_DIM
    tm = min(ROW_TILE, T)
    in_specs = [pl.BlockSpec((1, tm, D), lambda b, t: (b, t, 0)), _const_spec((1, D)),
                _mod_spec(layer, row_of), _const_spec((D, N))]
    args = [x, nw.reshape(1, D), mods, w_bf16]
    if rope is not None:
        in_specs += [pl.BlockSpec((tm, LANES), lambda b, t: (t, 0))] * 2
        args += list(rope)
    out_specs = [pl.BlockSpec((1, tm, N), lambda b, t: (b, t, 0))]
    out_shape = [jax.ShapeDtypeStruct((B, T, N), BF16)]
    n_cached = None
    if cache is not None:
        n_cached = 0 if cache[0] is None else cache[0].shape[1]
        if n_cached:
            in_specs += [pl.BlockSpec((None, n_cached, H, tm, DV), lambda b, t: (b, 0, 0, t, 0))] * 2
            args += list(cache)
        out_specs += [pl.BlockSpec((None, n_cached + 1, H, tm, DV), lambda b, t: (b, 0, 0, t, 0))] * 2
        out_shape += [jax.ShapeDtypeStruct((B, n_cached + 1, H, T, DV), F32)] * 2
    return pl.pallas_call(
        functools.partial(_attn_in_kernel, rope=rope is not None, n_cached=n_cached),
        grid=(B, T // tm),
        in_specs=in_specs,
        out_specs=out_specs,
        out_shape=out_shape,
        compiler_params=_cparams(("parallel", "parallel")),
        name="attn_in",
    )(*args)


def _rope(x, cos, sin):
    lane = lax.broadcasted_iota(jnp.int32, x.shape, 1)
    first = (lane % 32) < 16
    partner = jnp.where(first, pltpu.roll(x, LANES - 16, axis=1), pltpu.roll(x, 16, axis=1))
    return x * cos + partner * sin


def _attn_core_kernel(*refs, lam_init, has_ctx, tq, group):
    it = iter(refs)
    lam_ref, q_ref, k_ref, v_ref, g_ref, subw_ref = (next(it) for _ in range(6))
    if has_ctx:
        ck_ref, cv_ref = next(it), next(it)
    o_ref = next(it)
    T = q_ref.shape[1]
    n_heads = q_ref.shape[2] // LANES
    n_tiles = T // tq
    k_scr = [next(it) for _ in range(n_heads)]
    vt_scr = [next(it) for _ in range(n_heads)]
    ring = 2 * group
    s_scr = [[next(it), next(it)] for _ in range(ring)]
    e_scr = [[next(it), next(it)] for _ in range(ring)]
    m_scr = [[next(it), next(it)] for _ in range(ring)]

    lv = lam_ref[...]
    lam = (jnp.exp(jnp.sum(lv[0:1] * lv[1:2], axis=-1, keepdims=True))
           - jnp.exp(jnp.sum(lv[2:3] * lv[3:4], axis=-1, keepdims=True)) + lam_init)

    tk = k_scr[0].shape[0]
    DV = ATTN_V_DIM
    for hh in range(n_heads):
        hl = slice(hh * LANES, (hh + 1) * LANES)
        k_scr[hh][0:T, :] = k_ref[0, :, hl]
        vt_scr[hh][0:DV, 0:T] = v_ref[0, :, hl].astype(F32).T.astype(BF16)
        if has_ctx:
            k_scr[hh][T:, :] = ck_ref[hh].astype(BF16)
            vt_scr[hh][0:DV, T:] = cv_ref[hh].T.astype(BF16)
        vt_scr[hh][DV:, :] = jnp.ones((vt_scr[hh].shape[0] - DV, tk), BF16)

    kblocks = [slice(c, c + MXU_DIM) for c in range(0, tk, MXU_DIM)]
    tasks = [(hh, i) for hh in range(n_heads) for i in range(n_tiles)]

    def score_stage(n):
        hh, i = tasks[n]
        q = q_ref[0, i * tq:(i + 1) * tq, hh * LANES:(hh + 1) * LANES]
        lane = lax.broadcasted_iota(jnp.int32, q.shape, 1)
        for mp in range(2):
            sel = (lane < ATTN_HEAD_DIM) if mp == 0 else (lane >= ATTN_HEAD_DIM)
            qm = jnp.where(sel, q, jnp.zeros_like(q))
            mx = None
            for kb in kblocks:
                st = _mm(k_scr[hh][kb, :], qm, DN_NT)
                s_scr[n % ring][mp][kb, :] = st
                part = jnp.max(st.reshape(MXU_DIM // SUBLANES, SUBLANES, tq), axis=0)
                mx = part if mx is None else jnp.maximum(mx, part)
                yield
            m_scr[n % ring][mp][...] = jnp.broadcast_to(jnp.max(mx, axis=0, keepdims=True), mx.shape)

    def attend_stage(n):
        hh, i = tasks[n]
        outs = []
        for mp in range(2):
            m = m_scr[n % ring][mp][0:1, :]
            for kb in kblocks:
                e_scr[n % ring][mp][kb, :] = jnp.exp2(s_scr[n % ring][mp][kb, :] - m).astype(BF16)
                yield
            ov = _mm(vt_scr[hh][...], e_scr[n % ring][mp][...])
            outs.append((ov[0:DV], ov[DV:DV + 1]))
            yield
        (o1, d1), (o2, d2) = outs
        ot = o1 * (1.0 / d1) - o2 * (lam / d2)
        ms = jnp.mean(ot * ot, axis=0, keepdims=True)
        ot = ot * lax.rsqrt(ms + 1e-5) * subw_ref[...] * (1.0 - lam_init)
        rows, hl = slice(i * tq, (i + 1) * tq), slice(hh * LANES, (hh + 1) * LANES)
        o_ref[0, rows, hl] = (ot.T * _silu(g_ref[0, rows, hl].astype(F32))).astype(BF16)

    n_groups = len(tasks) // group
    for t in range(n_groups + 1):
        live = []
        if t < n_groups:
            live += [score_stage(t * group + u) for u in range(group)]
        if t >= 1:
            live += [attend_stage((t - 1) * group + u) for u in range(group)]
        while live:
            for gen in list(live):
                if next(gen, "done") == "done":
                    live.remove(gen)


def _attn_core(qkvg, lam_vecs, subw, layer_idx, ctx=None):
    B, T, _ = qkvg.shape
    H, DV = ATTN_HEADS, ATTN_V_DIM
    HS = min(H, max(1, ATTN_ROWS_PER_STEP // T))
    lam_init = 0.8 - 0.6 * math.exp(-0.3 * layer_idx)
    has_ctx = ctx is not None
    tq = min(ATTN_Q_TILE, T)
    tk = T + (ctx[0].shape[3] if has_ctx else 0)

    def col(off):
        return pl.BlockSpec((1, T, HS * DV), lambda b, h: (b, 0, off + h))

    nb = H // HS
    in_specs = [pl.BlockSpec((4, ATTN_HEAD_DIM), lambda b, h: (0, 0)),
                col(0), col(nb), col(2 * nb), col(3 * nb),
                pl.BlockSpec((DV, tq), lambda b, h: (0, 0))]
    args = [lam_vecs, qkvg, qkvg, qkvg, qkvg, jnp.broadcast_to(subw[:, None], (DV, tq))]
    if has_ctx:
        ck, cv, j = ctx
        P = ck.shape[3]
        cache_spec = pl.BlockSpec((None, None, HS, P, DV), lambda b, h: (b, j, h, 0, 0))
        in_specs += [cache_spec, cache_spec]
        args += [ck, cv]
    group = ATTN_TASK_GROUP if tk // MXU_DIM >= 2 * ATTN_TASK_GROUP else 1
    assert (HS * (T // tq)) % group == 0
    n_buf = 2 * group * 2
    vt_rows = DV + 2 * SUBLANES
    return pl.pallas_call(
        functools.partial(_attn_core_kernel, lam_init=lam_init, has_ctx=has_ctx, tq=tq, group=group),
        grid=(B, nb),
        in_specs=in_specs,
        out_specs=pl.BlockSpec((1, T, HS * DV), lambda b, h: (b, 0, h)),
        out_shape=jax.ShapeDtypeStruct((B, T, D_INNER), BF16),
        scratch_shapes=([pltpu.VMEM((tk, DV), BF16)] * HS + [pltpu.VMEM((vt_rows, tk), BF16)] * HS
                        + [pltpu.VMEM((tk, tq), F32)] * n_buf + [pltpu.VMEM((tk, tq), BF16)] * n_buf
                        + [pltpu.VMEM((SUBLANES, tq), F32)] * n_buf),
        compiler_params=_cparams(("parallel", "parallel")),
        name="attn_core",
    )(*args)


def _gated_residual(o, w_ref, x_ref, mod_ref):
    return x_ref[0] + mod_ref[:, 2 * D_MODEL:3 * D_MODEL] * _dot(o, w_ref[...])


def _out_proj_kernel(o_ref, w_ref, x_ref, mod_ref, *rest, final_norm):
    xn = _gated_residual(o_ref[0], w_ref, x_ref, mod_ref)
    if final_norm:
        fw_ref, y_ref = rest
        ms = jnp.mean(xn * xn, axis=-1, keepdims=True)
        xn = xn * lax.rsqrt(ms + NORM_EPS) * fw_ref[...]
    else:
        (y_ref,) = rest
    y_ref[0] = xn


def _out_proj(o, w_bf16, x, mods, layer, row_of, final_w=None):
    B, T, D = x.shape
    tm = min(ROW_TILE, T)
    tile = pl.BlockSpec((1, tm, D), lambda b, t: (b, t, 0))
    in_specs = [tile, _const_spec((D, D)), tile, _mod_spec(layer, row_of)]
    args = [o, w_bf16, x, mods]
    if final_w is not None:
        in_specs.append(_const_spec((1, D)))
        args.append(final_w.reshape(1, D))
    return pl.pallas_call(
        functools.partial(_out_proj_kernel, final_norm=final_w is not None),
        grid=(B, T // tm),
        in_specs=in_specs,
        out_specs=tile,
        out_shape=jax.ShapeDtypeStruct((B, T, D), F32),
        compiler_params=_cparams(("parallel", "parallel")),
        name="out_proj",
    )(*args)


def _rwkv_in_kernel(x_ref, xp_ref, xn_ref, nw_ref, mod_ref, mu_ref, w_ref, l1_ref, w2_ref, a2_ref,
                    w0_ref, a0_ref, kk_ref, ka_ref, rk_ref, seg_ref,
                    r_out, kk_out, v_out, sg_out, bonus_out, lw_out, a_out, kz_out):
    t = pl.program_id(1)
    nt = pl.num_programs(1)
    nw = nw_ref[...]
    mod = mod_ref[...]
    h = _normmod(x_ref[0], nw, mod)
    tm = h.shape[0]
    hp = _normmod(xp_ref[0], nw, mod)[SUBLANES - 1:SUBLANES, :]
    hn = _normmod(xn_ref[0], nw, mod)[0:1, :]
    hp = jnp.where(t > 0, hp, 0.0)
    hn = jnp.where(t < nt - 1, hn, 0.0)
    row = lax.broadcasted_iota(jnp.int32, h.shape, 0)
    h_prev = jnp.where(row == 0, hp, pltpu.roll(h, 1, axis=0))
    h_next = jnp.where(row == tm - 1, hn, pltpu.roll(h, tm - 1, axis=0))
    dx = 0.5 * (h_prev + h_next) - h

    def mix(n):
        return (h + dx * mu_ref[n:n + 1, :]).astype(BF16)

    E = D_INNER
    r = _mm(mix(0), w_ref[:, 0:E])
    k = _mm(mix(1), w_ref[:, E:2 * E])
    v = _mm(mix(2), w_ref[:, 2 * E:3 * E])
    g = _mm(mix(3), w_ref[:, 3 * E:4 * E])
    lw1 = jnp.tanh(_mm(mix(4), l1_ref[:, 0:2 * LORA]))
    la1 = _mm(mix(5), l1_ref[:, 2 * LORA:4 * LORA])
    lane = lax.broadcasted_iota(jnp.int32, lw1.shape, 1)

    kk = k * kk_ref[...]
    kk = kk * lax.rsqrt(jnp.maximum(_headsum(kk * kk, seg_ref), 1e-24))
    r_out[0] = r
    kk_out[0] = kk
    v_out[0] = v
    sg_out[0] = _silu(g).astype(BF16)

    ksum = None
    for z in range(2):
        sel = (lane >= z * LORA) & (lane < (z + 1) * LORA)
        lw = _dot(jnp.where(sel, lw1, 0.0), w2_ref[...])
        la = _dot(jnp.where(sel, la1, 0.0), a2_ref[...])
        lw_out[z, 0] = -math.exp(-0.5) * jax.nn.sigmoid(w0_ref[z:z + 1, :] + lw)
        a = jax.nn.sigmoid(a0_ref[z:z + 1, :] + la)
        a_out[z, 0] = a
        kz = k * (1.0 + (a - 1.0) * ka_ref[...])
        kz_out[z, 0] = kz
        ksum = kz if ksum is None else ksum + kz
    bonus_out[0] = (_headsum(r * rk_ref[...] * ksum, seg_ref) * v).astype(BF16)


def _rwkv_in(x, nw, mods, layer, row_of, p):
    B, T, D = x.shape
    E = D_INNER
    tm = min(RWKV_ROW_TILE, T)
    g8 = tm // SUBLANES
    last8 = T // SUBLANES - 1
    tile = pl.BlockSpec((1, tm, D), lambda b, t: (b, t, 0))
    prev8 = pl.BlockSpec((1, SUBLANES, D), lambda b, t: (b, jnp.maximum(t * g8 - 1, 0), 0))
    next8 = pl.BlockSpec((1, SUBLANES, D), lambda b, t: (b, jnp.minimum((t + 1) * g8, last8), 0))
    otile = pl.BlockSpec((1, tm, E), lambda b, t: (b, t, 0))
    ztile = pl.BlockSpec((2, 1, tm, E), lambda b, t: (0, b, t, 0))
    one = jax.ShapeDtypeStruct((B, T, E), F32)
    two = jax.ShapeDtypeStruct((2, B, T, E), F32)
    return pl.pallas_call(
        _rwkv_in_kernel,
        grid=(B, T // tm),
        in_specs=[tile, prev8, next8, _const_spec((1, D)), _mod_spec(layer, row_of),
                  _const_spec((6, D)), _const_spec((D, 4 * E)), _const_spec((D, 4 * LORA)),
                  _const_spec((2 * LORA, E)), _const_spec((2 * LORA, E)),
                  _const_spec((2, E)), _const_spec((2, E)),
                  _const_spec((1, E)), _const_spec((1, E)), _const_spec((1, E)),
                  _const_spec((MXU_DIM, MXU_DIM))],
        out_specs=[otile] * 5 + [ztile] * 3,
        out_shape=[one] * 3 + [jax.ShapeDtypeStruct((B, T, E), BF16)] * 2 + [two] * 3,
        compiler_params=_cparams(("parallel", "parallel")),
        name="rwkv_in",
    )(x, x, x, nw.reshape(1, D), mods, p["mu"], p["w_in"], p["l1"], p["w2"], p["a2"],
      p["w0"], p["a0"], p["k_k"], p["k_a"], p["r_k"], p["seg"])


def _scan_kernel(*refs, has_init):
    n_in = 6
    ins = [refs[z * n_in:(z + 1) * n_in] for z in range(2)]
    rest = refs[2 * n_in:]
    if has_init:
        s0_refs, rest = rest[0:2], rest[2:]
    y_refs, sf_refs, s_scr = rest[0:2], rest[2:4], rest[4]
    C = SCAN_CHUNK
    PW = 2 * C
    HD = RWKV_HEAD_DIM
    NP = y_refs[0].shape[-1] // LANES
    chains = [(z, p) for z in range(2) for p in range(NP)]
    idx = range(len(chains))
    zs = [z for z, _ in chains]

    @pl.when(pl.program_id(1) == 0)
    def _():
        for z in range(2):
            s_scr[z] = s0_refs[z][...] if has_init else jnp.zeros_like(s_scr[z])

    ri = lax.broadcasted_iota(jnp.int32, (PW, PW), 0)
    ci = lax.broadcasted_iota(jnp.int32, (PW, PW), 1)
    same_blk = (ri // C) == (ci // C)
    tr, tc = ri % C, ci % C
    same_head = (ri // HD) == (ci // HD)
    diag = ri == ci
    eye = jnp.where(diag, 1.0, 0.0)
    li = lax.broadcasted_iota(jnp.int32, (C, C), 0)
    lj = lax.broadcasted_iota(jnp.int32, (C, C), 1)
    strict, incl, cum_mat = [], [], []
    for z in range(2):
        before = (tc > tr) if z == 1 else (tc < tr)
        strict.append(same_blk & before)
        incl.append(same_blk & (before | (tc == tr)))
        cum_mat.append(jnp.where((lj >= li) if z == 1 else (lj <= li), 1.0, 0.0).astype(BF16))
    lo = lax.broadcasted_iota(jnp.int32, (C, LANES), 1) < HD
    zeros_c = jnp.zeros((C, LANES), F32)
    zeros_p = jnp.zeros((PW, LANES), F32)

    def stack2(x):
        return jnp.concatenate([x, x], axis=0)

    def unstack(x2):
        return jnp.where(lo, x2[0:C], x2[C:PW])

    def load(z, which, p):
        ref = ins[z][which]
        sl = slice(p * LANES, (p + 1) * LANES)
        return ref[0, :, sl] if len(ref.shape) == 3 else ref[0, 0, :, sl]

    lw = [load(z, 3, p) for z, p in chains]
    cum = []
    for i in idx:
        hi, lo16 = _split2(lw[i])
        cum.append(_mm(cum_mat[zs[i]], hi) + _mm(cum_mat[zs[i]], lo16))
    tot = [(cum[i][0:1] if zs[i] == 1 else cum[i][C - 1:C]) for i in idx]
    abar, rbar, bh, kh, v, res = [], [], [], [], [], []
    for i, (z, p) in enumerate(chains):
        kk = load(z, 1, p)
        b = kk * load(z, 4, p)
        kz = load(z, 5, p)
        e_neg = jnp.exp(-cum[i])
        e_tot = jnp.exp(tot[i] - cum[i])
        abar.append(-kk * jnp.exp(cum[i] - lw[i]))
        rbar.append(load(z, 0, p) * jnp.exp(cum[i]))
        bt, kt = b * e_neg, kz * e_neg
        bh.append(b * e_tot)
        kh.append(kz * e_tot)
        v.append(load(z, 2, p))
        lhs = jnp.concatenate([abar[i], rbar[i]], axis=0)
        rhs = jnp.concatenate([jnp.where(lo, bt, 0.0), jnp.where(lo, 0.0, bt),
                               jnp.where(lo, kt, 0.0), jnp.where(lo, 0.0, kt)], axis=0)
        res.append(_mm(lhs.astype(BF16), rhs.astype(BF16), DN_NT))
    a_ab = [jnp.where(strict[zs[i]], stack2(res[i][0:C, 0:LANES]), 0.0) for i in idx]
    a_ak = [jnp.where(strict[zs[i]], stack2(res[i][0:C, LANES:2 * LANES]), 0.0) for i in idx]
    a_rb = [jnp.where(incl[zs[i]], stack2(res[i][C:PW, 0:LANES]), 0.0) for i in idx]
    a_rk = [jnp.where(incl[zs[i]], stack2(res[i][C:PW, LANES:2 * LANES]), 0.0) for i in idx]
    v2 = [stack2(x) for x in v]
    wv = [_dot(a_ak[i], v2[i]) for i in idx]

    pw = [_dot(x, x) for x in a_ab]
    q = [eye + x for x in a_ab]
    n = 4
    while n < C:
        m = [_dot(pw[i], jnp.concatenate([q[i], pw[i]], axis=1)) for i in idx]
        q = [q[i] + m[i][:, 0:LANES] for i in idx]
        pw = [x[:, LANES:2 * LANES] for x in m]
        n *= 2
    q = [q[i] + _dot(pw[i], q[i]) for i in idx]

    x = [_dot(q[i], jnp.concatenate([stack2(abar[i]), wv[i]], axis=1)) for i in idx]
    ahat = [unstack(t[:, 0:LANES]) for t in x]
    u0 = [unstack(t[:, LANES:2 * LANES]) for t in x]

    ry = [_dot(jnp.concatenate([a_rb[i], a_rk[i]], axis=1),
               jnp.concatenate([jnp.concatenate([stack2(ahat[i]), stack2(u0[i])], axis=1),
                                jnp.concatenate([zeros_p, v2[i]], axis=1)], axis=0)) for i in idx]
    rhat = [rbar[i] + unstack(ry[i][:, 0:LANES]) for i in idx]
    y0 = [unstack(t[:, LANES:2 * LANES]) for t in ry]

    mn = [_dot(jnp.concatenate([bh[i], kh[i]], axis=0).T,
               jnp.concatenate([jnp.concatenate([ahat[i], u0[i]], axis=1),
                                jnp.concatenate([zeros_c, v[i]], axis=1)], axis=0)) for i in idx]
    for i, (z, p) in enumerate(chains):
        m_c = jnp.where(same_head, mn[i][:, 0:LANES], 0.0) + jnp.where(diag, jnp.exp(tot[i]), 0.0)
        n_c = jnp.where(same_head, mn[i][:, LANES:2 * LANES], 0.0)
        s_old = s_scr[z, p]
        y_refs[z][0, :, p * LANES:(p + 1) * LANES] = _dot(rhat[i], s_old) + y0[i]
        s_new = _dot(m_c, s_old) + n_c
        s_scr[z, p] = s_new
        sf_refs[z][p] = s_new


def _rwkv_scan(r, kk, v, lw, a, kz, s0=None):
    B, T, E = r.shape
    NP = E // LANES
    C = SCAN_CHUNK
    nb = T // C
    state = pl.BlockSpec((None, NP, LANES, LANES), lambda b, t: (b, 0, 0, 0))
    in_specs, args, y_specs = [], [], []
    for z in range(2):
        tok = (lambda t: t) if z == 0 else (lambda t: nb - 1 - t)
        one = pl.BlockSpec((1, C, E), lambda b, t, tok=tok: (b, tok(t), 0))
        two = pl.BlockSpec((1, 1, C, E), lambda b, t, tok=tok, z=z: (z, b, tok(t), 0))
        in_specs += [one, one, one, two, two, two]
        args += [r, kk, v, lw, a, kz]
        y_specs.append(one)
    if s0 is not None:
        in_specs += [state, state]
        args += list(s0)
    return pl.pallas_call(
        functools.partial(_scan_kernel, has_init=s0 is not None),
        grid=(B, nb),
        in_specs=in_specs,
        out_specs=y_specs + [state, state],
        out_shape=[jax.ShapeDtypeStruct((B, T, E), F32)] * 2
        + [jax.ShapeDtypeStruct((B, NP, LANES, LANES), F32)] * 2,
        scratch_shapes=[pltpu.VMEM((2, NP, LANES, LANES), F32)],
        compiler_params=_cparams(("parallel", "arbitrary")),
        name="rwkv_scan",
    )(*args)


def _rwkv_out_kernel(y0_ref, y1_ref, bonus_ref, sg_ref, lnw_ref, lnb_ref, seg_ref, w_ref, x_ref, mod_ref, o_ref):
    y = y0_ref[0] + y1_ref[0]
    inv_n = 1.0 / RWKV_HEAD_DIM
    d = y - _headsum(y, seg_ref) * inv_n
    var = _headsum(d * d, seg_ref) * inv_n
    yn = d * lax.rsqrt(var + RWKV_GN_EPS) * lnw_ref[...] + lnb_ref[...] + bonus_ref[0].astype(F32)
    o_ref[0] = _gated_residual(yn * sg_ref[0].astype(F32), w_ref, x_ref, mod_ref)


def _rwkv_out(y0, y1, bonus, sg, p, x, mods, layer, row_of):
    B, T, D = x.shape
    tm = min(ROW_TILE, T)
    tile = pl.BlockSpec((1, tm, D), lambda b, t: (b, t, 0))
    return pl.pallas_call(
        _rwkv_out_kernel,
        grid=(B, T // tm),
        in_specs=[tile, tile, tile, tile, _const_spec((1, D)), _const_spec((1, D)),
                  _const_spec((MXU_DIM, MXU_DIM)), _const_spec((D, D)), tile, _mod_spec(layer, row_of)],
        out_specs=tile,
        out_shape=jax.ShapeDtypeStruct((B, T, D), F32),
        compiler_params=_cparams(("parallel", "parallel")),
        name="rwkv_out",
    )(y0, y1, bonus, sg, p["ln_w"], p["ln_b"], p["seg"], p["w_out"], x, mods)


def _conv_in_kernel(x_ref, nw_ref, mod_ref, w_ref, z_ref, sg_ref):
    hb = _normmod(x_ref[0], nw_ref[...], mod_ref[...]).astype(BF16)
    E = D_INNER
    for n in range(0, E, N_CHUNK):
        a = _mm(hb, w_ref[:, n:n + N_CHUNK])
        b = _mm(hb, w_ref[:, E + n:E + n + N_CHUNK])
        g = _mm(hb, w_ref[:, 2 * E + n:2 * E + n + N_CHUNK])
        z_ref[0, :, n:n + N_CHUNK] = a * jax.nn.sigmoid(b)
        sg_ref[0, :, n:n + N_CHUNK] = _silu(g).astype(BF16)


def _conv_in(x, nw, mods, layer, row_of, w_bf16):
    B, T, D = x.shape
    E = D_INNER
    tm = min(ROW_TILE, T)
    tile = pl.BlockSpec((1, tm, D), lambda b, t: (b, t, 0))
    return pl.pallas_call(
        _conv_in_kernel,
        grid=(B, T // tm),
        in_specs=[tile, _const_spec((1, D)), _mod_spec(layer, row_of), _const_spec((D, 3 * E))],
        out_specs=[tile, tile],
        out_shape=[jax.ShapeDtypeStruct((B, T, E), F32), jax.ShapeDtypeStruct((B, T, E), BF16)],
        compiler_params=_cparams(("parallel", "parallel")),
        name="conv_in",
    )(x, nw.reshape(1, D), mods, w_bf16)


def _conv_out_kernel(z_ref, zp_ref, zn_ref, sg_ref, dw_ref, dwb_ref, lnw_ref, lnb_ref, w_ref, x_ref, mod_ref,
                     o_ref, zs, cbuf):
    t = pl.program_id(1)
    nt = pl.num_programs(1)
    tm = z_ref.shape[1]
    rows = tm + 2 * HALO
    zs[0, 0:HALO, :] = jnp.where(t > 0, zp_ref[0], 0.0)
    zs[0, HALO:HALO + tm, :] = z_ref[0]
    zs[0, HALO + tm:rows, :] = jnp.where(t < nt - 1, zn_ref[0], 0.0)
    for s in range(1, SUBLANES):
        zs[s, 0:rows - SUBLANES, :] = zs[0, s:s + rows - SUBLANES, :]
    off = HALO - CONV_PAD
    for c0 in range(0, D_INNER, LANES):
        for r0 in range(0, tm, CONV_ROWS):
            acc = None
            for k in range(CONV_WIDTH):
                q8, s = divmod(off + k, SUBLANES)
                a0 = r0 + q8 * SUBLANES
                term = dw_ref[k:k + 1, c0:c0 + LANES] * zs[s, a0:a0 + CONV_ROWS, c0:c0 + LANES]
                acc = term if acc is None else acc + term
            cbuf[r0:r0 + CONV_ROWS, c0:c0 + LANES] = acc
    c = cbuf[...] + dwb_ref[...]
    m = jnp.mean(c, axis=-1, keepdims=True)
    d = c - m
    var = jnp.mean(d * d, axis=-1, keepdims=True)
    y = d * lax.rsqrt(var + 1e-5) * lnw_ref[...] + lnb_ref[...]
    o_ref[0] = _gated_residual(_silu(y) * sg_ref[0].astype(F32), w_ref, x_ref, mod_ref)


def _conv_out(z, sg, p, x, mods, layer, row_of):
    B, T, D = x.shape
    tm = min(CONV_ROW_TILE, T)
    gh = tm // HALO
    lasth = T // HALO - 1
    tile = pl.BlockSpec((1, tm, D), lambda b, t: (b, t, 0))
    prev = pl.BlockSpec((1, HALO, D), lambda b, t: (b, jnp.maximum(t * gh - 1, 0), 0))
    nxt = pl.BlockSpec((1, HALO, D), lambda b, t: (b, jnp.minimum((t + 1) * gh, lasth), 0))
    return pl.pallas_call(
        _conv_out_kernel,
        grid=(B, T // tm),
        in_specs=[tile, prev, nxt, tile, _const_spec((CONV_WIDTH, D)), _const_spec((1, D)),
                  _const_spec((1, D)), _const_spec((1, D)), _const_spec((D, D)), tile,
                  _mod_spec(layer, row_of)],
        out_specs=tile,
        out_shape=jax.ShapeDtypeStruct((B, T, D), F32),
        scratch_shapes=[pltpu.VMEM((SUBLANES, tm + 2 * HALO, D), F32), pltpu.VMEM((tm, D), F32)],
        compiler_params=_cparams(("parallel", "parallel")),
        name="conv_out",
    )(z, z, z, sg, p["dw_w"], p["dw_b"], p["ln_w"], p["ln_b"], p["w_out"], x, mods)


def _rope_tables(n_tok):
    t = jnp.arange(n_tok, dtype=jnp.int32)
    pos = jnp.stack([(t // GRID_W).astype(F32), (t % GRID_W).astype(F32)], axis=1)
    lane = jnp.arange(LANES, dtype=jnp.int32)
    d = lane % ATTN_HEAD_DIM
    axis = d // 32
    second_half = (d % 32) // 16
    axis_dim = ATTN_HEAD_DIM // 2
    inv_freq = ROPE_BASE ** (-(2.0 * (d % 16).astype(F32)) / axis_dim)
    ang = pos[:, axis] * inv_freq[None, :]
    sign = jnp.where(second_half == 1, 1.0, -1.0).astype(F32)
    return jnp.cos(ang), jnp.sin(ang) * sign[None, :]


def _pair_blockdiag(s):
    B, H, N, _ = s.shape
    st = jnp.swapaxes(s, -1, -2).reshape(B, H // 2, 2, N, N)
    z = jnp.zeros_like(st[:, :, 0])
    top = jnp.concatenate([st[:, :, 0], z], axis=-1)
    bot = jnp.concatenate([z, st[:, :, 1]], axis=-1)
    return jnp.concatenate([top, bot], axis=-2)


def _pair_unblock(sp):
    B, NP, _, _ = sp.shape
    N = RWKV_HEAD_DIM
    blocks = jnp.stack([sp[:, :, 0:N, 0:N], sp[:, :, N:2 * N, N:2 * N]], axis=2)
    return jnp.swapaxes(blocks, -1, -2).reshape(B, 2 * NP, N, N)


def kernel(x_prompt, x_sample, cache_attn_k, cache_attn_v, state_rwkv, c, c_ctx, norm_w, ada_w, ada_b, attn_w_in, attn_lambda, attn_subln_w, attn_w_out, rwkv_mu, rwkv_w_in, rwkv_w0, rwkv_w1, rwkv_w2, rwkv_a0, rwkv_a1, rwkv_a2, rwkv_k_k, rwkv_k_a, rwkv_r_k, rwkv_ln_w, rwkv_ln_b, rwkv_w_out, conv_w_in, conv_dw_w, conv_dw_b, conv_ln_w, conv_ln_b, conv_w_out, final_norm_w):
    D, E = D_MODEL, D_INNER
    dec_batch = x_sample.shape[0]
    assert dec_batch < COND_ROWS
    cond = jnp.concatenate([c, c_ctx[None, :], jnp.zeros((COND_ROWS - dec_batch - 1, D), F32)], axis=0)
    mods = _ada(cond, ada_w, ada_b)
    ctx_row = dec_batch
    streams = [(x_prompt, lambda b: ctx_row), (x_sample, lambda b: b)]
    cos, sin = _rope_tables(x_sample.shape[1])
    seg = (jnp.arange(MXU_DIM)[:, None] // RWKV_HEAD_DIM == jnp.arange(MXU_DIM)[None, :] // RWKV_HEAD_DIM).astype(BF16)

    xs = [x_prompt, x_sample]

    def rows(a, s):
        B, T, D = a.shape
        if s == 0 and T < ROW_TILE and (B * T) % ROW_TILE == 0:
            return a.reshape(B * T // ROW_TILE, ROW_TILE, D)
        return a
    new_kv, new_s = (None, None), []
    assert (DEPTH - 1) % N_MIXERS == 0
    for i in range(DEPTH):
        kind, j = i % N_MIXERS, i // N_MIXERS
        last = i == DEPTH - 1
        fw = final_norm_w if last else None
        if kind == 0:
            w_in = attn_w_in[j].astype(BF16)
            w_out = attn_w_out[j].astype(BF16)
            for s, (_, row_of) in enumerate(streams):
                if s == 0:
                    qkvg, *new_kv = _attn_in(xs[s], norm_w[i], mods, i, row_of, w_in, cache=new_kv)
                    o = _attn_core(qkvg, attn_lambda[j], attn_subln_w[j], i)
                else:
                    (qkvg,) = _attn_in(xs[s], norm_w[i], mods, i, row_of, w_in, rope=(cos, sin))
                    o = _attn_core(qkvg, attn_lambda[j], attn_subln_w[j], i, ctx=(cache_attn_k, cache_attn_v, j))
                xs[s] = _out_proj(rows(o, s), w_out, rows(xs[s], s), mods, i, row_of, final_w=fw).reshape(xs[s].shape)
        elif kind == 1:
            p = dict(
                mu=rwkv_mu[j],
                w_in=rwkv_w_in[j].astype(BF16),
                l1=jnp.concatenate([rwkv_w1[j, 0], rwkv_w1[j, 1], rwkv_a1[j, 0], rwkv_a1[j, 1]], axis=1).astype(BF16),
                w2=rwkv_w2[j].reshape(2 * LORA, E).astype(BF16),
                a2=rwkv_a2[j].reshape(2 * LORA, E).astype(BF16),
                w0=rwkv_w0[j], a0=rwkv_a0[j],
                k_k=rwkv_k_k[j].reshape(1, E), k_a=rwkv_k_a[j].reshape(1, E), r_k=rwkv_r_k[j].reshape(1, E),
                ln_w=rwkv_ln_w[j].reshape(1, E), ln_b=rwkv_ln_b[j].reshape(1, E),
                w_out=rwkv_w_out[j].astype(BF16), seg=seg)
            for s, (_, row_of) in enumerate(streams):
                r, kk, v, sg, bonus, lw, a, kz = _rwkv_in(xs[s], norm_w[i], mods, i, row_of, p)
                s0 = [_pair_blockdiag(state_rwkv[:, j, z]) for z in range(2)] if s == 1 else None
                y_f, y_b, sf_f, sf_b = _rwkv_scan(r, kk, v, lw, a, kz, s0)
                if s == 0:
                    new_s.append(jnp.stack([_pair_unblock(sf_f), _pair_unblock(sf_b)], axis=1))
                xs[s] = _rwkv_out(rows(y_f, s), rows(y_b, s), rows(bonus, s), rows(sg, s), p, rows(xs[s], s),
                                  mods, i, row_of).reshape(xs[s].shape)
        else:
            p = dict(dw_w=conv_dw_w[j], dw_b=conv_dw_b[j].reshape(1, E), ln_w=conv_ln_w[j].reshape(1, E),
                     ln_b=conv_ln_b[j].reshape(1, E), w_out=conv_w_out[j].astype(BF16))
            w_in = conv_w_in[j].astype(BF16)
            for s, (_, row_of) in enumerate(streams):
                zz, sg = _conv_in(rows(xs[s], s), norm_w[i], mods, i, row_of, w_in)
                xs[s] = _conv_out(zz.reshape(xs[s].shape), sg.reshape(xs[s].shape), p, xs[s], mods, i, row_of)
    return (xs[0], xs[1], new_kv[0], new_kv[1], jnp.stack(new_s, axis=1))
```

```python
import functools
import math

import jax
import jax.numpy as jnp
from jax import lax
from jax.experimental import pallas as pl
from jax.experimental.pallas import tpu as pltpu

F32 = jnp.float32
BF16 = jnp.bfloat16

D_MODEL = 1024
D_INNER = 1024
DEPTH = 4
N_MIXERS = 3
GRID_W = 64
NORM_EPS = 1e-6
ATTN_HEAD_DIM = 64
ATTN_HEADS = 8
ATTN_V_DIM = 128
ROPE_BASE = 10000.0
RWKV_HEAD_DIM = 64
RWKV_HEADS = 16
RWKV_GN_EPS = RWKV_HEAD_DIM * 1e-5
LORA = 64
CONV_WIDTH = 31
CONV_PAD = CONV_WIDTH // 2

LANES = 128
SUBLANES = 8
MXU_DIM = 256
VMEM_LIMIT_BYTES = 56 * 1024 * 1024

ROW_TILE = 512
CONV_ROW_TILE = 256
RWKV_ROW_TILE = 256
ATTN_Q_TILE = 256
ATTN_ROWS_PER_STEP = 4096
ATTN_TASK_GROUP = 2
N_CHUNK = 512
SCAN_CHUNK = 64
HALO = 16
CONV_ROWS = 64
COND_ROWS = 16
DN = (((1,), (0,)), ((), ()))
DN_NT = (((1,), (1,)), ((), ()))


def _cparams(sem):
    return pltpu.CompilerParams(dimension_semantics=sem, vmem_limit_bytes=VMEM_LIMIT_BYTES)


def _mm(a, b, dn=DN):
    return lax.dot_general(a, b, dn, preferred_element_type=F32)


def _dot(a, b):
    return _mm(a.astype(BF16), b.astype(BF16))


def _split2(x):
    hi = x.astype(BF16)
    lo = (x - hi.astype(F32)).astype(BF16)
    return hi, lo


def _dot3(a, b):
    a_hi, a_lo = _split2(a)
    b_hi, b_lo = _split2(b)
    return _mm(a_hi, b_hi) + (_mm(a_lo, b_hi) + _mm(a_hi, b_lo))


def _silu(x):
    return x * jax.nn.sigmoid(x)


def _normmod(x, nw, mod):
    ms = jnp.mean(x * x, axis=-1, keepdims=True)
    y = x * lax.rsqrt(ms + NORM_EPS) * nw
    return y * (1.0 + mod[:, D_MODEL:2 * D_MODEL]) + mod[:, 0:D_MODEL]


def _headsum(x, seg_ref):
    cols = []
    for c in range(0, x.shape[-1], MXU_DIM):
        hi, lo = _split2(x[:, c:c + MXU_DIM])
        cols.append(_mm(hi, seg_ref[...]) + _mm(lo, seg_ref[...]))
    return jnp.concatenate(cols, axis=-1)


def _mod_spec(layer, row_of):
    return pl.BlockSpec((None, None, 1, 3 * D_MODEL), lambda b, t: (layer, row_of(b), 0, 0))


def _const_spec(shape):
    return pl.BlockSpec(shape, lambda b, t: (0,) * len(shape))


def _ada_kernel(cond_ref, w_ref, b_ref, o_ref):
    o_ref[0, :, 0, :] = _dot3(_silu(cond_ref[...]), w_ref[0]) + b_ref[0]


def _ada(cond, ada_w, ada_b):
    depth, d, n3 = ada_w.shape
    tn = 1024
    return pl.pallas_call(
        _ada_kernel,
        grid=(depth, n3 // tn),
        in_specs=[
            pl.BlockSpec((COND_ROWS, d), lambda i, j: (0, 0)),
            pl.BlockSpec((1, d, tn), lambda i, j: (i, 0, j)),
            pl.BlockSpec((1, 1, tn), lambda i, j: (i, 0, j)),
        ],
        out_specs=pl.BlockSpec((1, COND_ROWS, 1, tn), lambda i, j: (i, 0, 0, j)),
        out_shape=jax.ShapeDtypeStruct((depth, COND_ROWS, 1, n3), F32),
        compiler_params=_cparams(("parallel", "parallel")),
        name="ada_mod",
    )(cond, ada_w, ada_b.reshape(depth, 1, n3))


ATTN_Q_SCALE = ATTN_HEAD_DIM ** -0.5 * math.log2(math.e)


def _attn_in_kernel(x_ref, nw_ref, mod_ref, w_ref, *rest, rope, n_cached):
    rest = list(rest)
    if rope:
        cos_ref, sin_ref = rest.pop(0), rest.pop(0)
    if n_cached:
        pk_ref, pv_ref = rest.pop(0), rest.pop(0)
    o_ref = rest.pop(0)
    if n_cached is not None:
        ko_ref, vo_ref = rest
        if n_cached:
            ko_ref[0:n_cached] = pk_ref[...]
            vo_ref[0:n_cached] = pv_ref[...]
    E = D_INNER
    hb = _normmod(x_ref[0], nw_ref[...], mod_ref[...]).astype(BF16)
    for n in range(0, 4 * E, N_CHUNK):
        y = _mm(hb, w_ref[:, n:n + N_CHUNK])
        if n_cached is not None and E <= n < 3 * E:
            cache_ref = ko_ref if n < 2 * E else vo_ref
            for c in range(0, N_CHUNK, LANES):
                cache_ref[n_cached, (n % E + c) // LANES] = y[:, c:c + LANES]
        if n < E:
            y = y * ATTN_Q_SCALE
        if rope and n < 2 * E:
            y = jnp.concatenate([_rope(y[:, c:c + LANES], cos_ref[...], sin_ref[...])
                                 for c in range(0, N_CHUNK, LANES)], axis=1)
        o_ref[0, :, n:n + N_CHUNK] = y.astype(BF16)


def _attn_in(x, nw, mods, layer, row_of, w_bf16, rope=None, cache=None):
    B, T, D = x.shape
    N = w_bf16.shape[1]
    H, DV = ATTN_HEADS, ATTN_V_DIM
    tm = min(ROW_TILE, T)
    in_specs = [pl.BlockSpec((1, tm, D), lambda b, t: (b, t, 0)), _const_spec((1, D)),
                _mod_spec(layer, row_of), _const_spec((D, N))]
    args = [x, nw.reshape(1, D), mods, w_bf16]
    if rope is not None:
        in_specs += [pl.BlockSpec((tm, LANES), lambda b, t: (t, 0))] * 2
        args += list(rope)
    out_specs = [pl.BlockSpec((1, tm, N), lambda b, t: (b, t, 0))]
    out_shape = [jax.ShapeDtypeStruct((B, T, N), BF16)]
    n_cached = None
    if cache is not None:
        n_cached = 0 if cache[0] is None else cache[0].shape[1]
        if n_cached:
            in_specs += [pl.BlockSpec((None, n_cached, H, tm, DV), lambda b, t: (b, 0, 0, t, 0))] * 2
            args += list(cache)
        out_specs += [pl.BlockSpec((None, n_cached + 1, H, tm, DV), lambda b, t: (b, 0, 0, t, 0))] * 2
        out_shape += [jax.ShapeDtypeStruct((B, n_cached + 1, H, T, DV), F32)] * 2
    return pl.pallas_call(
        functools.partial(_attn_in_kernel, rope=rope is not None, n_cached=n_cached),
        grid=(B, T // tm),
        in_specs=in_specs,
        out_specs=out_specs,
        out_shape=out_shape,
        compiler_params=_cparams(("parallel", "parallel")),
        name="attn_in",
    )(*args)


def _rope(x, cos, sin):
    lane = lax.broadcasted_iota(jnp.int32, x.shape, 1)
    first = (lane % 32) < 16
    partner = jnp.where(first, pltpu.roll(x, LANES - 16, axis=1), pltpu.roll(x, 16, axis=1))
    return x * cos + partner * sin


def _attn_core_kernel(*refs, lam_init, has_ctx, tq, group):
    it = iter(refs)
    lam_ref, q_ref, k_ref, v_ref, g_ref, subw_ref = (next(it) for _ in range(6))
    if has_ctx:
        ck_ref, cv_ref = next(it), next(it)
    o_ref = next(it)
    T = q_ref.shape[1]
    n_heads = q_ref.shape[2] // LANES
    n_tiles = T // tq
    k_scr = [next(it) for _ in range(n_heads)]
    v_scr = [next(it) for _ in range(n_heads)]
    ring = 2 * group
    s_scr = [[next(it), next(it)] for _ in range(ring)]
    e_scr = [[next(it), next(it)] for _ in range(ring)]
    m_scr = [[next(it), next(it)] for _ in range(ring)]

    lv = lam_ref[...]
    lam = (jnp.exp(jnp.sum(lv[0:1] * lv[1:2], axis=-1, keepdims=True))
           - jnp.exp(jnp.sum(lv[2:3] * lv[3:4], axis=-1, keepdims=True)) + lam_init)

    tk = k_scr[0].shape[0]
    for hh in range(n_heads):
        hl = slice(hh * LANES, (hh + 1) * LANES)
        if has_ctx:
            k_scr[hh][T:, :] = ck_ref[hh].astype(BF16)
            v_scr[hh][T:, 0:LANES] = cv_ref[hh].astype(BF16)
        k_scr[hh][0:T, :] = k_ref[0, :, hl]
        v_scr[hh][0:T, 0:LANES] = v_ref[0, :, hl]
        v_scr[hh][:, LANES:2 * LANES] = jnp.ones((tk, LANES), BF16)

    kblocks = [slice(c, c + MXU_DIM) for c in range(0, tk, MXU_DIM)]
    tasks = [(hh, i) for hh in range(n_heads) for i in range(n_tiles)]

    def halves_max(x):
        return jnp.maximum(x[:, 0:LANES], x[:, LANES:2 * LANES])

    def score_stage(n):
        hh, i = tasks[n]
        q = q_ref[0, i * tq:(i + 1) * tq, hh * LANES:(hh + 1) * LANES]
        lane = lax.broadcasted_iota(jnp.int32, q.shape, 1)
        for mp in range(2):
            sel = (lane < ATTN_HEAD_DIM) if mp == 0 else (lane >= ATTN_HEAD_DIM)
            qm = jnp.where(sel, q, jnp.zeros_like(q))
            mx = None
            for kb in kblocks:
                s = _mm(qm, k_scr[hh][kb, :], DN_NT)
                s_scr[n % ring][mp][:, kb] = s
                mx = halves_max(s) if mx is None else jnp.maximum(mx, halves_max(s))
                yield
            m_scr[n % ring][mp][...] = jnp.broadcast_to(jnp.max(mx, axis=-1, keepdims=True), mx.shape)

    def attend_stage(n):
        hh, i = tasks[n]
        outs = []
        for mp in range(2):
            m = m_scr[n % ring][mp][...]
            m = jnp.concatenate([m, m], axis=1)
            for kb in kblocks:
                e_scr[n % ring][mp][:, kb] = jnp.exp2(s_scr[n % ring][mp][:, kb] - m).astype(BF16)
                yield
            ov = _mm(e_scr[n % ring][mp][...], v_scr[hh][...])
            outs.append((ov[:, 0:LANES], ov[:, LANES:2 * LANES]))
            yield
        (o1, d1), (o2, d2) = outs
        o = o1 * (1.0 / d1) - o2 * (lam / d2)
        ms = jnp.mean(o * o, axis=-1, keepdims=True)
        o = o * lax.rsqrt(ms + 1e-5) * subw_ref[...] * (1.0 - lam_init)
        rows, hl = slice(i * tq, (i + 1) * tq), slice(hh * LANES, (hh + 1) * LANES)
        o_ref[0, rows, hl] = (o * _silu(g_ref[0, rows, hl].astype(F32))).astype(BF16)

    n_groups = len(tasks) // group
    for t in range(n_groups + 1):
        live = []
        if t < n_groups:
            live += [score_stage(t * group + u) for u in range(group)]
        if t >= 1:
            live += [attend_stage((t - 1) * group + u) for u in range(group)]
        while live:
            for gen in list(live):
                if next(gen, "done") == "done":
                    live.remove(gen)


def _attn_core(qkvg, lam_vecs, subw, layer_idx, ctx=None):
    B, T, _ = qkvg.shape
    H, DV = ATTN_HEADS, ATTN_V_DIM
    HS = min(H, max(1, ATTN_ROWS_PER_STEP // T))
    lam_init = 0.8 - 0.6 * math.exp(-0.3 * layer_idx)
    has_ctx = ctx is not None
    tq = min(ATTN_Q_TILE, T)
    tk = T + (ctx[0].shape[3] if has_ctx else 0)

    def col(off):
        return pl.BlockSpec((1, T, HS * DV), lambda b, h: (b, 0, off + h))

    nb = H // HS
    in_specs = [pl.BlockSpec((4, ATTN_HEAD_DIM), lambda b, h: (0, 0)),
                col(0), col(nb), col(2 * nb), col(3 * nb),
                pl.BlockSpec((1, DV), lambda b, h: (0, 0))]
    args = [lam_vecs, qkvg, qkvg, qkvg, qkvg, subw.reshape(1, DV)]
    if has_ctx:
        ck, cv, j = ctx
        P = ck.shape[3]
        cache_spec = pl.BlockSpec((None, None, HS, P, DV), lambda b, h: (b, j, h, 0, 0))
        in_specs += [cache_spec, cache_spec]
        args += [ck, cv]
    group = ATTN_TASK_GROUP if tk // MXU_DIM >= 2 * ATTN_TASK_GROUP else 1
    assert (HS * (T // tq)) % group == 0
    n_buf = 2 * group * 2
    return pl.pallas_call(
        functools.partial(_attn_core_kernel, lam_init=lam_init, has_ctx=has_ctx, tq=tq, group=group),
        grid=(B, nb),
        in_specs=in_specs,
        out_specs=pl.BlockSpec((1, T, HS * DV), lambda b, h: (b, 0, h)),
        out_shape=jax.ShapeDtypeStruct((B, T, D_INNER), BF16),
        scratch_shapes=([pltpu.VMEM((tk, DV), BF16)] * HS + [pltpu.VMEM((tk, 2 * DV), BF16)] * HS
                        + [pltpu.VMEM((tq, tk), F32)] * n_buf + [pltpu.VMEM((tq, tk), BF16)] * n_buf
                        + [pltpu.VMEM((tq, LANES), F32)] * n_buf),
        compiler_params=_cparams(("parallel", "parallel")),
        name="attn_core",
    )(*args)


def _gated_residual(o, w_ref, x_ref, mod_ref):
    return x_ref[0] + mod_ref[:, 2 * D_MODEL:3 * D_MODEL] * _dot(o, w_ref[...])


def _out_proj_kernel(o_ref, w_ref, x_ref, mod_ref, *rest, final_norm):
    xn = _gated_residual(o_ref[0], w_ref, x_ref, mod_ref)
    if final_norm:
        fw_ref, y_ref = rest
        ms = jnp.mean(xn * xn, axis=-1, keepdims=True)
        xn = xn * lax.rsqrt(ms + NORM_EPS) * fw_ref[...]
    else:
        (y_ref,) = rest
    y_ref[0] = xn


def _out_proj(o, w_bf16, x, mods, layer, row_of, final_w=None):
    B, T, D = x.shape
    tm = min(ROW_TILE, T)
    tile = pl.BlockSpec((1, tm, D), lambda b, t: (b, t, 0))
    in_specs = [tile, _const_spec((D, D)), tile, _mod_spec(layer, row_of)]
    args = [o, w_bf16, x, mods]
    if final_w is not None:
        in_specs.append(_const_spec((1, D)))
        args.append(final_w.reshape(1, D))
    return pl.pallas_call(
        functools.partial(_out_proj_kernel, final_norm=final_w is not None),
        grid=(B, T // tm),
        in_specs=in_specs,
        out_specs=tile,
        out_shape=jax.ShapeDtypeStruct((B, T, D), F32),
        compiler_params=_cparams(("parallel", "parallel")),
        name="out_proj",
    )(*args)


def _rwkv_in_kernel(x_ref, xp_ref, xn_ref, nw_ref, mod_ref, mu_ref, w_ref, l1_ref, w2_ref, a2_ref,
                    w0_ref, a0_ref, kk_ref, ka_ref, rk_ref, seg_ref,
                    r_out, kk_out, v_out, sg_out, bonus_out, lw_out, a_out, kz_out):
    t = pl.program_id(1)
    nt = pl.num_programs(1)
    nw = nw_ref[...]
    mod = mod_ref[...]
    h = _normmod(x_ref[0], nw, mod)
    tm = h.shape[0]
    hp = _normmod(xp_ref[0], nw, mod)[SUBLANES - 1:SUBLANES, :]
    hn = _normmod(xn_ref[0], nw, mod)[0:1, :]
    hp = jnp.where(t > 0, hp, 0.0)
    hn = jnp.where(t < nt - 1, hn, 0.0)
    row = lax.broadcasted_iota(jnp.int32, h.shape, 0)
    h_prev = jnp.where(row == 0, hp, pltpu.roll(h, 1, axis=0))
    h_next = jnp.where(row == tm - 1, hn, pltpu.roll(h, tm - 1, axis=0))
    dx = 0.5 * (h_prev + h_next) - h

    def mix(n):
        return (h + dx * mu_ref[n:n + 1, :]).astype(BF16)

    E = D_INNER
    r = _mm(mix(0), w_ref[:, 0:E])
    k = _mm(mix(1), w_ref[:, E:2 * E])
    v = _mm(mix(2), w_ref[:, 2 * E:3 * E])
    g = _mm(mix(3), w_ref[:, 3 * E:4 * E])
    lw1 = jnp.tanh(_mm(mix(4), l1_ref[:, 0:2 * LORA]))
    la1 = _mm(mix(5), l1_ref[:, 2 * LORA:4 * LORA])
    lane = lax.broadcasted_iota(jnp.int32, lw1.shape, 1)

    kk = k * kk_ref[...]
    kk = kk * lax.rsqrt(jnp.maximum(_headsum(kk * kk, seg_ref), 1e-24))
    r_out[0] = r
    kk_out[0] = kk
    v_out[0] = v
    sg_out[0] = _silu(g).astype(BF16)

    ksum = None
    for z in range(2):
        sel = (lane >= z * LORA) & (lane < (z + 1) * LORA)
        lw = _dot(jnp.where(sel, lw1, 0.0), w2_ref[...])
        la = _dot(jnp.where(sel, la1, 0.0), a2_ref[...])
        lw_out[z, 0] = -math.exp(-0.5) * jax.nn.sigmoid(w0_ref[z:z + 1, :] + lw)
        a = jax.nn.sigmoid(a0_ref[z:z + 1, :] + la)
        a_out[z, 0] = a
        kz = k * (1.0 + (a - 1.0) * ka_ref[...])
        kz_out[z, 0] = kz
        ksum = kz if ksum is None else ksum + kz
    bonus_out[0] = (_headsum(r * rk_ref[...] * ksum, seg_ref) * v).astype(BF16)


def _rwkv_in(x, nw, mods, layer, row_of, p):
    B, T, D = x.shape
    E = D_INNER
    tm = min(RWKV_ROW_TILE, T)
    g8 = tm // SUBLANES
    last8 = T // SUBLANES - 1
    tile = pl.BlockSpec((1, tm, D), lambda b, t: (b, t, 0))
    prev8 = pl.BlockSpec((1, SUBLANES, D), lambda b, t: (b, jnp.maximum(t * g8 - 1, 0), 0))
    next8 = pl.BlockSpec((1, SUBLANES, D), lambda b, t: (b, jnp.minimum((t + 1) * g8, last8), 0))
    otile = pl.BlockSpec((1, tm, E), lambda b, t: (b, t, 0))
    ztile = pl.BlockSpec((2, 1, tm, E), lambda b, t: (0, b, t, 0))
    one = jax.ShapeDtypeStruct((B, T, E), F32)
    two = jax.ShapeDtypeStruct((2, B, T, E), F32)
    return pl.pallas_call(
        _rwkv_in_kernel,
        grid=(B, T // tm),
        in_specs=[tile, prev8, next8, _const_spec((1, D)), _mod_spec(layer, row_of),
                  _const_spec((6, D)), _const_spec((D, 4 * E)), _const_spec((D, 4 * LORA)),
                  _const_spec((2 * LORA, E)), _const_spec((2 * LORA, E)),
                  _const_spec((2, E)), _const_spec((2, E)),
                  _const_spec((1, E)), _const_spec((1, E)), _const_spec((1, E)),
                  _const_spec((MXU_DIM, MXU_DIM))],
        out_specs=[otile] * 5 + [ztile] * 3,
        out_shape=[one] * 3 + [jax.ShapeDtypeStruct((B, T, E), BF16)] * 2 + [two] * 3,
        compiler_params=_cparams(("parallel", "parallel")),
        name="rwkv_in",
    )(x, x, x, nw.reshape(1, D), mods, p["mu"], p["w_in"], p["l1"], p["w2"], p["a2"],
      p["w0"], p["a0"], p["k_k"], p["k_a"], p["r_k"], p["seg"])


def _scan_kernel(*refs, has_init):
    n_in = 6
    ins = [refs[z * n_in:(z + 1) * n_in] for z in range(2)]
    rest = refs[2 * n_in:]
    if has_init:
        s0_refs, rest = rest[0:2], rest[2:]
    y_refs, sf_refs, s_scr = rest[0:2], rest[2:4], rest[4]
    C = SCAN_CHUNK
    PW = 2 * C
    HD = RWKV_HEAD_DIM
    NP = y_refs[0].shape[-1] // LANES
    chains = [(z, p) for z in range(2) for p in range(NP)]
    idx = range(len(chains))
    zs = [z for z, _ in chains]

    @pl.when(pl.program_id(1) == 0)
    def _():
        for z in range(2):
            s_scr[z] = s0_refs[z][...] if has_init else jnp.zeros_like(s_scr[z])

    state_diag = (lax.broadcasted_iota(jnp.int32, (HD, LANES), 0)
                  == lax.broadcasted_iota(jnp.int32, (HD, LANES), 1) % HD)
    tr = lax.broadcasted_iota(jnp.int32, (C, PW), 0)
    tc = lax.broadcasted_iota(jnp.int32, (C, PW), 1) % C
    eye = jnp.where(tr == tc, 1.0, 0.0)
    li = lax.broadcasted_iota(jnp.int32, (C, C), 0)
    lj = lax.broadcasted_iota(jnp.int32, (C, C), 1)
    strict, incl, cum_mat = [], [], []
    for z in range(2):
        before = (tc > tr) if z == 1 else (tc < tr)
        strict.append(before)
        incl.append(before | (tc == tr))
        cum_mat.append(jnp.where((lj >= li) if z == 1 else (lj <= li), 1.0, 0.0).astype(BF16))
    lo = lax.broadcasted_iota(jnp.int32, (C, LANES), 1) < HD
    zeros_c = jnp.zeros((C, LANES), F32)
    zeros_p = jnp.zeros((PW, LANES), F32)

    def heads_down(x):
        return jnp.concatenate([jnp.where(lo, x, 0.0), jnp.where(lo, 0.0, x)], axis=0)

    def load(z, which, p):
        ref = ins[z][which]
        sl = slice(p * LANES, (p + 1) * LANES)
        return ref[0, :, sl] if len(ref.shape) == 3 else ref[0, 0, :, sl]

    lw = [load(z, 3, p) for z, p in chains]
    cum = []
    for i in idx:
        hi, lo16 = _split2(lw[i])
        cum.append(_mm(cum_mat[zs[i]], hi) + _mm(cum_mat[zs[i]], lo16))
    tot = [(cum[i][0:1] if zs[i] == 1 else cum[i][C - 1:C]) for i in idx]
    abar, rbar, bh, kh, v, res = [], [], [], [], [], []
    for i, (z, p) in enumerate(chains):
        kk = load(z, 1, p)
        b = kk * load(z, 4, p)
        kz = load(z, 5, p)
        e_neg = jnp.exp(-cum[i])
        e_tot = jnp.exp(tot[i] - cum[i])
        abar.append(-kk * jnp.exp(cum[i] - lw[i]))
        rbar.append(load(z, 0, p) * jnp.exp(cum[i]))
        bt, kt = b * e_neg, kz * e_neg
        bh.append(b * e_tot)
        kh.append(kz * e_tot)
        v.append(load(z, 2, p))
        lhs = jnp.concatenate([abar[i], rbar[i]], axis=0)
        rhs = jnp.concatenate([heads_down(bt), heads_down(kt)], axis=0)
        res.append(_mm(lhs.astype(BF16), rhs.astype(BF16), DN_NT))
    a_ab = [jnp.where(strict[zs[i]], res[i][0:C, 0:PW], 0.0) for i in idx]
    a_ak = [jnp.where(strict[zs[i]], res[i][0:C, PW:2 * PW], 0.0) for i in idx]
    a_rb = [jnp.where(incl[zs[i]], res[i][C:PW, 0:PW], 0.0) for i in idx]
    a_rk = [jnp.where(incl[zs[i]], res[i][C:PW, PW:2 * PW], 0.0) for i in idx]
    vd = [heads_down(x) for x in v]
    wv = [_dot(a_ak[i], vd[i]) for i in idx]

    pw = [_dot(x, heads_down(x)) for x in a_ab]
    q = [eye + x for x in a_ab]
    n = 4
    while n < C:
        m = [_dot(pw[i], jnp.concatenate([heads_down(q[i]), heads_down(pw[i])], axis=1)) for i in idx]
        q = [q[i] + m[i][:, 0:PW] for i in idx]
        pw = [x[:, PW:2 * PW] for x in m]
        n *= 2
    q = [q[i] + _dot(pw[i], heads_down(q[i])) for i in idx]

    x = [_dot(q[i], jnp.concatenate([heads_down(abar[i]), heads_down(wv[i])], axis=1)) for i in idx]
    ahat = [t[:, 0:LANES] for t in x]
    u0 = [t[:, LANES:2 * LANES] for t in x]

    ry = [_dot(jnp.concatenate([a_rb[i], a_rk[i]], axis=1),
               jnp.concatenate([jnp.concatenate([heads_down(ahat[i]), heads_down(u0[i])], axis=1),
                                jnp.concatenate([zeros_p, vd[i]], axis=1)], axis=0)) for i in idx]
    rhat = [rbar[i] + ry[i][:, 0:LANES] for i in idx]
    y0 = [t[:, LANES:2 * LANES] for t in ry]

    mn = [_dot(jnp.concatenate([bh[i], kh[i]], axis=0).T,
               jnp.concatenate([jnp.concatenate([ahat[i], u0[i]], axis=1),
                                jnp.concatenate([zeros_c, v[i]], axis=1)], axis=0)) for i in idx]
    for i, (z, p) in enumerate(chains):
        m_c = jnp.where(lo, mn[i][0:HD, 0:LANES], mn[i][HD:2 * HD, 0:LANES]) + jnp.where(state_diag, jnp.exp(tot[i]), 0.0)
        n_c = jnp.where(lo, mn[i][0:HD, LANES:2 * LANES], mn[i][HD:2 * HD, LANES:2 * LANES])
        s_old = heads_down(s_scr[z, p])
        y_refs[z][0, :, p * LANES:(p + 1) * LANES] = _dot(rhat[i], s_old) + y0[i]
        s_new = _dot(m_c, s_old) + n_c
        s_scr[z, p] = s_new
        sf_refs[z][p] = s_new


def _rwkv_scan(r, kk, v, lw, a, kz, s0=None):
    B, T, E = r.shape
    NP = E // LANES
    C = SCAN_CHUNK
    nb = T // C
    HD = RWKV_HEAD_DIM
    state = pl.BlockSpec((None, NP, HD, LANES), lambda b, t: (b, 0, 0, 0))
    in_specs, args, y_specs = [], [], []
    for z in range(2):
        tok = (lambda t: t) if z == 0 else (lambda t: nb - 1 - t)
        one = pl.BlockSpec((1, C, E), lambda b, t, tok=tok: (b, tok(t), 0))
        two = pl.BlockSpec((1, 1, C, E), lambda b, t, tok=tok, z=z: (z, b, tok(t), 0))
        in_specs += [one, one, one, two, two, two]
        args += [r, kk, v, lw, a, kz]
        y_specs.append(one)
    if s0 is not None:
        in_specs += [state, state]
        args += list(s0)
    return pl.pallas_call(
        functools.partial(_scan_kernel, has_init=s0 is not None),
        grid=(B, nb),
        in_specs=in_specs,
        out_specs=y_specs + [state, state],
        out_shape=[jax.ShapeDtypeStruct((B, T, E), F32)] * 2
        + [jax.ShapeDtypeStruct((B, NP, HD, LANES), F32)] * 2,
        scratch_shapes=[pltpu.VMEM((2, NP, HD, LANES), F32)],
        compiler_params=_cparams(("parallel", "arbitrary")),
        name="rwkv_scan",
    )(*args)


def _rwkv_out_kernel(y0_ref, y1_ref, bonus_ref, sg_ref, lnw_ref, lnb_ref, seg_ref, w_ref, x_ref, mod_ref, o_ref):
    y = y0_ref[0] + y1_ref[0]
    inv_n = 1.0 / RWKV_HEAD_DIM
    d = y - _headsum(y, seg_ref) * inv_n
    var = _headsum(d * d, seg_ref) * inv_n
    yn = d * lax.rsqrt(var + RWKV_GN_EPS) * lnw_ref[...] + lnb_ref[...] + bonus_ref[0].astype(F32)
    o_ref[0] = _gated_residual(yn * sg_ref[0].astype(F32), w_ref, x_ref, mod_ref)


def _rwkv_out(y0, y1, bonus, sg, p, x, mods, layer, row_of):
    B, T, D = x.shape
    tm = min(ROW_TILE, T)
    tile = pl.BlockSpec((1, tm, D), lambda b, t: (b, t, 0))
    return pl.pallas_call(
        _rwkv_out_kernel,
        grid=(B, T // tm),
        in_specs=[tile, tile, tile, tile, _const_spec((1, D)), _const_spec((1, D)),
                  _const_spec((MXU_DIM, MXU_DIM)), _const_spec((D, D)), tile, _mod_spec(layer, row_of)],
        out_specs=tile,
        out_shape=jax.ShapeDtypeStruct((B, T, D), F32),
        compiler_params=_cparams(("parallel", "parallel")),
        name="rwkv_out",
    )(y0, y1, bonus, sg, p["ln_w"], p["ln_b"], p["seg"], p["w_out"], x, mods)


def _conv_in_kernel(x_ref, nw_ref, mod_ref, w_ref, z_ref, sg_ref):
    hb = _normmod(x_ref[0], nw_ref[...], mod_ref[...]).astype(BF16)
    E = D_INNER
    for n in range(0, E, N_CHUNK):
        a = _mm(hb, w_ref[:, n:n + N_CHUNK])
        b = _mm(hb, w_ref[:, E + n:E + n + N_CHUNK])
        g = _mm(hb, w_ref[:, 2 * E + n:2 * E + n + N_CHUNK])
        z_ref[0, :, n:n + N_CHUNK] = a * jax.nn.sigmoid(b)
        sg_ref[0, :, n:n + N_CHUNK] = _silu(g).astype(BF16)


def _conv_in(x, nw, mods, layer, row_of, w_bf16):
    B, T, D = x.shape
    E = D_INNER
    tm = min(ROW_TILE, T)
    tile = pl.BlockSpec((1, tm, D), lambda b, t: (b, t, 0))
    return pl.pallas_call(
        _conv_in_kernel,
        grid=(B, T // tm),
        in_specs=[tile, _const_spec((1, D)), _mod_spec(layer, row_of), _const_spec((D, 3 * E))],
        out_specs=[tile, tile],
        out_shape=[jax.ShapeDtypeStruct((B, T, E), F32), jax.ShapeDtypeStruct((B, T, E), BF16)],
        compiler_params=_cparams(("parallel", "parallel")),
        name="conv_in",
    )(x, nw.reshape(1, D), mods, w_bf16)


def _conv_out_kernel(z_ref, zp_ref, zn_ref, sg_ref, dw_ref, dwb_ref, lnw_ref, lnb_ref, w_ref, x_ref, mod_ref,
                     o_ref, zs, cbuf):
    t = pl.program_id(1)
    nt = pl.num_programs(1)
    tm = z_ref.shape[1]
    rows = tm + 2 * HALO
    zs[0, 0:HALO, :] = jnp.where(t > 0, zp_ref[0], 0.0)
    zs[0, HALO:HALO + tm, :] = z_ref[0]
    zs[0, HALO + tm:rows, :] = jnp.where(t < nt - 1, zn_ref[0], 0.0)
    for s in range(1, SUBLANES):
        zs[s, 0:rows - SUBLANES, :] = zs[0, s:s + rows - SUBLANES, :]
    off = HALO - CONV_PAD
    for c0 in range(0, D_INNER, LANES):
        for r0 in range(0, tm, CONV_ROWS):
            acc = None
            for k in range(CONV_WIDTH):
                q8, s = divmod(off + k, SUBLANES)
                a0 = r0 + q8 * SUBLANES
                term = dw_ref[k:k + 1, c0:c0 + LANES] * zs[s, a0:a0 + CONV_ROWS, c0:c0 + LANES]
                acc = term if acc is None else acc + term
            cbuf[r0:r0 + CONV_ROWS, c0:c0 + LANES] = acc
    c = cbuf[...] + dwb_ref[...]
    m = jnp.mean(c, axis=-1, keepdims=True)
    d = c - m
    var = jnp.mean(d * d, axis=-1, keepdims=True)
    y = d * lax.rsqrt(var + 1e-5) * lnw_ref[...] + lnb_ref[...]
    o_ref[0] = _gated_residual(_silu(y) * sg_ref[0].astype(F32), w_ref, x_ref, mod_ref)


def _conv_out(z, sg, p, x, mods, layer, row_of):
    B, T, D = x.shape
    tm = min(CONV_ROW_TILE, T)
    gh = tm // HALO
    lasth = T // HALO - 1
    tile = pl.BlockSpec((1, tm, D), lambda b, t: (b, t, 0))
    prev = pl.BlockSpec((1, HALO, D), lambda b, t: (b, jnp.maximum(t * gh - 1, 0), 0))
    nxt = pl.BlockSpec((1, HALO, D), lambda b, t: (b, jnp.minimum((t + 1) * gh, lasth), 0))
    return pl.pallas_call(
        _conv_out_kernel,
        grid=(B, T // tm),
        in_specs=[tile, prev, nxt, tile, _const_spec((CONV_WIDTH, D)), _const_spec((1, D)),
                  _const_spec((1, D)), _const_spec((1, D)), _const_spec((D, D)), tile,
                  _mod_spec(layer, row_of)],
        out_specs=tile,
        out_shape=jax.ShapeDtypeStruct((B, T, D), F32),
        scratch_shapes=[pltpu.VMEM((SUBLANES, tm + 2 * HALO, D), F32), pltpu.VMEM((tm, D), F32)],
        compiler_params=_cparams(("parallel", "parallel")),
        name="conv_out",
    )(z, z, z, sg, p["dw_w"], p["dw_b"], p["ln_w"], p["ln_b"], p["w_out"], x, mods)


def _rope_tables(n_tok):
    t = jnp.arange(n_tok, dtype=jnp.int32)
    pos = jnp.stack([(t // GRID_W).astype(F32), (t % GRID_W).astype(F32)], axis=1)
    lane = jnp.arange(LANES, dtype=jnp.int32)
    d = lane % ATTN_HEAD_DIM
    axis = d // 32
    second_half = (d % 32) // 16
    axis_dim = ATTN_HEAD_DIM // 2
    inv_freq = ROPE_BASE ** (-(2.0 * (d % 16).astype(F32)) / axis_dim)
    ang = pos[:, axis] * inv_freq[None, :]
    sign = jnp.where(second_half == 1, 1.0, -1.0).astype(F32)
    return jnp.cos(ang), jnp.sin(ang) * sign[None, :]


def _pair_states(s):
    B, H, N, _ = s.shape
    st = jnp.swapaxes(s, -1, -2).reshape(B, H // 2, 2, N, N)
    return jnp.swapaxes(st, 2, 3).reshape(B, H // 2, N, 2 * N)


def _unpair_states(sp):
    B, NP, N, _ = sp.shape
    st = jnp.swapaxes(sp.reshape(B, NP, N, 2, N), 2, 3)
    return jnp.swapaxes(st, -1, -2).reshape(B, 2 * NP, N, N)


def kernel(x_prompt, x_sample, cache_attn_k, cache_attn_v, state_rwkv, c, c_ctx, norm_w, ada_w, ada_b, attn_w_in, attn_lambda, attn_subln_w, attn_w_out, rwkv_mu, rwkv_w_in, rwkv_w0, rwkv_w1, rwkv_w2, rwkv_a0, rwkv_a1, rwkv_a2, rwkv_k_k, rwkv_k_a, rwkv_r_k, rwkv_ln_w, rwkv_ln_b, rwkv_w_out, conv_w_in, conv_dw_w, conv_dw_b, conv_ln_w, conv_ln_b, conv_w_out, final_norm_w):
    D, E = D_MODEL, D_INNER
    dec_batch = x_sample.shape[0]
    assert dec_batch < COND_ROWS
    cond = jnp.concatenate([c, c_ctx[None, :], jnp.zeros((COND_ROWS - dec_batch - 1, D), F32)], axis=0)
    mods = _ada(cond, ada_w, ada_b)
    ctx_row = dec_batch
    streams = [(x_prompt, lambda b: ctx_row), (x_sample, lambda b: b)]
    cos, sin = _rope_tables(x_sample.shape[1])
    seg = (jnp.arange(MXU_DIM)[:, None] // RWKV_HEAD_DIM == jnp.arange(MXU_DIM)[None, :] // RWKV_HEAD_DIM).astype(BF16)

    xs = [x_prompt, x_sample]

    def rows(a, s):
        B, T, D = a.shape
        if s == 0 and T < ROW_TILE and (B * T) % ROW_TILE == 0:
            return a.reshape(B * T // ROW_TILE, ROW_TILE, D)
        return a
    new_kv, new_s = (None, None), []
    assert (DEPTH - 1) % N_MIXERS == 0
    for i in range(DEPTH):
        kind, j = i % N_MIXERS, i // N_MIXERS
        last = i == DEPTH - 1
        fw = final_norm_w if last else None
        if kind == 0:
            w_in = attn_w_in[j].astype(BF16)
            w_out = attn_w_out[j].astype(BF16)
            for s, (_, row_of) in enumerate(streams):
                if s == 0:
                    qkvg, *new_kv = _attn_in(xs[s], norm_w[i], mods, i, row_of, w_in, cache=new_kv)
                    o = _attn_core(qkvg, attn_lambda[j], attn_subln_w[j], i)
                else:
                    (qkvg,) = _attn_in(xs[s], norm_w[i], mods, i, row_of, w_in, rope=(cos, sin))
                    o = _attn_core(qkvg, attn_lambda[j], attn_subln_w[j], i, ctx=(cache_attn_k, cache_attn_v, j))
                xs[s] = _out_proj(rows(o, s), w_out, rows(xs[s], s), mods, i, row_of, final_w=fw).reshape(xs[s].shape)
        elif kind == 1:
            p = dict(
                mu=rwkv_mu[j],
                w_in=rwkv_w_in[j].astype(BF16),
                l1=jnp.concatenate([rwkv_w1[j, 0], rwkv_w1[j, 1], rwkv_a1[j, 0], rwkv_a1[j, 1]], axis=1).astype(BF16),
                w2=rwkv_w2[j].reshape(2 * LORA, E).astype(BF16),
                a2=rwkv_a2[j].reshape(2 * LORA, E).astype(BF16),
                w0=rwkv_w0[j], a0=rwkv_a0[j],
                k_k=rwkv_k_k[j].reshape(1, E), k_a=rwkv_k_a[j].reshape(1, E), r_k=rwkv_r_k[j].reshape(1, E),
                ln_w=rwkv_ln_w[j].reshape(1, E), ln_b=rwkv_ln_b[j].reshape(1, E),
                w_out=rwkv_w_out[j].astype(BF16), seg=seg)
            for s, (_, row_of) in enumerate(streams):
                r, kk, v, sg, bonus, lw, a, kz = _rwkv_in(xs[s], norm_w[i], mods, i, row_of, p)
                s0 = [_pair_states(state_rwkv[:, j, z]) for z in range(2)] if s == 1 else None
                y_f, y_b, sf_f, sf_b = _rwkv_scan(r, kk, v, lw, a, kz, s0)
                if s == 0:
                    new_s.append(jnp.stack([_unpair_states(sf_f), _unpair_states(sf_b)], axis=1))
                xs[s] = _rwkv_out(rows(y_f, s), rows(y_b, s), rows(bonus, s), rows(sg, s), p, rows(xs[s], s),
                                  mods, i, row_of).reshape(xs[s].shape)
        else:
            p = dict(dw_w=conv_dw_w[j], dw_b=conv_dw_b[j].reshape(1, E), ln_w=conv_ln_w[j].reshape(1, E),
                     ln_b=conv_ln_b[j].reshape(1, E), w_out=conv_w_out[j].astype(BF16))
            w_in = conv_w_in[j].astype(BF16)
            for s, (_, row_of) in enumerate(streams):
                zz, sg = _conv_in(rows(xs[s], s), norm_w[i], mods, i, row_of, w_in)
                xs[s] = _conv_out(zz.reshape(xs[s].shape), sg.reshape(xs[s].shape), p, xs[s], mods, i, row_of)
    return (xs[0], xs[1], new_kv[0], new_kv[1], jnp.stack(new_s, axis=1))
```

```python
import functools
import math

import jax
import jax.numpy as jnp
from jax import lax
from jax.experimental import pallas as pl
from jax.experimental.pallas import tpu as pltpu

F32 = jnp.float32
BF16 = jnp.bfloat16

D_MODEL = 1024
D_INNER = 1024
DEPTH = 4
N_MIXERS = 3
GRID_W = 64
NORM_EPS = 1e-6
ATTN_HEAD_DIM = 64
ATTN_HEADS = 8
ATTN_V_DIM = 128
ROPE_BASE = 10000.0
RWKV_HEAD_DIM = 64
RWKV_HEADS = 16
RWKV_GN_EPS = RWKV_HEAD_DIM * 1e-5
LORA = 64
CONV_WIDTH = 31
CONV_PAD = CONV_WIDTH // 2

LANES = 128
SUBLANES = 8
MXU_DIM = 256
VMEM_LIMIT_BYTES = 56 * 1024 * 1024

ROW_TILE = 512
CONV_ROW_TILE = 256
RWKV_ROW_TILE = 256
ATTN_Q_TILE = 256
ATTN_ROWS_PER_STEP = 4096
ATTN_TASK_GROUP = 2
N_CHUNK = 512
SCAN_CHUNK = 64
HALO = 16
CONV_ROWS = 64
COND_ROWS = 16
DN = (((1,), (0,)), ((), ()))
DN_NT = (((1,), (1,)), ((), ()))


def _cparams(sem):
    return pltpu.CompilerParams(dimension_semantics=sem, vmem_limit_bytes=VMEM_LIMIT_BYTES)


def _mm(a, b, dn=DN):
    return lax.dot_general(a, b, dn, preferred_element_type=F32)


def _dot(a, b):
    return _mm(a.astype(BF16), b.astype(BF16))


def _split2(x):
    hi = x.astype(BF16)
    lo = (x - hi.astype(F32)).astype(BF16)
    return hi, lo


def _dot3(a, b):
    a_hi, a_lo = _split2(a)
    b_hi, b_lo = _split2(b)
    return _mm(a_hi, b_hi) + (_mm(a_lo, b_hi) + _mm(a_hi, b_lo))


def _silu(x):
    return x * jax.nn.sigmoid(x)


def _normmod(x, nw, mod):
    ms = jnp.mean(x * x, axis=-1, keepdims=True)
    y = x * lax.rsqrt(ms + NORM_EPS) * nw
    return y * (1.0 + mod[:, D_MODEL:2 * D_MODEL]) + mod[:, 0:D_MODEL]


def _headsum(x, seg_ref):
    cols = []
    for c in range(0, x.shape[-1], MXU_DIM):
        hi, lo = _split2(x[:, c:c + MXU_DIM])
        cols.append(_mm(hi, seg_ref[...]) + _mm(lo, seg_ref[...]))
    return jnp.concatenate(cols, axis=-1)


def _mod_spec(layer, row_of):
    return pl.BlockSpec((None, None, 1, 3 * D_MODEL), lambda b, t: (layer, row_of(b), 0, 0))


def _const_spec(shape):
    return pl.BlockSpec(shape, lambda b, t: (0,) * len(shape))


def _ada_kernel(cond_ref, w_ref, b_ref, o_ref):
    o_ref[0, :, 0, :] = _dot3(_silu(cond_ref[...]), w_ref[0]) + b_ref[0]


def _ada(cond, ada_w, ada_b):
    depth, d, n3 = ada_w.shape
    tn = 1024
    return pl.pallas_call(
        _ada_kernel,
        grid=(depth, n3 // tn),
        in_specs=[
            pl.BlockSpec((COND_ROWS, d), lambda i, j: (0, 0)),
            pl.BlockSpec((1, d, tn), lambda i, j: (i, 0, j)),
            pl.BlockSpec((1, 1, tn), lambda i, j: (i, 0, j)),
        ],
        out_specs=pl.BlockSpec((1, COND_ROWS, 1, tn), lambda i, j: (i, 0, 0, j)),
        out_shape=jax.ShapeDtypeStruct((depth, COND_ROWS, 1, n3), F32),
        compiler_params=_cparams(("parallel", "parallel")),
        name="ada_mod",
    )(cond, ada_w, ada_b.reshape(depth, 1, n3))


ATTN_Q_SCALE = ATTN_HEAD_DIM ** -0.5 * math.log2(math.e)


def _attn_in_kernel(x_ref, nw_ref, mod_ref, w_ref, *rest, rope, n_cached):
    rest = list(rest)
    if rope:
        cos_ref, sin_ref = rest.pop(0), rest.pop(0)
    if n_cached:
        pk_ref, pv_ref = rest.pop(0), rest.pop(0)
    o_ref = rest.pop(0)
    if n_cached is not None:
        ko_ref, vo_ref = rest
        if n_cached:
            ko_ref[0:n_cached] = pk_ref[...]
            vo_ref[0:n_cached] = pv_ref[...]
    E = D_INNER
    hb = _normmod(x_ref[0], nw_ref[...], mod_ref[...]).astype(BF16)
    for n in range(0, 4 * E, N_CHUNK):
        y = _mm(hb, w_ref[:, n:n + N_CHUNK])
        if n_cached is not None and E <= n < 3 * E:
            cache_ref = ko_ref if n < 2 * E else vo_ref
            for c in range(0, N_CHUNK, LANES):
                cache_ref[n_cached, (n % E + c) // LANES] = y[:, c:c + LANES]
        if n < E:
            y = y * ATTN_Q_SCALE
        if rope and n < 2 * E:
            y = jnp.concatenate([_rope(y[:, c:c + LANES], cos_ref[...], sin_ref[...])
                                 for c in range(0, N_CHUNK, LANES)], axis=1)
        o_ref[0, :, n:n + N_CHUNK] = y.astype(BF16)


def _attn_in(x, nw, mods, layer, row_of, w_bf16, rope=None, cache=None):
    B, T, D = x.shape
    N = w_bf16.shape[1]
    H, DV = ATTN_HEADS, ATTN_V_DIM
    tm = min(ROW_TILE, T)
    in_specs = [pl.BlockSpec((1, tm, D), lambda b, t: (b, t, 0)), _const_spec((1, D)),
                _mod_spec(layer, row_of), _const_spec((D, N))]
    args = [x, nw.reshape(1, D), mods, w_bf16]
    if rope is not None:
        in_specs += [pl.BlockSpec((tm, LANES), lambda b, t: (t, 0))] * 2
        args += list(rope)
    out_specs = [pl.BlockSpec((1, tm, N), lambda b, t: (b, t, 0))]
    out_shape = [jax.ShapeDtypeStruct((B, T, N), BF16)]
    n_cached = None
    if cache is not None:
        n_cached = 0 if cache[0] is None else cache[0].shape[1]
        if n_cached:
            in_specs += [pl.BlockSpec((None, n_cached, H, tm, DV), lambda b, t: (b, 0, 0, t, 0))] * 2
            args += list(cache)
        out_specs += [pl.BlockSpec((None, n_cached + 1, H, tm, DV), lambda b, t: (b, 0, 0, t, 0))] * 2
        out_shape += [jax.ShapeDtypeStruct((B, n_cached + 1, H, T, DV), F32)] * 2
    return pl.pallas_call(
        functools.partial(_attn_in_kernel, rope=rope is not None, n_cached=n_cached),
        grid=(B, T // tm),
        in_specs=in_specs,
        out_specs=out_specs,
        out_shape=out_shape,
        compiler_params=_cparams(("parallel", "parallel")),
        name="attn_in",
    )(*args)


def _rope(x, cos, sin):
    lane = lax.broadcasted_iota(jnp.int32, x.shape, 1)
    first = (lane % 32) < 16
    partner = jnp.where(first, pltpu.roll(x, LANES - 16, axis=1), pltpu.roll(x, 16, axis=1))
    return x * cos + partner * sin


def _attn_core_kernel(*refs, lam_init, has_ctx, tq, group):
    it = iter(refs)
    lam_ref, q_ref, k_ref, v_ref, g_ref, subw_ref = (next(it) for _ in range(6))
    if has_ctx:
        ck_ref, cv_ref = next(it), next(it)
    o_ref = next(it)
    T = q_ref.shape[1]
    n_heads = q_ref.shape[2] // LANES
    n_tiles = T // tq
    k_scr = [next(it) for _ in range(n_heads)]
    v_scr = [next(it) for _ in range(n_heads)]
    ring = 2 * group
    s_scr = [[next(it), next(it)] for _ in range(ring)]
    e_scr = [[next(it), next(it)] for _ in range(ring)]
    m_scr = [[next(it), next(it)] for _ in range(ring)]

    lv = lam_ref[...]
    lam = (jnp.exp(jnp.sum(lv[0:1] * lv[1:2], axis=-1, keepdims=True))
           - jnp.exp(jnp.sum(lv[2:3] * lv[3:4], axis=-1, keepdims=True)) + lam_init)

    tk = k_scr[0].shape[0]
    for hh in range(n_heads):
        hl = slice(hh * LANES, (hh + 1) * LANES)
        if has_ctx:
            k_scr[hh][T:, :] = ck_ref[hh].astype(BF16)
            v_scr[hh][T:, 0:LANES] = cv_ref[hh].astype(BF16)
        k_scr[hh][0:T, :] = k_ref[0, :, hl]
        v_scr[hh][0:T, 0:LANES] = v_ref[0, :, hl]
        v_scr[hh][:, LANES:2 * LANES] = jnp.ones((tk, LANES), BF16)

    kblocks = [slice(c, c + MXU_DIM) for c in range(0, tk, MXU_DIM)]
    tasks = [(hh, i) for hh in range(n_heads) for i in range(n_tiles)]

    def halves_max(x):
        return jnp.maximum(x[:, 0:LANES], x[:, LANES:2 * LANES])

    def score_stage(n):
        hh, i = tasks[n]
        q = q_ref[0, i * tq:(i + 1) * tq, hh * LANES:(hh + 1) * LANES]
        lane = lax.broadcasted_iota(jnp.int32, q.shape, 1)
        for mp in range(2):
            sel = (lane < ATTN_HEAD_DIM) if mp == 0 else (lane >= ATTN_HEAD_DIM)
            qm = jnp.where(sel, q, jnp.zeros_like(q))
            mx = None
            for kb in kblocks:
                s = _mm(qm, k_scr[hh][kb, :], DN_NT)
                s_scr[n % ring][mp][:, kb] = s
                mx = halves_max(s) if mx is None else jnp.maximum(mx, halves_max(s))
                yield
            m_scr[n % ring][mp][...] = jnp.broadcast_to(jnp.max(mx, axis=-1, keepdims=True), mx.shape)

    def attend_stage(n):
        hh, i = tasks[n]
        outs = []
        for mp in range(2):
            m = m_scr[n % ring][mp][...]
            m = jnp.concatenate([m, m], axis=1)
            for kb in kblocks:
                e_scr[n % ring][mp][:, kb] = jnp.exp2(s_scr[n % ring][mp][:, kb] - m).astype(BF16)
                yield
            ov = _mm(e_scr[n % ring][mp][...], v_scr[hh][...])
            outs.append((ov[:, 0:LANES], ov[:, LANES:2 * LANES]))
            yield
        (o1, d1), (o2, d2) = outs
        o = o1 * (1.0 / d1) - o2 * (lam / d2)
        ms = jnp.mean(o * o, axis=-1, keepdims=True)
        o = o * lax.rsqrt(ms + 1e-5) * subw_ref[...] * (1.0 - lam_init)
        rows, hl = slice(i * tq, (i + 1) * tq), slice(hh * LANES, (hh + 1) * LANES)
        o_ref[0, rows, hl] = (o * _silu(g_ref[0, rows, hl].astype(F32))).astype(BF16)

    n_groups = len(tasks) // group
    for t in range(n_groups + 1):
        live = []
        if t < n_groups:
            live += [score_stage(t * group + u) for u in range(group)]
        if t >= 1:
            live += [attend_stage((t - 1) * group + u) for u in range(group)]
        while live:
            for gen in list(live):
                if next(gen, "done") == "done":
                    live.remove(gen)


def _attn_core(qkvg, lam_vecs, subw, layer_idx, ctx=None):
    B, T, _ = qkvg.shape
    H, DV = ATTN_HEADS, ATTN_V_DIM
    HS = min(H, max(1, ATTN_ROWS_PER_STEP // T))
    lam_init = 0.8 - 0.6 * math.exp(-0.3 * layer_idx)
    has_ctx = ctx is not None
    tq = min(ATTN_Q_TILE, T)
    tk = T + (ctx[0].shape[3] if has_ctx else 0)

    def col(off):
        return pl.BlockSpec((1, T, HS * DV), lambda b, h: (b, 0, off + h))

    nb = H // HS
    in_specs = [pl.BlockSpec((4, ATTN_HEAD_DIM), lambda b, h: (0, 0)),
                col(0), col(nb), col(2 * nb), col(3 * nb),
                pl.BlockSpec((1, DV), lambda b, h: (0, 0))]
    args = [lam_vecs, qkvg, qkvg, qkvg, qkvg, subw.reshape(1, DV)]
    if has_ctx:
        ck, cv, j = ctx
        P = ck.shape[3]
        cache_spec = pl.BlockSpec((None, None, HS, P, DV), lambda b, h: (b, j, h, 0, 0))
        in_specs += [cache_spec, cache_spec]
        args += [ck, cv]
    group = ATTN_TASK_GROUP if tk // MXU_DIM >= 2 * ATTN_TASK_GROUP else 1
    assert (HS * (T // tq)) % group == 0
    n_buf = 2 * group * 2
    return pl.pallas_call(
        functools.partial(_attn_core_kernel, lam_init=lam_init, has_ctx=has_ctx, tq=tq, group=group),
        grid=(B, nb),
        in_specs=in_specs,
        out_specs=pl.BlockSpec((1, T, HS * DV), lambda b, h: (b, 0, h)),
        out_shape=jax.ShapeDtypeStruct((B, T, D_INNER), BF16),
        scratch_shapes=([pltpu.VMEM((tk, DV), BF16)] * HS + [pltpu.VMEM((tk, 2 * DV), BF16)] * HS
                        + [pltpu.VMEM((tq, tk), F32)] * n_buf + [pltpu.VMEM((tq, tk), BF16)] * n_buf
                        + [pltpu.VMEM((tq, LANES), F32)] * n_buf),
        compiler_params=_cparams(("parallel", "parallel")),
        name="attn_core",
    )(*args)


def _gated_residual(o, w_ref, x_ref, mod_ref):
    return x_ref[0] + mod_ref[:, 2 * D_MODEL:3 * D_MODEL] * _dot(o, w_ref[...])


def _out_proj_kernel(o_ref, w_ref, x_ref, mod_ref, *rest, final_norm):
    xn = _gated_residual(o_ref[0], w_ref, x_ref, mod_ref)
    if final_norm:
        fw_ref, y_ref = rest
        ms = jnp.mean(xn * xn, axis=-1, keepdims=True)
        xn = xn * lax.rsqrt(ms + NORM_EPS) * fw_ref[...]
    else:
        (y_ref,) = rest
    y_ref[0] = xn


def _out_proj(o, w_bf16, x, mods, layer, row_of, final_w=None):
    B, T, D = x.shape
    tm = min(ROW_TILE, T)
    tile = pl.BlockSpec((1, tm, D), lambda b, t: (b, t, 0))
    in_specs = [tile, _const_spec((D, D)), tile, _mod_spec(layer, row_of)]
    args = [o, w_bf16, x, mods]
    if final_w is not None:
        in_specs.append(_const_spec((1, D)))
        args.append(final_w.reshape(1, D))
    return pl.pallas_call(
        functools.partial(_out_proj_kernel, final_norm=final_w is not None),
        grid=(B, T // tm),
        in_specs=in_specs,
        out_specs=tile,
        out_shape=jax.ShapeDtypeStruct((B, T, D), F32),
        compiler_params=_cparams(("parallel", "parallel")),
        name="out_proj",
    )(*args)


def _rwkv_in_kernel(x_ref, xp_ref, xn_ref, nw_ref, mod_ref, mu_ref, w_ref, l1_ref, w2_ref, a2_ref,
                    w0_ref, a0_ref, kk_ref, ka_ref, rk_ref, seg_ref,
                    r_out, kk_out, v_out, sg_out, bonus_out, lw_out, a_out, kz_out):
    t = pl.program_id(1)
    nt = pl.num_programs(1)
    nw = nw_ref[...]
    mod = mod_ref[...]
    h = _normmod(x_ref[0], nw, mod)
    tm = h.shape[0]
    hp = _normmod(xp_ref[0], nw, mod)[SUBLANES - 1:SUBLANES, :]
    hn = _normmod(xn_ref[0], nw, mod)[0:1, :]
    hp = jnp.where(t > 0, hp, 0.0)
    hn = jnp.where(t < nt - 1, hn, 0.0)
    row = lax.broadcasted_iota(jnp.int32, h.shape, 0)
    h_prev = jnp.where(row == 0, hp, pltpu.roll(h, 1, axis=0))
    h_next = jnp.where(row == tm - 1, hn, pltpu.roll(h, tm - 1, axis=0))
    dx = 0.5 * (h_prev + h_next) - h

    def mix(n):
        return (h + dx * mu_ref[n:n + 1, :]).astype(BF16)

    E = D_INNER
    r = _mm(mix(0), w_ref[:, 0:E])
    k = _mm(mix(1), w_ref[:, E:2 * E])
    v = _mm(mix(2), w_ref[:, 2 * E:3 * E])
    g = _mm(mix(3), w_ref[:, 3 * E:4 * E])
    lw1 = jnp.tanh(_mm(mix(4), l1_ref[:, 0:2 * LORA]))
    la1 = _mm(mix(5), l1_ref[:, 2 * LORA:4 * LORA])
    lane = lax.broadcasted_iota(jnp.int32, lw1.shape, 1)

    kk = k * kk_ref[...]
    kk = kk * lax.rsqrt(jnp.maximum(_headsum(kk * kk, seg_ref), 1e-24))
    r_out[0] = r
    kk_out[0] = kk
    v_out[0] = v
    sg_out[0] = _silu(g).astype(BF16)

    ksum = None
    for z in range(2):
        sel = (lane >= z * LORA) & (lane < (z + 1) * LORA)
        lw = _dot(jnp.where(sel, lw1, 0.0), w2_ref[...])
        la = _dot(jnp.where(sel, la1, 0.0), a2_ref[...])
        lw_out[z, 0] = -math.exp(-0.5) * jax.nn.sigmoid(w0_ref[z:z + 1, :] + lw)
        a = jax.nn.sigmoid(a0_ref[z:z + 1, :] + la)
        a_out[z, 0] = a
        kz = k * (1.0 + (a - 1.0) * ka_ref[...])
        kz_out[z, 0] = kz
        ksum = kz if ksum is None else ksum + kz
    bonus_out[0] = (_headsum(r * rk_ref[...] * ksum, seg_ref) * v).astype(BF16)


def _rwkv_in(x, nw, mods, layer, row_of, p):
    B, T, D = x.shape
    E = D_INNER
    tm = min(RWKV_ROW_TILE, T)
    g8 = tm // SUBLANES
    last8 = T // SUBLANES - 1
    tile = pl.BlockSpec((1, tm, D), lambda b, t: (b, t, 0))
    prev8 = pl.BlockSpec((1, SUBLANES, D), lambda b, t: (b, jnp.maximum(t * g8 - 1, 0), 0))
    next8 = pl.BlockSpec((1, SUBLANES, D), lambda b, t: (b, jnp.minimum((t + 1) * g8, last8), 0))
    otile = pl.BlockSpec((1, tm, E), lambda b, t: (b, t, 0))
    ztile = pl.BlockSpec((2, 1, tm, E), lambda b, t: (0, b, t, 0))
    one = jax.ShapeDtypeStruct((B, T, E), F32)
    two = jax.ShapeDtypeStruct((2, B, T, E), F32)
    return pl.pallas_call(
        _rwkv_in_kernel,
        grid=(B, T // tm),
        in_specs=[tile, prev8, next8, _const_spec((1, D)), _mod_spec(layer, row_of),
                  _const_spec((6, D)), _const_spec((D, 4 * E)), _const_spec((D, 4 * LORA)),
                  _const_spec((2 * LORA, E)), _const_spec((2 * LORA, E)),
                  _const_spec((2, E)), _const_spec((2, E)),
                  _const_spec((1, E)), _const_spec((1, E)), _const_spec((1, E)),
                  _const_spec((MXU_DIM, MXU_DIM))],
        out_specs=[otile] * 5 + [ztile] * 3,
        out_shape=[one] * 3 + [jax.ShapeDtypeStruct((B, T, E), BF16)] * 2 + [two] * 3,
        compiler_params=_cparams(("parallel", "parallel")),
        name="rwkv_in",
    )(x, x, x, nw.reshape(1, D), mods, p["mu"], p["w_in"], p["l1"], p["w2"], p["a2"],
      p["w0"], p["a0"], p["k_k"], p["k_a"], p["r_k"], p["seg"])


def _scan_kernel(*refs, has_init):
    n_in = 6
    ins = [refs[z * n_in:(z + 1) * n_in] for z in range(2)]
    rest = refs[2 * n_in:]
    if has_init:
        s0_refs, rest = rest[0:2], rest[2:]
    y_refs, sf_ref, s_scr = rest[0:2], rest[2], rest[3]
    C = SCAN_CHUNK
    PW = 2 * C
    HD = RWKV_HEAD_DIM
    NP = y_refs[0].shape[-1] // LANES
    chains = [(z, p) for z in range(2) for p in range(NP)]
    idx = range(len(chains))
    zs = [z for z, _ in chains]
    lo_h = lax.broadcasted_iota(jnp.int32, (HD, LANES), 1) < HD

    @pl.when(pl.program_id(1) == 0)
    def _():
        if not has_init:
            s_scr[...] = jnp.zeros_like(s_scr)
            return
        zero = jnp.zeros((HD, HD), F32)
        for z, p in chains:
            both = jnp.concatenate([jnp.concatenate([s0_refs[z][2 * p], zero], axis=1),
                                    jnp.concatenate([zero, s0_refs[z][2 * p + 1]], axis=1)], axis=0).T
            s_scr[z, p] = jnp.where(lo_h, both[0:HD], both[HD:2 * HD])

    state_diag = (lax.broadcasted_iota(jnp.int32, (HD, LANES), 0)
                  == lax.broadcasted_iota(jnp.int32, (HD, LANES), 1) % HD)
    tr = lax.broadcasted_iota(jnp.int32, (C, PW), 0)
    tc = lax.broadcasted_iota(jnp.int32, (C, PW), 1) % C
    eye = jnp.where(tr == tc, 1.0, 0.0)
    li = lax.broadcasted_iota(jnp.int32, (C, C), 0)
    lj = lax.broadcasted_iota(jnp.int32, (C, C), 1)
    strict, incl, cum_mat = [], [], []
    for z in range(2):
        before = (tc > tr) if z == 1 else (tc < tr)
        strict.append(before)
        incl.append(before | (tc == tr))
        cum_mat.append(jnp.where((lj >= li) if z == 1 else (lj <= li), 1.0, 0.0).astype(BF16))
    lo = lax.broadcasted_iota(jnp.int32, (C, LANES), 1) < HD
    zeros_c = jnp.zeros((C, LANES), F32)
    zeros_p = jnp.zeros((PW, LANES), F32)

    def heads_down(x):
        return jnp.concatenate([jnp.where(lo, x, 0.0), jnp.where(lo, 0.0, x)], axis=0)

    def load(z, which, p):
        ref = ins[z][which]
        sl = slice(p * LANES, (p + 1) * LANES)
        return ref[0, :, sl] if len(ref.shape) == 3 else ref[0, 0, :, sl]

    lw = [load(z, 3, p) for z, p in chains]
    cum = []
    for i in idx:
        hi, lo16 = _split2(lw[i])
        cum.append(_mm(cum_mat[zs[i]], hi) + _mm(cum_mat[zs[i]], lo16))
    tot = [(cum[i][0:1] if zs[i] == 1 else cum[i][C - 1:C]) for i in idx]
    abar, rbar, bh, kh, v, res = [], [], [], [], [], []
    for i, (z, p) in enumerate(chains):
        kk = load(z, 1, p)
        b = kk * load(z, 4, p)
        kz = load(z, 5, p)
        e_neg = jnp.exp(-cum[i])
        e_tot = jnp.exp(tot[i] - cum[i])
        abar.append(-kk * jnp.exp(cum[i] - lw[i]))
        rbar.append(load(z, 0, p) * jnp.exp(cum[i]))
        bt, kt = b * e_neg, kz * e_neg
        bh.append(b * e_tot)
        kh.append(kz * e_tot)
        v.append(load(z, 2, p))
        lhs = jnp.concatenate([abar[i], rbar[i]], axis=0)
        rhs = jnp.concatenate([heads_down(bt), heads_down(kt)], axis=0)
        res.append(_mm(lhs.astype(BF16), rhs.astype(BF16), DN_NT))
    a_ab = [jnp.where(strict[zs[i]], res[i][0:C, 0:PW], 0.0) for i in idx]
    a_ak = [jnp.where(strict[zs[i]], res[i][0:C, PW:2 * PW], 0.0) for i in idx]
    a_rb = [jnp.where(incl[zs[i]], res[i][C:PW, 0:PW], 0.0) for i in idx]
    a_rk = [jnp.where(incl[zs[i]], res[i][C:PW, PW:2 * PW], 0.0) for i in idx]
    vd = [heads_down(x) for x in v]
    wv = [_dot(a_ak[i], vd[i]) for i in idx]

    pw = [_dot(x, heads_down(x)) for x in a_ab]
    q = [eye + x for x in a_ab]
    n = 4
    while n < C:
        m = [_dot(pw[i], jnp.concatenate([heads_down(q[i]), heads_down(pw[i])], axis=1)) for i in idx]
        q = [q[i] + m[i][:, 0:PW] for i in idx]
        pw = [x[:, PW:2 * PW] for x in m]
        n *= 2
    q = [q[i] + _dot(pw[i], heads_down(q[i])) for i in idx]

    x = [_dot(q[i], jnp.concatenate([heads_down(abar[i]), heads_down(wv[i])], axis=1)) for i in idx]
    ahat = [t[:, 0:LANES] for t in x]
    u0 = [t[:, LANES:2 * LANES] for t in x]

    ry = [_dot(jnp.concatenate([a_rb[i], a_rk[i]], axis=1),
               jnp.concatenate([jnp.concatenate([heads_down(ahat[i]), heads_down(u0[i])], axis=1),
                                jnp.concatenate([zeros_p, vd[i]], axis=1)], axis=0)) for i in idx]
    rhat = [rbar[i] + ry[i][:, 0:LANES] for i in idx]
    y0 = [t[:, LANES:2 * LANES] for t in ry]

    mn = [_dot(jnp.concatenate([bh[i], kh[i]], axis=0).T,
               jnp.concatenate([jnp.concatenate([ahat[i], u0[i]], axis=1),
                                jnp.concatenate([zeros_c, v[i]], axis=1)], axis=0)) for i in idx]
    for i, (z, p) in enumerate(chains):
        m_c = jnp.where(lo, mn[i][0:HD, 0:LANES], mn[i][HD:2 * HD, 0:LANES]) + jnp.where(state_diag, jnp.exp(tot[i]), 0.0)
        n_c = jnp.where(lo, mn[i][0:HD, LANES:2 * LANES], mn[i][HD:2 * HD, LANES:2 * LANES])
        s_old = heads_down(s_scr[z, p])
        y_refs[z][0, :, p * LANES:(p + 1) * LANES] = _dot(rhat[i], s_old) + y0[i]
        s_scr[z, p] = _dot(m_c, s_old) + n_c

    @pl.when(pl.program_id(1) == pl.num_programs(1) - 1)
    def _():
        for z, p in chains:
            both = heads_down(s_scr[z, p]).T
            sf_ref[z, 2 * p] = both[0:HD, 0:HD]
            sf_ref[z, 2 * p + 1] = both[HD:2 * HD, HD:2 * HD]


def _rwkv_scan(r, kk, v, lw, a, kz, s0=None):
    B, T, E = r.shape
    NP = E // LANES
    C = SCAN_CHUNK
    nb = T // C
    HD = RWKV_HEAD_DIM
    H = E // HD
    assert C == HD
    in_specs, args, y_specs = [], [], []
    for z in range(2):
        tok = (lambda t: t) if z == 0 else (lambda t: nb - 1 - t)
        one = pl.BlockSpec((1, C, E), lambda b, t, tok=tok: (b, tok(t), 0))
        two = pl.BlockSpec((1, 1, C, E), lambda b, t, tok=tok, z=z: (z, b, tok(t), 0))
        in_specs += [one, one, one, two, two, two]
        args += [r, kk, v, lw, a, kz]
        y_specs.append(one)
    if s0 is not None:
        states, layer = s0
        for z in range(2):
            in_specs.append(pl.BlockSpec((None, None, None, H, HD, HD), lambda b, t, z=z: (b, layer, z, 0, 0, 0)))
            args.append(states)
    return pl.pallas_call(
        functools.partial(_scan_kernel, has_init=s0 is not None),
        grid=(B, nb),
        in_specs=in_specs,
        out_specs=y_specs + [pl.BlockSpec((None, 2, H, HD, HD), lambda b, t: (b, 0, 0, 0, 0))],
        out_shape=[jax.ShapeDtypeStruct((B, T, E), F32)] * 2 + [jax.ShapeDtypeStruct((B, 2, H, HD, HD), F32)],
        scratch_shapes=[pltpu.VMEM((2, NP, HD, LANES), F32)],
        compiler_params=_cparams(("parallel", "arbitrary")),
        name="rwkv_scan",
    )(*args)


def _rwkv_out_kernel(y0_ref, y1_ref, bonus_ref, sg_ref, lnw_ref, lnb_ref, seg_ref, w_ref, x_ref, mod_ref, o_ref):
    y = y0_ref[0] + y1_ref[0]
    inv_n = 1.0 / RWKV_HEAD_DIM
    d = y - _headsum(y, seg_ref) * inv_n
    var = _headsum(d * d, seg_ref) * inv_n
    yn = d * lax.rsqrt(var + RWKV_GN_EPS) * lnw_ref[...] + lnb_ref[...] + bonus_ref[0].astype(F32)
    o_ref[0] = _gated_residual(yn * sg_ref[0].astype(F32), w_ref, x_ref, mod_ref)


def _rwkv_out(y0, y1, bonus, sg, p, x, mods, layer, row_of):
    B, T, D = x.shape
    tm = min(ROW_TILE, T)
    tile = pl.BlockSpec((1, tm, D), lambda b, t: (b, t, 0))
    return pl.pallas_call(
        _rwkv_out_kernel,
        grid=(B, T // tm),
        in_specs=[tile, tile, tile, tile, _const_spec((1, D)), _const_spec((1, D)),
                  _const_spec((MXU_DIM, MXU_DIM)), _const_spec((D, D)), tile, _mod_spec(layer, row_of)],
        out_specs=tile,
        out_shape=jax.ShapeDtypeStruct((B, T, D), F32),
        compiler_params=_cparams(("parallel", "parallel")),
        name="rwkv_out",
    )(y0, y1, bonus, sg, p["ln_w"], p["ln_b"], p["seg"], p["w_out"], x, mods)


def _conv_in_kernel(x_ref, nw_ref, mod_ref, w_ref, z_ref, sg_ref):
    hb = _normmod(x_ref[0], nw_ref[...], mod_ref[...]).astype(BF16)
    E = D_INNER
    for n in range(0, E, N_CHUNK):
        a = _mm(hb, w_ref[:, n:n + N_CHUNK])
        b = _mm(hb, w_ref[:, E + n:E + n + N_CHUNK])
        g = _mm(hb, w_ref[:, 2 * E + n:2 * E + n + N_CHUNK])
        z_ref[0, :, n:n + N_CHUNK] = a * jax.nn.sigmoid(b)
        sg_ref[0, :, n:n + N_CHUNK] = _silu(g).astype(BF16)


def _conv_in(x, nw, mods, layer, row_of, w_bf16):
    B, T, D = x.shape
    E = D_INNER
    tm = min(ROW_TILE, T)
    tile = pl.BlockSpec((1, tm, D), lambda b, t: (b, t, 0))
    return pl.pallas_call(
        _conv_in_kernel,
        grid=(B, T // tm),
        in_specs=[tile, _const_spec((1, D)), _mod_spec(layer, row_of), _const_spec((D, 3 * E))],
        out_specs=[tile, tile],
        out_shape=[jax.ShapeDtypeStruct((B, T, E), F32), jax.ShapeDtypeStruct((B, T, E), BF16)],
        compiler_params=_cparams(("parallel", "parallel")),
        name="conv_in",
    )(x, nw.reshape(1, D), mods, w_bf16)


def _conv_out_kernel(z_ref, zp_ref, zn_ref, sg_ref, dw_ref, dwb_ref, lnw_ref, lnb_ref, w_ref, x_ref, mod_ref,
                     o_ref, zs, cbuf):
    t = pl.program_id(1)
    nt = pl.num_programs(1)
    tm = z_ref.shape[1]
    rows = tm + 2 * HALO
    zs[0, 0:HALO, :] = jnp.where(t > 0, zp_ref[0], 0.0)
    zs[0, HALO:HALO + tm, :] = z_ref[0]
    zs[0, HALO + tm:rows, :] = jnp.where(t < nt - 1, zn_ref[0], 0.0)
    for s in range(1, SUBLANES):
        zs[s, 0:rows - SUBLANES, :] = zs[0, s:s + rows - SUBLANES, :]
    off = HALO - CONV_PAD
    for c0 in range(0, D_INNER, LANES):
        for r0 in range(0, tm, CONV_ROWS):
            acc = None
            for k in range(CONV_WIDTH):
                q8, s = divmod(off + k, SUBLANES)
                a0 = r0 + q8 * SUBLANES
                term = dw_ref[k:k + 1, c0:c0 + LANES] * zs[s, a0:a0 + CONV_ROWS, c0:c0 + LANES]
                acc = term if acc is None else acc + term
            cbuf[r0:r0 + CONV_ROWS, c0:c0 + LANES] = acc
    c = cbuf[...] + dwb_ref[...]
    m = jnp.mean(c, axis=-1, keepdims=True)
    d = c - m
    var = jnp.mean(d * d, axis=-1, keepdims=True)
    y = d * lax.rsqrt(var + 1e-5) * lnw_ref[...] + lnb_ref[...]
    o_ref[0] = _gated_residual(_silu(y) * sg_ref[0].astype(F32), w_ref, x_ref, mod_ref)


def _conv_out(z, sg, p, x, mods, layer, row_of):
    B, T, D = x.shape
    tm = min(CONV_ROW_TILE, T)
    gh = tm // HALO
    lasth = T // HALO - 1
    tile = pl.BlockSpec((1, tm, D), lambda b, t: (b, t, 0))
    prev = pl.BlockSpec((1, HALO, D), lambda b, t: (b, jnp.maximum(t * gh - 1, 0), 0))
    nxt = pl.BlockSpec((1, HALO, D), lambda b, t: (b, jnp.minimum((t + 1) * gh, lasth), 0))
    return pl.pallas_call(
        _conv_out_kernel,
        grid=(B, T // tm),
        in_specs=[tile, prev, nxt, tile, _const_spec((CONV_WIDTH, D)), _const_spec((1, D)),
                  _const_spec((1, D)), _const_spec((1, D)), _const_spec((D, D)), tile,
                  _mod_spec(layer, row_of)],
        out_specs=tile,
        out_shape=jax.ShapeDtypeStruct((B, T, D), F32),
        scratch_shapes=[pltpu.VMEM((SUBLANES, tm + 2 * HALO, D), F32), pltpu.VMEM((tm, D), F32)],
        compiler_params=_cparams(("parallel", "parallel")),
        name="conv_out",
    )(z, z, z, sg, p["dw_w"], p["dw_b"], p["ln_w"], p["ln_b"], p["w_out"], x, mods)


def _rope_tables(n_tok):
    t = jnp.arange(n_tok, dtype=jnp.int32)
    pos = jnp.stack([(t // GRID_W).astype(F32), (t % GRID_W).astype(F32)], axis=1)
    lane = jnp.arange(LANES, dtype=jnp.int32)
    d = lane % ATTN_HEAD_DIM
    axis = d // 32
    second_half = (d % 32) // 16
    axis_dim = ATTN_HEAD_DIM // 2
    inv_freq = ROPE_BASE ** (-(2.0 * (d % 16).astype(F32)) / axis_dim)
    ang = pos[:, axis] * inv_freq[None, :]
    sign = jnp.where(second_half == 1, 1.0, -1.0).astype(F32)
    return jnp.cos(ang), jnp.sin(ang) * sign[None, :]


def kernel(x_prompt, x_sample, cache_attn_k, cache_attn_v, state_rwkv, c, c_ctx, norm_w, ada_w, ada_b, attn_w_in, attn_lambda, attn_subln_w, attn_w_out, rwkv_mu, rwkv_w_in, rwkv_w0, rwkv_w1, rwkv_w2, rwkv_a0, rwkv_a1, rwkv_a2, rwkv_k_k, rwkv_k_a, rwkv_r_k, rwkv_ln_w, rwkv_ln_b, rwkv_w_out, conv_w_in, conv_dw_w, conv_dw_b, conv_ln_w, conv_ln_b, conv_w_out, final_norm_w):
    D, E = D_MODEL, D_INNER
    dec_batch = x_sample.shape[0]
    assert dec_batch < COND_ROWS
    cond = jnp.concatenate([c, c_ctx[None, :], jnp.zeros((COND_ROWS - dec_batch - 1, D), F32)], axis=0)
    mods = _ada(cond, ada_w, ada_b)
    ctx_row = dec_batch
    streams = [(x_prompt, lambda b: ctx_row), (x_sample, lambda b: b)]
    cos, sin = _rope_tables(x_sample.shape[1])
    seg = (jnp.arange(MXU_DIM)[:, None] // RWKV_HEAD_DIM == jnp.arange(MXU_DIM)[None, :] // RWKV_HEAD_DIM).astype(BF16)

    xs = [x_prompt, x_sample]

    def rows(a, s):
        B, T, D = a.shape
        if s == 0 and T < ROW_TILE and (B * T) % ROW_TILE == 0:
            return a.reshape(B * T // ROW_TILE, ROW_TILE, D)
        return a
    new_kv, new_s = (None, None), []
    assert (DEPTH - 1) % N_MIXERS == 0
    for i in range(DEPTH):
        kind, j = i % N_MIXERS, i // N_MIXERS
        last = i == DEPTH - 1
        fw = final_norm_w if last else None
        if kind == 0:
            w_in = attn_w_in[j].astype(BF16)
            w_out = attn_w_out[j].astype(BF16)
            for s, (_, row_of) in enumerate(streams):
                if s == 0:
                    qkvg, *new_kv = _attn_in(xs[s], norm_w[i], mods, i, row_of, w_in, cache=new_kv)
                    o = _attn_core(qkvg, attn_lambda[j], attn_subln_w[j], i)
                else:
                    (qkvg,) = _attn_in(xs[s], norm_w[i], mods, i, row_of, w_in, rope=(cos, sin))
                    o = _attn_core(qkvg, attn_lambda[j], attn_subln_w[j], i, ctx=(cache_attn_k, cache_attn_v, j))
                xs[s] = _out_proj(rows(o, s), w_out, rows(xs[s], s), mods, i, row_of, final_w=fw).reshape(xs[s].shape)
        elif kind == 1:
            p = dict(
                mu=rwkv_mu[j],
                w_in=rwkv_w_in[j].astype(BF16),
                l1=jnp.concatenate([rwkv_w1[j, 0], rwkv_w1[j, 1], rwkv_a1[j, 0], rwkv_a1[j, 1]], axis=1).astype(BF16),
                w2=rwkv_w2[j].reshape(2 * LORA, E).astype(BF16),
                a2=rwkv_a2[j].reshape(2 * LORA, E).astype(BF16),
                w0=rwkv_w0[j], a0=rwkv_a0[j],
                k_k=rwkv_k_k[j].reshape(1, E), k_a=rwkv_k_a[j].reshape(1, E), r_k=rwkv_r_k[j].reshape(1, E),
                ln_w=rwkv_ln_w[j].reshape(1, E), ln_b=rwkv_ln_b[j].reshape(1, E),
                w_out=rwkv_w_out[j].astype(BF16), seg=seg)
            for s, (_, row_of) in enumerate(streams):
                r, kk, v, sg, bonus, lw, a, kz = _rwkv_in(xs[s], norm_w[i], mods, i, row_of, p)
                y_f, y_b, final_states = _rwkv_scan(r, kk, v, lw, a, kz, (state_rwkv, j) if s == 1 else None)
                if s == 0:
                    new_s.append(final_states)
                xs[s] = _rwkv_out(rows(y_f, s), rows(y_b, s), rows(bonus, s), rows(sg, s), p, rows(xs[s], s),
                                  mods, i, row_of).reshape(xs[s].shape)
        else:
            p = dict(dw_w=conv_dw_w[j], dw_b=conv_dw_b[j].reshape(1, E), ln_w=conv_ln_w[j].reshape(1, E),
                     ln_b=conv_ln_b[j].reshape(1, E), w_out=conv_w_out[j].astype(BF16))
            w_in = conv_w_in[j].astype(BF16)
            for s, (_, row_of) in enumerate(streams):
                zz, sg = _conv_in(rows(xs[s], s), norm_w[i], mods, i, row_of, w_in)
                xs[s] = _conv_out(zz.reshape(xs[s].shape), sg.reshape(xs[s].shape), p, xs[s], mods, i, row_of)
    return (xs[0], xs[1], new_kv[0], new_kv[1], jnp.stack(new_s, axis=1))
```

```python
import functools
import math

import jax
import jax.numpy as jnp
from jax import lax
from jax.experimental import pallas as pl
from jax.experimental.pallas import tpu as pltpu

F32 = jnp.float32
BF16 = jnp.bfloat16

D_MODEL = 1024
D_INNER = 1024
DEPTH = 4
N_MIXERS = 3
GRID_W = 64
NORM_EPS = 1e-6
ATTN_HEAD_DIM = 64
ATTN_HEADS = 8
ATTN_V_DIM = 128
ROPE_BASE = 10000.0
RWKV_HEAD_DIM = 64
RWKV_HEADS = 16
RWKV_GN_EPS = RWKV_HEAD_DIM * 1e-5
LORA = 64
CONV_WIDTH = 31
CONV_PAD = CONV_WIDTH // 2

LANES = 128
SUBLANES = 8
MXU_DIM = 256
VMEM_LIMIT_BYTES = 56 * 1024 * 1024

ROW_TILE = 512
CONV_ROW_TILE = 256
RWKV_ROW_TILE = 256
ATTN_Q_TILE = 256
ATTN_ROWS_PER_STEP = 4096
ATTN_TASK_GROUP = 2
N_CHUNK = 512
SCAN_CHUNK = 64
HALO = 16
CONV_ROWS = 64
COND_ROWS = 16
DN = (((1,), (0,)), ((), ()))
DN_NT = (((1,), (1,)), ((), ()))


def _cparams(sem):
    return pltpu.CompilerParams(dimension_semantics=sem, vmem_limit_bytes=VMEM_LIMIT_BYTES)


def _mm(a, b, dn=DN):
    return lax.dot_general(a, b, dn, preferred_element_type=F32)


def _dot(a, b):
    return _mm(a.astype(BF16), b.astype(BF16))


def _split2(x):
    hi = x.astype(BF16)
    lo = (x - hi.astype(F32)).astype(BF16)
    return hi, lo


def _dot3(a, b):
    a_hi, a_lo = _split2(a)
    b_hi, b_lo = _split2(b)
    return _mm(a_hi, b_hi) + (_mm(a_lo, b_hi) + _mm(a_hi, b_lo))


def _silu(x):
    return x * jax.nn.sigmoid(x)


def _normmod(x, nw, mod):
    ms = jnp.mean(x * x, axis=-1, keepdims=True)
    y = x * lax.rsqrt(ms + NORM_EPS) * nw
    return y * (1.0 + mod[:, D_MODEL:2 * D_MODEL]) + mod[:, 0:D_MODEL]


def _headsum(x, seg_ref):
    cols = []
    for c in range(0, x.shape[-1], MXU_DIM):
        hi, lo = _split2(x[:, c:c + MXU_DIM])
        cols.append(_mm(hi, seg_ref[...]) + _mm(lo, seg_ref[...]))
    return jnp.concatenate(cols, axis=-1)


def _mod_spec(layer, row_of):
    return pl.BlockSpec((None, None, 1, 3 * D_MODEL), lambda b, t: (layer, row_of(b), 0, 0))


def _const_spec(shape):
    return pl.BlockSpec(shape, lambda b, t: (0,) * len(shape))


def _ada_kernel(cond_ref, w_ref, b_ref, o_ref):
    o_ref[0, :, 0, :] = _dot3(_silu(cond_ref[...]), w_ref[0]) + b_ref[0]


def _ada(cond, ada_w, ada_b):
    depth, d, n3 = ada_w.shape
    tn = 1024
    return pl.pallas_call(
        _ada_kernel,
        grid=(depth, n3 // tn),
        in_specs=[
            pl.BlockSpec((COND_ROWS, d), lambda i, j: (0, 0)),
            pl.BlockSpec((1, d, tn), lambda i, j: (i, 0, j)),
            pl.BlockSpec((1, 1, tn), lambda i, j: (i, 0, j)),
        ],
        out_specs=pl.BlockSpec((1, COND_ROWS, 1, tn), lambda i, j: (i, 0, 0, j)),
        out_shape=jax.ShapeDtypeStruct((depth, COND_ROWS, 1, n3), F32),
        compiler_params=_cparams(("parallel", "parallel")),
        name="ada_mod",
    )(cond, ada_w, ada_b.reshape(depth, 1, n3))


ATTN_Q_SCALE = ATTN_HEAD_DIM ** -0.5 * math.log2(math.e)


def _attn_in_kernel(x_ref, nw_ref, mod_ref, w_ref, *rest, rope, n_cached):
    rest = list(rest)
    if rope:
        cos_ref, sin_ref = rest.pop(0), rest.pop(0)
    if n_cached:
        pk_ref, pv_ref = rest.pop(0), rest.pop(0)
    o_ref = rest.pop(0)
    if n_cached is not None:
        ko_ref, vo_ref = rest
        if n_cached:
            ko_ref[0:n_cached] = pk_ref[...]
            vo_ref[0:n_cached] = pv_ref[...]
    E = D_INNER
    hb = _normmod(x_ref[0], nw_ref[...], mod_ref[...]).astype(BF16)
    for n in range(0, 4 * E, N_CHUNK):
        y = _mm(hb, w_ref[:, n:n + N_CHUNK])
        if n_cached is not None and E <= n < 3 * E:
            cache_ref = ko_ref if n < 2 * E else vo_ref
            for c in range(0, N_CHUNK, LANES):
                cache_ref[n_cached, (n % E + c) // LANES] = y[:, c:c + LANES]
        if n < E:
            y = y * ATTN_Q_SCALE
        if rope and n < 2 * E:
            y = jnp.concatenate([_rope(y[:, c:c + LANES], cos_ref[...], sin_ref[...])
                                 for c in range(0, N_CHUNK, LANES)], axis=1)
        o_ref[0, :, n:n + N_CHUNK] = y.astype(BF16)


def _attn_in(x, nw, mods, layer, row_of, w_bf16, rope=None, cache=None):
    B, T, D = x.shape
    N = w_bf16.shape[1]
    H, DV = ATTN_HEADS, ATTN_V_DIM
    tm = min(ROW_TILE, T)
    in_specs = [pl.BlockSpec((1, tm, D), lambda b, t: (b, t, 0)), _const_spec((1, D)),
                _mod_spec(layer, row_of), _const_spec((D, N))]
    args = [x, nw.reshape(1, D), mods, w_bf16]
    if rope is not None:
        in_specs += [pl.BlockSpec((tm, LANES), lambda b, t: (t, 0))] * 2
        args += list(rope)
    out_specs = [pl.BlockSpec((1, tm, N), lambda b, t: (b, t, 0))]
    out_shape = [jax.ShapeDtypeStruct((B, T, N), BF16)]
    n_cached = None
    if cache is not None:
        n_cached = 0 if cache[0] is None else cache[0].shape[1]
        if n_cached:
            in_specs += [pl.BlockSpec((None, n_cached, H, tm, DV), lambda b, t: (b, 0, 0, t, 0))] * 2
            args += list(cache)
        out_specs += [pl.BlockSpec((None, n_cached + 1, H, tm, DV), lambda b, t: (b, 0, 0, t, 0))] * 2
        out_shape += [jax.ShapeDtypeStruct((B, n_cached + 1, H, T, DV), F32)] * 2
    return pl.pallas_call(
        functools.partial(_attn_in_kernel, rope=rope is not None, n_cached=n_cached),
        grid=(B, T // tm),
        in_specs=in_specs,
        out_specs=out_specs,
        out_shape=out_shape,
        compiler_params=_cparams(("parallel", "parallel")),
        name="attn_in",
    )(*args)


def _rope(x, cos, sin):
    lane = lax.broadcasted_iota(jnp.int32, x.shape, 1)
    first = (lane % 32) < 16
    partner = jnp.where(first, pltpu.roll(x, LANES - 16, axis=1), pltpu.roll(x, 16, axis=1))
    return x * cos + partner * sin


def _attn_core_kernel(*refs, lam_init, has_ctx, tq, group):
    it = iter(refs)
    lam_ref, q_ref, k_ref, v_ref, g_ref, subw_ref = (next(it) for _ in range(6))
    if has_ctx:
        ck_ref, cv_ref = next(it), next(it)
    o_ref = next(it)
    T = q_ref.shape[1]
    n_heads = q_ref.shape[2] // LANES
    n_tiles = T // tq
    k_scr = [next(it) for _ in range(n_heads)]
    v_scr = [next(it) for _ in range(n_heads)]
    ring = 2 * group
    s_scr = [[next(it), next(it)] for _ in range(ring)]
    e_scr = [[next(it), next(it)] for _ in range(ring)]
    m_scr = [[next(it), next(it)] for _ in range(ring)]

    lv = lam_ref[...]
    lam = (jnp.exp(jnp.sum(lv[0:1] * lv[1:2], axis=-1, keepdims=True))
           - jnp.exp(jnp.sum(lv[2:3] * lv[3:4], axis=-1, keepdims=True)) + lam_init)

    tk = k_scr[0].shape[0]
    for hh in range(n_heads):
        hl = slice(hh * LANES, (hh + 1) * LANES)
        if has_ctx:
            k_scr[hh][T:, :] = ck_ref[hh].astype(BF16)
            v_scr[hh][T:, 0:LANES] = cv_ref[hh].astype(BF16)
        k_scr[hh][0:T, :] = k_ref[0, :, hl]
        v_scr[hh][0:T, 0:LANES] = v_ref[0, :, hl]
        v_scr[hh][:, LANES:2 * LANES] = jnp.ones((tk, LANES), BF16)

    kblocks = [slice(c, c + MXU_DIM) for c in range(0, tk, MXU_DIM)]
    tasks = [(hh, i) for hh in range(n_heads) for i in range(n_tiles)]

    def halves_max(x):
        return jnp.maximum(x[:, 0:LANES], x[:, LANES:2 * LANES])

    def score_stage(n):
        hh, i = tasks[n]
        q = q_ref[0, i * tq:(i + 1) * tq, hh * LANES:(hh + 1) * LANES]
        lane = lax.broadcasted_iota(jnp.int32, q.shape, 1)
        for mp in range(2):
            sel = (lane < ATTN_HEAD_DIM) if mp == 0 else (lane >= ATTN_HEAD_DIM)
            qm = jnp.where(sel, q, jnp.zeros_like(q))
            mx = None
            for kb in kblocks:
                s = _mm(qm, k_scr[hh][kb, :], DN_NT)
                s_scr[n % ring][mp][:, kb] = s
                mx = halves_max(s) if mx is None else jnp.maximum(mx, halves_max(s))
                yield
            m_scr[n % ring][mp][...] = jnp.broadcast_to(jnp.max(mx, axis=-1, keepdims=True), mx.shape)

    def attend_stage(n):
        hh, i = tasks[n]
        outs = []
        for mp in range(2):
            m = m_scr[n % ring][mp][...]
            m = jnp.concatenate([m, m], axis=1)
            for kb in kblocks:
                e_scr[n % ring][mp][:, kb] = jnp.exp2(s_scr[n % ring][mp][:, kb] - m).astype(BF16)
                yield
            ov = _mm(e_scr[n % ring][mp][...], v_scr[hh][...])
            outs.append((ov[:, 0:LANES], ov[:, LANES:2 * LANES]))
            yield
        (o1, d1), (o2, d2) = outs
        o = o1 * (1.0 / d1) - o2 * (lam / d2)
        ms = jnp.mean(o * o, axis=-1, keepdims=True)
        o = o * lax.rsqrt(ms + 1e-5) * subw_ref[...] * (1.0 - lam_init)
        rows, hl = slice(i * tq, (i + 1) * tq), slice(hh * LANES, (hh + 1) * LANES)
        o_ref[0, rows, hl] = (o * _silu(g_ref[0, rows, hl].astype(F32))).astype(BF16)

    n_groups = len(tasks) // group
    for t in range(n_groups + 1):
        live = []
        if t < n_groups:
            live += [score_stage(t * group + u) for u in range(group)]
        if t >= 1:
            live += [attend_stage((t - 1) * group + u) for u in range(group)]
        while live:
            for gen in list(live):
                if next(gen, "done") == "done":
                    live.remove(gen)


def _attn_core(qkvg, lam_vecs, subw, layer_idx, ctx=None):
    B, T, _ = qkvg.shape
    H, DV = ATTN_HEADS, ATTN_V_DIM
    HS = min(H, max(1, ATTN_ROWS_PER_STEP // T))
    lam_init = 0.8 - 0.6 * math.exp(-0.3 * layer_idx)
    has_ctx = ctx is not None
    tq = min(ATTN_Q_TILE, T)
    tk = T + (ctx[0].shape[3] if has_ctx else 0)

    def col(off):
        return pl.BlockSpec((1, T, HS * DV), lambda b, h: (b, 0, off + h))

    nb = H // HS
    in_specs = [pl.BlockSpec((4, ATTN_HEAD_DIM), lambda b, h: (0, 0)),
                col(0), col(nb), col(2 * nb), col(3 * nb),
                pl.BlockSpec((1, DV), lambda b, h: (0, 0))]
    args = [lam_vecs, qkvg, qkvg, qkvg, qkvg, subw.reshape(1, DV)]
    if has_ctx:
        ck, cv, j = ctx
        P = ck.shape[3]
        cache_spec = pl.BlockSpec((None, None, HS, P, DV), lambda b, h: (b, j, h, 0, 0))
        in_specs += [cache_spec, cache_spec]
        args += [ck, cv]
    group = ATTN_TASK_GROUP if tk // MXU_DIM >= 2 * ATTN_TASK_GROUP else 1
    assert (HS * (T // tq)) % group == 0
    n_buf = 2 * group * 2
    return pl.pallas_call(
        functools.partial(_attn_core_kernel, lam_init=lam_init, has_ctx=has_ctx, tq=tq, group=group),
        grid=(B, nb),
        in_specs=in_specs,
        out_specs=pl.BlockSpec((1, T, HS * DV), lambda b, h: (b, 0, h)),
        out_shape=jax.ShapeDtypeStruct((B, T, D_INNER), BF16),
        scratch_shapes=([pltpu.VMEM((tk, DV), BF16)] * HS + [pltpu.VMEM((tk, 2 * DV), BF16)] * HS
                        + [pltpu.VMEM((tq, tk), F32)] * n_buf + [pltpu.VMEM((tq, tk), BF16)] * n_buf
                        + [pltpu.VMEM((tq, LANES), F32)] * n_buf),
        compiler_params=_cparams(("parallel", "parallel")),
        name="attn_core",
    )(*args)


def _gated_residual(o, w_ref, x_ref, mod_ref):
    return x_ref[0] + mod_ref[:, 2 * D_MODEL:3 * D_MODEL] * _dot(o, w_ref[...])


def _out_proj_kernel(o_ref, w_ref, x_ref, mod_ref, *rest, final_norm):
    xn = _gated_residual(o_ref[0], w_ref, x_ref, mod_ref)
    if final_norm:
        fw_ref, y_ref = rest
        ms = jnp.mean(xn * xn, axis=-1, keepdims=True)
        xn = xn * lax.rsqrt(ms + NORM_EPS) * fw_ref[...]
    else:
        (y_ref,) = rest
    y_ref[0] = xn


def _out_proj(o, w_bf16, x, mods, layer, row_of, final_w=None):
    B, T, D = x.shape
    tm = min(ROW_TILE, T)
    tile = pl.BlockSpec((1, tm, D), lambda b, t: (b, t, 0))
    in_specs = [tile, _const_spec((D, D)), tile, _mod_spec(layer, row_of)]
    args = [o, w_bf16, x, mods]
    if final_w is not None:
        in_specs.append(_const_spec((1, D)))
        args.append(final_w.reshape(1, D))
    return pl.pallas_call(
        functools.partial(_out_proj_kernel, final_norm=final_w is not None),
        grid=(B, T // tm),
        in_specs=in_specs,
        out_specs=tile,
        out_shape=jax.ShapeDtypeStruct((B, T, D), F32),
        compiler_params=_cparams(("parallel", "parallel")),
        name="out_proj",
    )(*args)


def _rwkv_in_kernel(x_ref, xp_ref, xn_ref, nw_ref, mod_ref, mu_ref, w_ref, l1_ref, w2_ref, a2_ref,
                    w0_ref, a0_ref, kk_ref, ka_ref, rk_ref, seg_ref,
                    r_out, kk_out, v_out, sg_out, bonus_out, lw_out, a_out, kz_out):
    t = pl.program_id(1)
    nt = pl.num_programs(1)
    nw = nw_ref[...]
    mod = mod_ref[...]
    h = _normmod(x_ref[0], nw, mod)
    tm = h.shape[0]
    hp = _normmod(xp_ref[0], nw, mod)[SUBLANES - 1:SUBLANES, :]
    hn = _normmod(xn_ref[0], nw, mod)[0:1, :]
    hp = jnp.where(t > 0, hp, 0.0)
    hn = jnp.where(t < nt - 1, hn, 0.0)
    row = lax.broadcasted_iota(jnp.int32, h.shape, 0)
    h_prev = jnp.where(row == 0, hp, pltpu.roll(h, 1, axis=0))
    h_next = jnp.where(row == tm - 1, hn, pltpu.roll(h, tm - 1, axis=0))
    dx = 0.5 * (h_prev + h_next) - h

    def mix(n):
        return (h + dx * mu_ref[n:n + 1, :]).astype(BF16)

    E = D_INNER
    r = _mm(mix(0), w_ref[:, 0:E])
    k = _mm(mix(1), w_ref[:, E:2 * E])
    v = _mm(mix(2), w_ref[:, 2 * E:3 * E])
    g = _mm(mix(3), w_ref[:, 3 * E:4 * E])
    lw1 = jnp.tanh(_mm(mix(4), l1_ref[:, 0:2 * LORA]))
    la1 = _mm(mix(5), l1_ref[:, 2 * LORA:4 * LORA])
    lane = lax.broadcasted_iota(jnp.int32, lw1.shape, 1)

    kk = k * kk_ref[...]
    kk = kk * lax.rsqrt(jnp.maximum(_headsum(kk * kk, seg_ref), 1e-24))
    r_out[0] = r
    kk_out[0] = kk
    v_out[0] = v
    sg_out[0] = _silu(g).astype(BF16)

    ksum = None
    for z in range(2):
        sel = (lane >= z * LORA) & (lane < (z + 1) * LORA)
        lw = _dot(jnp.where(sel, lw1, 0.0), w2_ref[...])
        la = _dot(jnp.where(sel, la1, 0.0), a2_ref[...])
        lw_out[z, 0] = -math.exp(-0.5) * jax.nn.sigmoid(w0_ref[z:z + 1, :] + lw)
        a = jax.nn.sigmoid(a0_ref[z:z + 1, :] + la)
        a_out[z, 0] = a
        kz = k * (1.0 + (a - 1.0) * ka_ref[...])
        kz_out[z, 0] = kz
        ksum = kz if ksum is None else ksum + kz
    bonus_out[0] = (_headsum(r * rk_ref[...] * ksum, seg_ref) * v).astype(BF16)


def _rwkv_in(x, nw, mods, layer, row_of, p):
    B, T, D = x.shape
    E = D_INNER
    tm = min(RWKV_ROW_TILE, T)
    g8 = tm // SUBLANES
    last8 = T // SUBLANES - 1
    tile = pl.BlockSpec((1, tm, D), lambda b, t: (b, t, 0))
    prev8 = pl.BlockSpec((1, SUBLANES, D), lambda b, t: (b, jnp.maximum(t * g8 - 1, 0), 0))
    next8 = pl.BlockSpec((1, SUBLANES, D), lambda b, t: (b, jnp.minimum((t + 1) * g8, last8), 0))
    otile = pl.BlockSpec((1, tm, E), lambda b, t: (b, t, 0))
    ztile = pl.BlockSpec((2, 1, tm, E), lambda b, t: (0, b, t, 0))
    one = jax.ShapeDtypeStruct((B, T, E), F32)
    two = jax.ShapeDtypeStruct((2, B, T, E), F32)
    return pl.pallas_call(
        _rwkv_in_kernel,
        grid=(B, T // tm),
        in_specs=[tile, prev8, next8, _const_spec((1, D)), _mod_spec(layer, row_of),
                  _const_spec((6, D)), _const_spec((D, 4 * E)), _const_spec((D, 4 * LORA)),
                  _const_spec((2 * LORA, E)), _const_spec((2 * LORA, E)),
                  _const_spec((2, E)), _const_spec((2, E)),
                  _const_spec((1, E)), _const_spec((1, E)), _const_spec((1, E)),
                  _const_spec((MXU_DIM, MXU_DIM))],
        out_specs=[otile] * 5 + [ztile] * 3,
        out_shape=[one] * 3 + [jax.ShapeDtypeStruct((B, T, E), BF16)] * 2 + [two] * 3,
        compiler_params=_cparams(("parallel", "parallel")),
        name="rwkv_in",
    )(x, x, x, nw.reshape(1, D), mods, p["mu"], p["w_in"], p["l1"], p["w2"], p["a2"],
      p["w0"], p["a0"], p["k_k"], p["k_a"], p["r_k"], p["seg"])


def _scan_kernel(*refs, has_init):
    n_in = 6
    ins = [refs[z * n_in:(z + 1) * n_in] for z in range(2)]
    rest = refs[2 * n_in:]
    if has_init:
        s0_refs, rest = rest[0:2], rest[2:]
    y_refs, sf_ref, s_scr = rest[0:2], rest[2], rest[3]
    C = SCAN_CHUNK
    PW = 2 * C
    HD = RWKV_HEAD_DIM
    NP = y_refs[0].shape[-1] // LANES
    chains = [(z, p) for z in range(2) for p in range(NP)]
    idx = range(len(chains))
    zs = [z for z, _ in chains]
    lo_h = lax.broadcasted_iota(jnp.int32, (HD, LANES), 1) < HD

    @pl.when(pl.program_id(1) == 0)
    def _():
        if not has_init:
            s_scr[...] = jnp.zeros_like(s_scr)
            return
        zero = jnp.zeros((HD, HD), F32)
        for z, p in chains:
            both = jnp.concatenate([jnp.concatenate([s0_refs[z][2 * p], zero], axis=1),
                                    jnp.concatenate([zero, s0_refs[z][2 * p + 1]], axis=1)], axis=0).T
            s_scr[z, p] = jnp.where(lo_h, both[0:HD], both[HD:2 * HD])

    state_diag = (lax.broadcasted_iota(jnp.int32, (HD, LANES), 0)
                  == lax.broadcasted_iota(jnp.int32, (HD, LANES), 1) % HD)
    tr = lax.broadcasted_iota(jnp.int32, (C, PW), 0)
    tc = lax.broadcasted_iota(jnp.int32, (C, PW), 1) % C
    eye = jnp.where(tr == tc, 1.0, 0.0)
    strict, incl = [], []
    for z in range(2):
        before = (tc > tr) if z == 1 else (tc < tr)
        strict.append(before)
        incl.append(before | (tc == tr))
    lo = lax.broadcasted_iota(jnp.int32, (C, LANES), 1) < HD
    zeros_c = jnp.zeros((C, LANES), F32)
    zeros_p = jnp.zeros((PW, LANES), F32)

    def heads_down(x):
        return jnp.concatenate([jnp.where(lo, x, 0.0), jnp.where(lo, 0.0, x)], axis=0)

    def load(z, which, p):
        ref = ins[z][which]
        sl = slice(p * LANES, (p + 1) * LANES)
        return ref[0, :, sl] if len(ref.shape) == 3 else ref[0, 0, :, sl]

    row = lax.broadcasted_iota(jnp.int32, (C, LANES), 0)

    def running_sum(x, z):
        s = 1
        while s < C:
            if z == 0:
                x = x + jnp.where(row >= s, pltpu.roll(x, s, axis=0), 0.0)
            else:
                x = x + jnp.where(row < C - s, pltpu.roll(x, C - s, axis=0), 0.0)
            s *= 2
        return x

    lw = [load(z, 3, p) for z, p in chains]
    cum = [running_sum(lw[i], zs[i]) for i in idx]
    tot = [(cum[i][0:1] if zs[i] == 1 else cum[i][C - 1:C]) for i in idx]
    abar, rbar, bh, kh, v, res = [], [], [], [], [], []
    for i, (z, p) in enumerate(chains):
        kk = load(z, 1, p)
        b = kk * load(z, 4, p)
        kz = load(z, 5, p)
        e_neg = jnp.exp(-cum[i])
        e_tot = jnp.exp(tot[i] - cum[i])
        abar.append(-kk * jnp.exp(cum[i] - lw[i]))
        rbar.append(load(z, 0, p) * jnp.exp(cum[i]))
        bt, kt = b * e_neg, kz * e_neg
        bh.append(b * e_tot)
        kh.append(kz * e_tot)
        v.append(load(z, 2, p))
        lhs = jnp.concatenate([abar[i], rbar[i]], axis=0)
        rhs = jnp.concatenate([heads_down(bt), heads_down(kt)], axis=0)
        res.append(_mm(lhs.astype(BF16), rhs.astype(BF16), DN_NT))
    a_ab = [jnp.where(strict[zs[i]], res[i][0:C, 0:PW], 0.0) for i in idx]
    a_ak = [jnp.where(strict[zs[i]], res[i][0:C, PW:2 * PW], 0.0) for i in idx]
    a_rb = [jnp.where(incl[zs[i]], res[i][C:PW, 0:PW], 0.0) for i in idx]
    a_rk = [jnp.where(incl[zs[i]], res[i][C:PW, PW:2 * PW], 0.0) for i in idx]
    vd = [heads_down(x) for x in v]
    wv = [_dot(a_ak[i], vd[i]) for i in idx]

    pw = [_dot(x, heads_down(x)) for x in a_ab]
    q = [eye + x for x in a_ab]
    n = 4
    while n < C:
        m = [_dot(pw[i], jnp.concatenate([heads_down(q[i]), heads_down(pw[i])], axis=1)) for i in idx]
        q = [q[i] + m[i][:, 0:PW] for i in idx]
        pw = [x[:, PW:2 * PW] for x in m]
        n *= 2
    q = [q[i] + _dot(pw[i], heads_down(q[i])) for i in idx]

    x = [_dot(q[i], jnp.concatenate([heads_down(abar[i]), heads_down(wv[i])], axis=1)) for i in idx]
    ahat = [t[:, 0:LANES] for t in x]
    u0 = [t[:, LANES:2 * LANES] for t in x]

    ry = [_dot(jnp.concatenate([a_rb[i], a_rk[i]], axis=1),
               jnp.concatenate([jnp.concatenate([heads_down(ahat[i]), heads_down(u0[i])], axis=1),
                                jnp.concatenate([zeros_p, vd[i]], axis=1)], axis=0)) for i in idx]
    rhat = [rbar[i] + ry[i][:, 0:LANES] for i in idx]
    y0 = [t[:, LANES:2 * LANES] for t in ry]

    mn = [_dot(jnp.concatenate([bh[i], kh[i]], axis=0).T,
               jnp.concatenate([jnp.concatenate([ahat[i], u0[i]], axis=1),
                                jnp.concatenate([zeros_c, v[i]], axis=1)], axis=0)) for i in idx]
    for i, (z, p) in enumerate(chains):
        m_c = jnp.where(lo, mn[i][0:HD, 0:LANES], mn[i][HD:2 * HD, 0:LANES]) + jnp.where(state_diag, jnp.exp(tot[i]), 0.0)
        n_c = jnp.where(lo, mn[i][0:HD, LANES:2 * LANES], mn[i][HD:2 * HD, LANES:2 * LANES])
        s_old = heads_down(s_scr[z, p])
        y_refs[z][0, :, p * LANES:(p + 1) * LANES] = _dot(rhat[i], s_old) + y0[i]
        s_scr[z, p] = _dot(m_c, s_old) + n_c

    @pl.when(pl.program_id(1) == pl.num_programs(1) - 1)
    def _():
        for z, p in chains:
            both = heads_down(s_scr[z, p]).T
            sf_ref[z, 2 * p] = both[0:HD, 0:HD]
            sf_ref[z, 2 * p + 1] = both[HD:2 * HD, HD:2 * HD]


def _rwkv_scan(r, kk, v, lw, a, kz, s0=None):
    B, T, E = r.shape
    NP = E // LANES
    C = SCAN_CHUNK
    nb = T // C
    HD = RWKV_HEAD_DIM
    H = E // HD
    assert C == HD
    in_specs, args, y_specs = [], [], []
    for z in range(2):
        tok = (lambda t: t) if z == 0 else (lambda t: nb - 1 - t)
        one = pl.BlockSpec((1, C, E), lambda b, t, tok=tok: (b, tok(t), 0))
        two = pl.BlockSpec((1, 1, C, E), lambda b, t, tok=tok, z=z: (z, b, tok(t), 0))
        in_specs += [one, one, one, two, two, two]
        args += [r, kk, v, lw, a, kz]
        y_specs.append(one)
    if s0 is not None:
        states, layer = s0
        for z in range(2):
            in_specs.append(pl.BlockSpec((None, None, None, H, HD, HD), lambda b, t, z=z: (b, layer, z, 0, 0, 0)))
            args.append(states)
    return pl.pallas_call(
        functools.partial(_scan_kernel, has_init=s0 is not None),
        grid=(B, nb),
        in_specs=in_specs,
        out_specs=y_specs + [pl.BlockSpec((None, 2, H, HD, HD), lambda b, t: (b, 0, 0, 0, 0))],
        out_shape=[jax.ShapeDtypeStruct((B, T, E), F32)] * 2 + [jax.ShapeDtypeStruct((B, 2, H, HD, HD), F32)],
        scratch_shapes=[pltpu.VMEM((2, NP, HD, LANES), F32)],
        compiler_params=_cparams(("parallel", "arbitrary")),
        name="rwkv_scan",
    )(*args)


def _rwkv_out_kernel(y0_ref, y1_ref, bonus_ref, sg_ref, lnw_ref, lnb_ref, seg_ref, w_ref, x_ref, mod_ref, o_ref):
    y = y0_ref[0] + y1_ref[0]
    inv_n = 1.0 / RWKV_HEAD_DIM
    d = y - _headsum(y, seg_ref) * inv_n
    var = _headsum(d * d, seg_ref) * inv_n
    yn = d * lax.rsqrt(var + RWKV_GN_EPS) * lnw_ref[...] + lnb_ref[...] + bonus_ref[0].astype(F32)
    o_ref[0] = _gated_residual(yn * sg_ref[0].astype(F32), w_ref, x_ref, mod_ref)


def _rwkv_out(y0, y1, bonus, sg, p, x, mods, layer, row_of):
    B, T, D = x.shape
    tm = min(ROW_TILE, T)
    tile = pl.BlockSpec((1, tm, D), lambda b, t: (b, t, 0))
    return pl.pallas_call(
        _rwkv_out_kernel,
        grid=(B, T // tm),
        in_specs=[tile, tile, tile, tile, _const_spec((1, D)), _const_spec((1, D)),
                  _const_spec((MXU_DIM, MXU_DIM)), _const_spec((D, D)), tile, _mod_spec(layer, row_of)],
        out_specs=tile,
        out_shape=jax.ShapeDtypeStruct((B, T, D), F32),
        compiler_params=_cparams(("parallel", "parallel")),
        name="rwkv_out",
    )(y0, y1, bonus, sg, p["ln_w"], p["ln_b"], p["seg"], p["w_out"], x, mods)


def _conv_in_kernel(x_ref, nw_ref, mod_ref, w_ref, z_ref, sg_ref):
    hb = _normmod(x_ref[0], nw_ref[...], mod_ref[...]).astype(BF16)
    E = D_INNER
    for n in range(0, E, N_CHUNK):
        a = _mm(hb, w_ref[:, n:n + N_CHUNK])
        b = _mm(hb, w_ref[:, E + n:E + n + N_CHUNK])
        g = _mm(hb, w_ref[:, 2 * E + n:2 * E + n + N_CHUNK])
        z_ref[0, :, n:n + N_CHUNK] = a * jax.nn.sigmoid(b)
        sg_ref[0, :, n:n + N_CHUNK] = _silu(g).astype(BF16)


def _conv_in(x, nw, mods, layer, row_of, w_bf16):
    B, T, D = x.shape
    E = D_INNER
    tm = min(ROW_TILE, T)
    tile = pl.BlockSpec((1, tm, D), lambda b, t: (b, t, 0))
    return pl.pallas_call(
        _conv_in_kernel,
        grid=(B, T // tm),
        in_specs=[tile, _const_spec((1, D)), _mod_spec(layer, row_of), _const_spec((D, 3 * E))],
        out_specs=[tile, tile],
        out_shape=[jax.ShapeDtypeStruct((B, T, E), F32), jax.ShapeDtypeStruct((B, T, E), BF16)],
        compiler_params=_cparams(("parallel", "parallel")),
        name="conv_in",
    )(x, nw.reshape(1, D), mods, w_bf16)


def _conv_out_kernel(z_ref, zp_ref, zn_ref, sg_ref, dw_ref, dwb_ref, lnw_ref, lnb_ref, w_ref, x_ref, mod_ref,
                     o_ref, zs, cbuf):
    t = pl.program_id(1)
    nt = pl.num_programs(1)
    tm = z_ref.shape[1]
    rows = tm + 2 * HALO
    zs[0, 0:HALO, :] = jnp.where(t > 0, zp_ref[0], 0.0)
    zs[0, HALO:HALO + tm, :] = z_ref[0]
    zs[0, HALO + tm:rows, :] = jnp.where(t < nt - 1, zn_ref[0], 0.0)
    for s in range(1, SUBLANES):
        zs[s, 0:rows - SUBLANES, :] = zs[0, s:s + rows - SUBLANES, :]
    off = HALO - CONV_PAD
    for c0 in range(0, D_INNER, LANES):
        for r0 in range(0, tm, CONV_ROWS):
            acc = None
            for k in range(CONV_WIDTH):
                q8, s = divmod(off + k, SUBLANES)
                a0 = r0 + q8 * SUBLANES
                term = dw_ref[k:k + 1, c0:c0 + LANES] * zs[s, a0:a0 + CONV_ROWS, c0:c0 + LANES]
                acc = term if acc is None else acc + term
            cbuf[r0:r0 + CONV_ROWS, c0:c0 + LANES] = acc
    c = cbuf[...] + dwb_ref[...]
    m = jnp.mean(c, axis=-1, keepdims=True)
    d = c - m
    var = jnp.mean(d * d, axis=-1, keepdims=True)
    y = d * lax.rsqrt(var + 1e-5) * lnw_ref[...] + lnb_ref[...]
    o_ref[0] = _gated_residual(_silu(y) * sg_ref[0].astype(F32), w_ref, x_ref, mod_ref)


def _conv_out(z, sg, p, x, mods, layer, row_of):
    B, T, D = x.shape
    tm = min(CONV_ROW_TILE, T)
    gh = tm // HALO
    lasth = T // HALO - 1
    tile = pl.BlockSpec((1, tm, D), lambda b, t: (b, t, 0))
    prev = pl.BlockSpec((1, HALO, D), lambda b, t: (b, jnp.maximum(t * gh - 1, 0), 0))
    nxt = pl.BlockSpec((1, HALO, D), lambda b, t: (b, jnp.minimum((t + 1) * gh, lasth), 0))
    return pl.pallas_call(
        _conv_out_kernel,
        grid=(B, T // tm),
        in_specs=[tile, prev, nxt, tile, _const_spec((CONV_WIDTH, D)), _const_spec((1, D)),
                  _const_spec((1, D)), _const_spec((1, D)), _const_spec((D, D)), tile,
                  _mod_spec(layer, row_of)],
        out_specs=tile,
        out_shape=jax.ShapeDtypeStruct((B, T, D), F32),
        scratch_shapes=[pltpu.VMEM((SUBLANES, tm + 2 * HALO, D), F32), pltpu.VMEM((tm, D), F32)],
        compiler_params=_cparams(("parallel", "parallel")),
        name="conv_out",
    )(z, z, z, sg, p["dw_w"], p["dw_b"], p["ln_w"], p["ln_b"], p["w_out"], x, mods)


def _rope_tables(n_tok):
    t = jnp.arange(n_tok, dtype=jnp.int32)
    pos = jnp.stack([(t // GRID_W).astype(F32), (t % GRID_W).astype(F32)], axis=1)
    lane = jnp.arange(LANES, dtype=jnp.int32)
    d = lane % ATTN_HEAD_DIM
    axis = d // 32
    second_half = (d % 32) // 16
    axis_dim = ATTN_HEAD_DIM // 2
    inv_freq = ROPE_BASE ** (-(2.0 * (d % 16).astype(F32)) / axis_dim)
    ang = pos[:, axis] * inv_freq[None, :]
    sign = jnp.where(second_half == 1, 1.0, -1.0).astype(F32)
    return jnp.cos(ang), jnp.sin(ang) * sign[None, :]


def kernel(x_prompt, x_sample, cache_attn_k, cache_attn_v, state_rwkv, c, c_ctx, norm_w, ada_w, ada_b, attn_w_in, attn_lambda, attn_subln_w, attn_w_out, rwkv_mu, rwkv_w_in, rwkv_w0, rwkv_w1, rwkv_w2, rwkv_a0, rwkv_a1, rwkv_a2, rwkv_k_k, rwkv_k_a, rwkv_r_k, rwkv_ln_w, rwkv_ln_b, rwkv_w_out, conv_w_in, conv_dw_w, conv_dw_b, conv_ln_w, conv_ln_b, conv_w_out, final_norm_w):
    D, E = D_MODEL, D_INNER
    dec_batch = x_sample.shape[0]
    assert dec_batch < COND_ROWS
    cond = jnp.concatenate([c, c_ctx[None, :], jnp.zeros((COND_ROWS - dec_batch - 1, D), F32)], axis=0)
    mods = _ada(cond, ada_w, ada_b)
    ctx_row = dec_batch
    streams = [(x_prompt, lambda b: ctx_row), (x_sample, lambda b: b)]
    cos, sin = _rope_tables(x_sample.shape[1])
    seg = (jnp.arange(MXU_DIM)[:, None] // RWKV_HEAD_DIM == jnp.arange(MXU_DIM)[None, :] // RWKV_HEAD_DIM).astype(BF16)

    xs = [x_prompt, x_sample]

    def rows(a, s):
        B, T, D = a.shape
        if s == 0 and T < ROW_TILE and (B * T) % ROW_TILE == 0:
            return a.reshape(B * T // ROW_TILE, ROW_TILE, D)
        return a
    new_kv, new_s = (None, None), []
    assert (DEPTH - 1) % N_MIXERS == 0
    for i in range(DEPTH):
        kind, j = i % N_MIXERS, i // N_MIXERS
        last = i == DEPTH - 1
        fw = final_norm_w if last else None
        if kind == 0:
            w_in = attn_w_in[j].astype(BF16)
            w_out = attn_w_out[j].astype(BF16)
            for s, (_, row_of) in enumerate(streams):
                if s == 0:
                    qkvg, *new_kv = _attn_in(xs[s], norm_w[i], mods, i, row_of, w_in, cache=new_kv)
                    o = _attn_core(qkvg, attn_lambda[j], attn_subln_w[j], i)
                else:
                    (qkvg,) = _attn_in(xs[s], norm_w[i], mods, i, row_of, w_in, rope=(cos, sin))
                    o = _attn_core(qkvg, attn_lambda[j], attn_subln_w[j], i, ctx=(cache_attn_k, cache_attn_v, j))
                xs[s] = _out_proj(rows(o, s), w_out, rows(xs[s], s), mods, i, row_of, final_w=fw).reshape(xs[s].shape)
        elif kind == 1:
            p = dict(
                mu=rwkv_mu[j],
                w_in=rwkv_w_in[j].astype(BF16),
                l1=jnp.concatenate([rwkv_w1[j, 0], rwkv_w1[j, 1], rwkv_a1[j, 0], rwkv_a1[j, 1]], axis=1).astype(BF16),
                w2=rwkv_w2[j].reshape(2 * LORA, E).astype(BF16),
                a2=rwkv_a2[j].reshape(2 * LORA, E).astype(BF16),
                w0=rwkv_w0[j], a0=rwkv_a0[j],
                k_k=rwkv_k_k[j].reshape(1, E), k_a=rwkv_k_a[j].reshape(1, E), r_k=rwkv_r_k[j].reshape(1, E),
                ln_w=rwkv_ln_w[j].reshape(1, E), ln_b=rwkv_ln_b[j].reshape(1, E),
                w_out=rwkv_w_out[j].astype(BF16), seg=seg)
            for s, (_, row_of) in enumerate(streams):
                r, kk, v, sg, bonus, lw, a, kz = _rwkv_in(xs[s], norm_w[i], mods, i, row_of, p)
                y_f, y_b, final_states = _rwkv_scan(r, kk, v, lw, a, kz, (state_rwkv, j) if s == 1 else None)
                if s == 0:
                    new_s.append(final_states)
                xs[s] = _rwkv_out(rows(y_f, s), rows(y_b, s), rows(bonus, s), rows(sg, s), p, rows(xs[s], s),
                                  mods, i, row_of).reshape(xs[s].shape)
        else:
            p = dict(dw_w=conv_dw_w[j], dw_b=conv_dw_b[j].reshape(1, E), ln_w=conv_ln_w[j].reshape(1, E),
                     ln_b=conv_ln_b[j].reshape(1, E), w_out=conv_w_out[j].astype(BF16))
            w_in = conv_w_in[j].astype(BF16)
            for s, (_, row_of) in enumerate(streams):
                zz, sg = _conv_in(rows(xs[s], s), norm_w[i], mods, i, row_of, w_in)
                xs[s] = _conv_out(zz.reshape(xs[s].shape), sg.reshape(xs[s].shape), p, xs[s], mods, i, row_of)
    return (xs[0], xs[1], new_kv[0], new_kv[1], jnp.stack(new_s, axis=1))
```

```python
import functools
import math

import jax
import jax.numpy as jnp
from jax import lax
from jax.experimental import pallas as pl
from jax.experimental.pallas import tpu as pltpu

F32 = jnp.float32
BF16 = jnp.bfloat16

D_MODEL = 1024
D_INNER = 1024
DEPTH = 4
N_MIXERS = 3
GRID_W = 64
NORM_EPS = 1e-6
ATTN_HEAD_DIM = 64
ATTN_HEADS = 8
ATTN_V_DIM = 128
ROPE_BASE = 10000.0
RWKV_HEAD_DIM = 64
RWKV_HEADS = 16
RWKV_GN_EPS = RWKV_HEAD_DIM * 1e-5
LORA = 64
CONV_WIDTH = 31
CONV_PAD = CONV_WIDTH // 2

LANES = 128
SUBLANES = 8
MXU_DIM = 256
VMEM_LIMIT_BYTES = 56 * 1024 * 1024

ROW_TILE = 1024
CONV_ROW_TILE = 256
RWKV_ROW_TILE = 256
ATTN_Q_TILE = 256
ATTN_ROWS_PER_STEP = 4096
ATTN_TASK_GROUP = 2
N_CHUNK = 512
SCAN_CHUNK = 64
HALO = 16
CONV_ROWS = 64
COND_ROWS = 16
DN = (((1,), (0,)), ((), ()))
DN_NT = (((1,), (1,)), ((), ()))


def _cparams(sem):
    return pltpu.CompilerParams(dimension_semantics=sem, vmem_limit_bytes=VMEM_LIMIT_BYTES)


def _mm(a, b, dn=DN):
    return lax.dot_general(a, b, dn, preferred_element_type=F32)


def _dot(a, b):
    return _mm(a.astype(BF16), b.astype(BF16))


def _split2(x):
    hi = x.astype(BF16)
    lo = (x - hi.astype(F32)).astype(BF16)
    return hi, lo


def _dot3(a, b):
    a_hi, a_lo = _split2(a)
    b_hi, b_lo = _split2(b)
    return _mm(a_hi, b_hi) + (_mm(a_lo, b_hi) + _mm(a_hi, b_lo))


def _silu(x):
    return x * jax.nn.sigmoid(x)


def _normmod(x, nw, mod):
    ms = jnp.mean(x * x, axis=-1, keepdims=True)
    y = x * lax.rsqrt(ms + NORM_EPS) * nw
    return y * (1.0 + mod[:, D_MODEL:2 * D_MODEL]) + mod[:, 0:D_MODEL]


def _headsum(x, seg_ref):
    cols = []
    for c in range(0, x.shape[-1], MXU_DIM):
        hi, lo = _split2(x[:, c:c + MXU_DIM])
        cols.append(_mm(hi, seg_ref[...]) + _mm(lo, seg_ref[...]))
    return jnp.concatenate(cols, axis=-1)


def _mod_spec(layer, row_of):
    return pl.BlockSpec((None, None, 1, 3 * D_MODEL), lambda b, t: (layer, row_of(b), 0, 0))


def _const_spec(shape):
    return pl.BlockSpec(shape, lambda b, t: (0,) * len(shape), pipeline_mode=pl.Buffered(1))


def _ada_kernel(cond_ref, w_ref, b_ref, o_ref):
    o_ref[0, :, 0, :] = _dot3(_silu(cond_ref[...]), w_ref[0]) + b_ref[0]


def _ada(cond, ada_w, ada_b):
    depth, d, n3 = ada_w.shape
    tn = 1024
    return pl.pallas_call(
        _ada_kernel,
        grid=(depth, n3 // tn),
        in_specs=[
            pl.BlockSpec((COND_ROWS, d), lambda i, j: (0, 0)),
            pl.BlockSpec((1, d, tn), lambda i, j: (i, 0, j)),
            pl.BlockSpec((1, 1, tn), lambda i, j: (i, 0, j)),
        ],
        out_specs=pl.BlockSpec((1, COND_ROWS, 1, tn), lambda i, j: (i, 0, 0, j)),
        out_shape=jax.ShapeDtypeStruct((depth, COND_ROWS, 1, n3), F32),
        compiler_params=_cparams(("parallel", "parallel")),
        name="ada_mod",
    )(cond, ada_w, ada_b.reshape(depth, 1, n3))


ATTN_Q_SCALE = ATTN_HEAD_DIM ** -0.5 * math.log2(math.e)


def _attn_in_kernel(x_ref, nw_ref, mod_ref, w_ref, *rest, rope, n_cached):
    rest = list(rest)
    if rope:
        cos_ref, sin_ref = rest.pop(0), rest.pop(0)
    if n_cached:
        pk_ref, pv_ref = rest.pop(0), rest.pop(0)
    o_ref = rest.pop(0)
    if n_cached is not None:
        ko_ref, vo_ref = rest
        if n_cached:
            ko_ref[0:n_cached] = pk_ref[...]
            vo_ref[0:n_cached] = pv_ref[...]
    E = D_INNER
    hb = _normmod(x_ref[0], nw_ref[...], mod_ref[...]).astype(BF16)
    for n in range(0, 4 * E, N_CHUNK):
        y = _mm(hb, w_ref[:, n:n + N_CHUNK])
        if n_cached is not None and E <= n < 3 * E:
            cache_ref = ko_ref if n < 2 * E else vo_ref
            for c in range(0, N_CHUNK, LANES):
                cache_ref[n_cached, (n % E + c) // LANES] = y[:, c:c + LANES]
        if n < E:
            y = y * ATTN_Q_SCALE
        if rope and n < 2 * E:
            y = jnp.concatenate([_rope(y[:, c:c + LANES], cos_ref[...], sin_ref[...])
                                 for c in range(0, N_CHUNK, LANES)], axis=1)
        o_ref[0, :, n:n + N_CHUNK] = y.astype(BF16)


def _attn_in(x, nw, mods, layer, row_of, w_bf16, rope=None, cache=None):
    B, T, D = x.shape
    N = w_bf16.shape[1]
    H, DV = ATTN_HEADS, ATTN_V_DIM
    tm = min(ROW_TILE, T)
    in_specs = [pl.BlockSpec((1, tm, D), lambda b, t: (b, t, 0)), _const_spec((1, D)),
                _mod_spec(layer, row_of), _const_spec((D, N))]
    args = [x, nw.reshape(1, D), mods, w_bf16]
    if rope is not None:
        in_specs += [pl.BlockSpec((tm, LANES), lambda b, t: (t, 0))] * 2
        args += list(rope)
    out_specs = [pl.BlockSpec((1, tm, N), lambda b, t: (b, t, 0))]
    out_shape = [jax.ShapeDtypeStruct((B, T, N), BF16)]
    n_cached = None
    if cache is not None:
        n_cached = 0 if cache[0] is None else cache[0].shape[1]
        if n_cached:
            in_specs += [pl.BlockSpec((None, n_cached, H, tm, DV), lambda b, t: (b, 0, 0, t, 0))] * 2
            args += list(cache)
        out_specs += [pl.BlockSpec((None, n_cached + 1, H, tm, DV), lambda b, t: (b, 0, 0, t, 0))] * 2
        out_shape += [jax.ShapeDtypeStruct((B, n_cached + 1, H, T, DV), F32)] * 2
    return pl.pallas_call(
        functools.partial(_attn_in_kernel, rope=rope is not None, n_cached=n_cached),
        grid=(B, T // tm),
        in_specs=in_specs,
        out_specs=out_specs,
        out_shape=out_shape,
        compiler_params=_cparams(("parallel", "parallel")),
        name="attn_in",
    )(*args)


def _rope(x, cos, sin):
    lane = lax.broadcasted_iota(jnp.int32, x.shape, 1)
    first = (lane % 32) < 16
    partner = jnp.where(first, pltpu.roll(x, LANES - 16, axis=1), pltpu.roll(x, 16, axis=1))
    return x * cos + partner * sin


def _attn_core_kernel(*refs, lam_init, has_ctx, tq, group):
    it = iter(refs)
    lam_ref, q_ref, k_ref, v_ref, g_ref, subw_ref = (next(it) for _ in range(6))
    if has_ctx:
        ck_ref, cv_ref = next(it), next(it)
    o_ref = next(it)
    T = q_ref.shape[1]
    n_heads = q_ref.shape[2] // LANES
    n_tiles = T // tq
    k_scr = [next(it) for _ in range(n_heads)]
    v_scr = [next(it) for _ in range(n_heads)]
    ring = 2 * group
    s_scr = [[next(it), next(it)] for _ in range(ring)]
    e_scr = [[next(it), next(it)] for _ in range(ring)]
    m_scr = [[next(it), next(it)] for _ in range(ring)]

    lv = lam_ref[...]
    lam = (jnp.exp(jnp.sum(lv[0:1] * lv[1:2], axis=-1, keepdims=True))
           - jnp.exp(jnp.sum(lv[2:3] * lv[3:4], axis=-1, keepdims=True)) + lam_init)

    tk = k_scr[0].shape[0]
    for hh in range(n_heads):
        hl = slice(hh * LANES, (hh + 1) * LANES)
        if has_ctx:
            k_scr[hh][T:, :] = ck_ref[hh].astype(BF16)
            v_scr[hh][T:, 0:LANES] = cv_ref[hh].astype(BF16)
        k_scr[hh][0:T, :] = k_ref[0, :, hl]
        v_scr[hh][0:T, 0:LANES] = v_ref[0, :, hl]
        v_scr[hh][:, LANES:2 * LANES] = jnp.ones((tk, LANES), BF16)

    kblocks = [slice(c, c + MXU_DIM) for c in range(0, tk, MXU_DIM)]
    tasks = [(hh, i) for hh in range(n_heads) for i in range(n_tiles)]

    def halves_max(x):
        return jnp.maximum(x[:, 0:LANES], x[:, LANES:2 * LANES])

    def score_stage(n):
        hh, i = tasks[n]
        q = q_ref[0, i * tq:(i + 1) * tq, hh * LANES:(hh + 1) * LANES]
        lane = lax.broadcasted_iota(jnp.int32, q.shape, 1)
        for mp in range(2):
            sel = (lane < ATTN_HEAD_DIM) if mp == 0 else (lane >= ATTN_HEAD_DIM)
            qm = jnp.where(sel, q, jnp.zeros_like(q))
            mx = None
            for kb in kblocks:
                s = _mm(qm, k_scr[hh][kb, :], DN_NT)
                s_scr[n % ring][mp][:, kb] = s
                mx = halves_max(s) if mx is None else jnp.maximum(mx, halves_max(s))
                yield
            m_scr[n % ring][mp][...] = jnp.broadcast_to(jnp.max(mx, axis=-1, keepdims=True), mx.shape)

    def attend_stage(n):
        hh, i = tasks[n]
        outs = []
        for mp in range(2):
            m = m_scr[n % ring][mp][...]
            m = jnp.concatenate([m, m], axis=1)
            for kb in kblocks:
                e_scr[n % ring][mp][:, kb] = jnp.exp2(s_scr[n % ring][mp][:, kb] - m).astype(BF16)
                yield
            ov = _mm(e_scr[n % ring][mp][...], v_scr[hh][...])
            outs.append((ov[:, 0:LANES], ov[:, LANES:2 * LANES]))
            yield
        (o1, d1), (o2, d2) = outs
        o = o1 * (1.0 / d1) - o2 * (lam / d2)
        ms = jnp.mean(o * o, axis=-1, keepdims=True)
        o = o * lax.rsqrt(ms + 1e-5) * subw_ref[...] * (1.0 - lam_init)
        rows, hl = slice(i * tq, (i + 1) * tq), slice(hh * LANES, (hh + 1) * LANES)
        o_ref[0, rows, hl] = (o * _silu(g_ref[0, rows, hl].astype(F32))).astype(BF16)

    n_groups = len(tasks) // group
    for t in range(n_groups + 1):
        live = []
        if t < n_groups:
            live += [score_stage(t * group + u) for u in range(group)]
        if t >= 1:
            live += [attend_stage((t - 1) * group + u) for u in range(group)]
        while live:
            for gen in list(live):
                if next(gen, "done") == "done":
                    live.remove(gen)


def _attn_core(qkvg, lam_vecs, subw, layer_idx, ctx=None):
    B, T, _ = qkvg.shape
    H, DV = ATTN_HEADS, ATTN_V_DIM
    HS = min(H, max(1, ATTN_ROWS_PER_STEP // T))
    lam_init = 0.8 - 0.6 * math.exp(-0.3 * layer_idx)
    has_ctx = ctx is not None
    tq = min(ATTN_Q_TILE, T)
    tk = T + (ctx[0].shape[3] if has_ctx else 0)

    def col(off):
        return pl.BlockSpec((1, T, HS * DV), lambda b, h: (b, 0, off + h))

    nb = H // HS
    in_specs = [pl.BlockSpec((4, ATTN_HEAD_DIM), lambda b, h: (0, 0)),
                col(0), col(nb), col(2 * nb), col(3 * nb),
                pl.BlockSpec((1, DV), lambda b, h: (0, 0))]
    args = [lam_vecs, qkvg, qkvg, qkvg, qkvg, subw.reshape(1, DV)]
    if has_ctx:
        ck, cv, j = ctx
        P = ck.shape[3]
        cache_spec = pl.BlockSpec((None, None, HS, P, DV), lambda b, h: (b, j, h, 0, 0))
        in_specs += [cache_spec, cache_spec]
        args += [ck, cv]
    group = ATTN_TASK_GROUP if tk // MXU_DIM >= 2 * ATTN_TASK_GROUP else 1
    assert (HS * (T // tq)) % group == 0
    n_buf = 2 * group * 2
    return pl.pallas_call(
        functools.partial(_attn_core_kernel, lam_init=lam_init, has_ctx=has_ctx, tq=tq, group=group),
        grid=(B, nb),
        in_specs=in_specs,
        out_specs=pl.BlockSpec((1, T, HS * DV), lambda b, h: (b, 0, h)),
        out_shape=jax.ShapeDtypeStruct((B, T, D_INNER), BF16),
        scratch_shapes=([pltpu.VMEM((tk, DV), BF16)] * HS + [pltpu.VMEM((tk, 2 * DV), BF16)] * HS
                        + [pltpu.VMEM((tq, tk), F32)] * n_buf + [pltpu.VMEM((tq, tk), BF16)] * n_buf
                        + [pltpu.VMEM((tq, LANES), F32)] * n_buf),
        compiler_params=_cparams(("parallel", "parallel")),
        name="attn_core",
    )(*args)


def _gated_residual(o, w_ref, x_ref, mod_ref):
    return x_ref[0] + mod_ref[:, 2 * D_MODEL:3 * D_MODEL] * _dot(o, w_ref[...])


def _out_proj_kernel(o_ref, w_ref, x_ref, mod_ref, *rest, final_norm):
    xn = _gated_residual(o_ref[0], w_ref, x_ref, mod_ref)
    if final_norm:
        fw_ref, y_ref = rest
        ms = jnp.mean(xn * xn, axis=-1, keepdims=True)
        xn = xn * lax.rsqrt(ms + NORM_EPS) * fw_ref[...]
    else:
        (y_ref,) = rest
    y_ref[0] = xn


def _out_proj(o, w_bf16, x, mods, layer, row_of, final_w=None):
    B, T, D = x.shape
    tm = min(ROW_TILE, T)
    tile = pl.BlockSpec((1, tm, D), lambda b, t: (b, t, 0))
    in_specs = [tile, _const_spec((D, D)), tile, _mod_spec(layer, row_of)]
    args = [o, w_bf16, x, mods]
    if final_w is not None:
        in_specs.append(_const_spec((1, D)))
        args.append(final_w.reshape(1, D))
    return pl.pallas_call(
        functools.partial(_out_proj_kernel, final_norm=final_w is not None),
        grid=(B, T // tm),
        in_specs=in_specs,
        out_specs=tile,
        out_shape=jax.ShapeDtypeStruct((B, T, D), F32),
        compiler_params=_cparams(("parallel", "parallel")),
        name="out_proj",
    )(*args)


def _rwkv_in_kernel(x_ref, xp_ref, xn_ref, nw_ref, mod_ref, mu_ref, w_ref, l1_ref, w2_ref, a2_ref,
                    w0_ref, a0_ref, kk_ref, ka_ref, rk_ref, seg_ref,
                    r_out, kk_out, v_out, sg_out, bonus_out, lw_out, a_out, kz_out):
    t = pl.program_id(1)
    nt = pl.num_programs(1)
    nw = nw_ref[...]
    mod = mod_ref[...]
    h = _normmod(x_ref[0], nw, mod)
    tm = h.shape[0]
    hp = _normmod(xp_ref[0], nw, mod)[SUBLANES - 1:SUBLANES, :]
    hn = _normmod(xn_ref[0], nw, mod)[0:1, :]
    hp = jnp.where(t > 0, hp, 0.0)
    hn = jnp.where(t < nt - 1, hn, 0.0)
    row = lax.broadcasted_iota(jnp.int32, h.shape, 0)
    h_prev = jnp.where(row == 0, hp, pltpu.roll(h, 1, axis=0))
    h_next = jnp.where(row == tm - 1, hn, pltpu.roll(h, tm - 1, axis=0))
    dx = 0.5 * (h_prev + h_next) - h

    def mix(n):
        return (h + dx * mu_ref[n:n + 1, :]).astype(BF16)

    E = D_INNER
    r = _mm(mix(0), w_ref[:, 0:E])
    k = _mm(mix(1), w_ref[:, E:2 * E])
    v = _mm(mix(2), w_ref[:, 2 * E:3 * E])
    g = _mm(mix(3), w_ref[:, 3 * E:4 * E])
    lw1 = jnp.tanh(_mm(mix(4), l1_ref[:, 0:2 * LORA]))
    la1 = _mm(mix(5), l1_ref[:, 2 * LORA:4 * LORA])
    lane = lax.broadcasted_iota(jnp.int32, lw1.shape, 1)

    kk = k * kk_ref[...]
    kk = kk * lax.rsqrt(jnp.maximum(_headsum(kk * kk, seg_ref), 1e-24))
    r_out[0] = r
    kk_out[0] = kk
    v_out[0] = v
    sg_out[0] = _silu(g).astype(BF16)

    ksum = None
    for z in range(2):
        sel = (lane >= z * LORA) & (lane < (z + 1) * LORA)
        lw = _dot(jnp.where(sel, lw1, 0.0), w2_ref[...])
        la = _dot(jnp.where(sel, la1, 0.0), a2_ref[...])
        lw_out[z, 0] = -math.exp(-0.5) * jax.nn.sigmoid(w0_ref[z:z + 1, :] + lw)
        a = jax.nn.sigmoid(a0_ref[z:z + 1, :] + la)
        a_out[z, 0] = a
        kz = k * (1.0 + (a - 1.0) * ka_ref[...])
        kz_out[z, 0] = kz
        ksum = kz if ksum is None else ksum + kz
    bonus_out[0] = (_headsum(r * rk_ref[...] * ksum, seg_ref) * v).astype(BF16)


def _rwkv_in(x, nw, mods, layer, row_of, p):
    B, T, D = x.shape
    E = D_INNER
    tm = min(RWKV_ROW_TILE, T)
    g8 = tm // SUBLANES
    last8 = T // SUBLANES - 1
    tile = pl.BlockSpec((1, tm, D), lambda b, t: (b, t, 0))
    prev8 = pl.BlockSpec((1, SUBLANES, D), lambda b, t: (b, jnp.maximum(t * g8 - 1, 0), 0))
    next8 = pl.BlockSpec((1, SUBLANES, D), lambda b, t: (b, jnp.minimum((t + 1) * g8, last8), 0))
    otile = pl.BlockSpec((1, tm, E), lambda b, t: (b, t, 0))
    ztile = pl.BlockSpec((2, 1, tm, E), lambda b, t: (0, b, t, 0))
    one = jax.ShapeDtypeStruct((B, T, E), F32)
    two = jax.ShapeDtypeStruct((2, B, T, E), F32)
    return pl.pallas_call(
        _rwkv_in_kernel,
        grid=(B, T // tm),
        in_specs=[tile, prev8, next8, _const_spec((1, D)), _mod_spec(layer, row_of),
                  _const_spec((6, D)), _const_spec((D, 4 * E)), _const_spec((D, 4 * LORA)),
                  _const_spec((2 * LORA, E)), _const_spec((2 * LORA, E)),
                  _const_spec((2, E)), _const_spec((2, E)),
                  _const_spec((1, E)), _const_spec((1, E)), _const_spec((1, E)),
                  _const_spec((MXU_DIM, MXU_DIM))],
        out_specs=[otile] * 5 + [ztile] * 3,
        out_shape=[one] * 3 + [jax.ShapeDtypeStruct((B, T, E), BF16)] * 2 + [two] * 3,
        compiler_params=_cparams(("parallel", "parallel")),
        name="rwkv_in",
    )(x, x, x, nw.reshape(1, D), mods, p["mu"], p["w_in"], p["l1"], p["w2"], p["a2"],
      p["w0"], p["a0"], p["k_k"], p["k_a"], p["r_k"], p["seg"])


def _scan_kernel(*refs, has_init):
    n_in = 6
    ins = [refs[z * n_in:(z + 1) * n_in] for z in range(2)]
    rest = refs[2 * n_in:]
    if has_init:
        s0_refs, rest = rest[0:2], rest[2:]
    y_refs, sf_ref, s_scr = rest[0:2], rest[2], rest[3]
    C = SCAN_CHUNK
    PW = 2 * C
    HD = RWKV_HEAD_DIM
    NP = y_refs[0].shape[-1] // LANES
    chains = [(z, p) for z in range(2) for p in range(NP)]
    idx = range(len(chains))
    zs = [z for z, _ in chains]
    lo_h = lax.broadcasted_iota(jnp.int32, (HD, LANES), 1) < HD

    @pl.when(pl.program_id(1) == 0)
    def _():
        if not has_init:
            s_scr[...] = jnp.zeros_like(s_scr)
            return
        zero = jnp.zeros((HD, HD), F32)
        for z, p in chains:
            both = jnp.concatenate([jnp.concatenate([s0_refs[z][2 * p], zero], axis=1),
                                    jnp.concatenate([zero, s0_refs[z][2 * p + 1]], axis=1)], axis=0).T
            s_scr[z, p] = jnp.where(lo_h, both[0:HD], both[HD:2 * HD])

    state_diag = (lax.broadcasted_iota(jnp.int32, (HD, LANES), 0)
                  == lax.broadcasted_iota(jnp.int32, (HD, LANES), 1) % HD)
    tr = lax.broadcasted_iota(jnp.int32, (C, PW), 0)
    tc = lax.broadcasted_iota(jnp.int32, (C, PW), 1) % C
    eye = jnp.where(tr == tc, 1.0, 0.0)
    strict, incl = [], []
    for z in range(2):
        before = (tc > tr) if z == 1 else (tc < tr)
        strict.append(before)
        incl.append(before | (tc == tr))
    lo = lax.broadcasted_iota(jnp.int32, (C, LANES), 1) < HD
    zeros_c = jnp.zeros((C, LANES), F32)
    zeros_p = jnp.zeros((PW, LANES), F32)

    def heads_down(x):
        return jnp.concatenate([jnp.where(lo, x, 0.0), jnp.where(lo, 0.0, x)], axis=0)

    def load(z, which, p):
        ref = ins[z][which]
        sl = slice(p * LANES, (p + 1) * LANES)
        return ref[0, :, sl] if len(ref.shape) == 3 else ref[0, 0, :, sl]

    row = lax.broadcasted_iota(jnp.int32, (C, LANES), 0)

    def running_sum(x, z):
        s = 1
        while s < C:
            if z == 0:
                x = x + jnp.where(row >= s, pltpu.roll(x, s, axis=0), 0.0)
            else:
                x = x + jnp.where(row < C - s, pltpu.roll(x, C - s, axis=0), 0.0)
            s *= 2
        return x

    lw = [load(z, 3, p) for z, p in chains]
    cum = [running_sum(lw[i], zs[i]) for i in idx]
    tot = [(cum[i][0:1] if zs[i] == 1 else cum[i][C - 1:C]) for i in idx]
    abar, rbar, bh, kh, v, res = [], [], [], [], [], []
    for i, (z, p) in enumerate(chains):
        kk = load(z, 1, p)
        b = kk * load(z, 4, p)
        kz = load(z, 5, p)
        e_neg = jnp.exp(-cum[i])
        e_tot = jnp.exp(tot[i] - cum[i])
        abar.append(-kk * jnp.exp(cum[i] - lw[i]))
        rbar.append(load(z, 0, p) * jnp.exp(cum[i]))
        bt, kt = b * e_neg, kz * e_neg
        bh.append(b * e_tot)
        kh.append(kz * e_tot)
        v.append(load(z, 2, p))
        lhs = jnp.concatenate([abar[i], rbar[i]], axis=0)
        rhs = jnp.concatenate([heads_down(bt), heads_down(kt)], axis=0)
        res.append(_mm(lhs.astype(BF16), rhs.astype(BF16), DN_NT))
    a_ab = [jnp.where(strict[zs[i]], res[i][0:C, 0:PW], 0.0) for i in idx]
    a_ak = [jnp.where(strict[zs[i]], res[i][0:C, PW:2 * PW], 0.0) for i in idx]
    a_rb = [jnp.where(incl[zs[i]], res[i][C:PW, 0:PW], 0.0) for i in idx]
    a_rk = [jnp.where(incl[zs[i]], res[i][C:PW, PW:2 * PW], 0.0) for i in idx]
    vd = [heads_down(x) for x in v]
    wv = [_dot(a_ak[i], vd[i]) for i in idx]

    pw = [_dot(x, heads_down(x)) for x in a_ab]
    q = [eye + x for x in a_ab]
    n = 4
    while n < C:
        m = [_dot(pw[i], jnp.concatenate([heads_down(q[i]), heads_down(pw[i])], axis=1)) for i in idx]
        q = [q[i] + m[i][:, 0:PW] for i in idx]
        pw = [x[:, PW:2 * PW] for x in m]
        n *= 2
    q = [q[i] + _dot(pw[i], heads_down(q[i])) for i in idx]

    x = [_dot(q[i], jnp.concatenate([heads_down(abar[i]), heads_down(wv[i])], axis=1)) for i in idx]
    ahat = [t[:, 0:LANES] for t in x]
    u0 = [t[:, LANES:2 * LANES] for t in x]

    ry = [_dot(jnp.concatenate([a_rb[i], a_rk[i]], axis=1),
               jnp.concatenate([jnp.concatenate([heads_down(ahat[i]), heads_down(u0[i])], axis=1),
                                jnp.concatenate([zeros_p, vd[i]], axis=1)], axis=0)) for i in idx]
    rhat = [rbar[i] + ry[i][:, 0:LANES] for i in idx]
    y0 = [t[:, LANES:2 * LANES] for t in ry]

    mn = [_dot(jnp.concatenate([bh[i], kh[i]], axis=0).T,
               jnp.concatenate([jnp.concatenate([ahat[i], u0[i]], axis=1),
                                jnp.concatenate([zeros_c, v[i]], axis=1)], axis=0)) for i in idx]
    for i, (z, p) in enumerate(chains):
        m_c = jnp.where(lo, mn[i][0:HD, 0:LANES], mn[i][HD:2 * HD, 0:LANES]) + jnp.where(state_diag, jnp.exp(tot[i]), 0.0)
        n_c = jnp.where(lo, mn[i][0:HD, LANES:2 * LANES], mn[i][HD:2 * HD, LANES:2 * LANES])
        s_old = heads_down(s_scr[z, p])
        y_refs[z][0, :, p * LANES:(p + 1) * LANES] = _dot(rhat[i], s_old) + y0[i]
        s_scr[z, p] = _dot(m_c, s_old) + n_c

    @pl.when(pl.program_id(1) == pl.num_programs(1) - 1)
    def _():
        for z, p in chains:
            both = heads_down(s_scr[z, p]).T
            sf_ref[z, 2 * p] = both[0:HD, 0:HD]
            sf_ref[z, 2 * p + 1] = both[HD:2 * HD, HD:2 * HD]


def _rwkv_scan(r, kk, v, lw, a, kz, s0=None):
    B, T, E = r.shape
    NP = E // LANES
    C = SCAN_CHUNK
    nb = T // C
    HD = RWKV_HEAD_DIM
    H = E // HD
    assert C == HD
    in_specs, args, y_specs = [], [], []
    for z in range(2):
        tok = (lambda t: t) if z == 0 else (lambda t: nb - 1 - t)
        one = pl.BlockSpec((1, C, E), lambda b, t, tok=tok: (b, tok(t), 0))
        two = pl.BlockSpec((1, 1, C, E), lambda b, t, tok=tok, z=z: (z, b, tok(t), 0))
        in_specs += [one, one, one, two, two, two]
        args += [r, kk, v, lw, a, kz]
        y_specs.append(one)
    if s0 is not None:
        states, layer = s0
        for z in range(2):
            in_specs.append(pl.BlockSpec((None, None, None, H, HD, HD), lambda b, t, z=z: (b, layer, z, 0, 0, 0)))
            args.append(states)
    return pl.pallas_call(
        functools.partial(_scan_kernel, has_init=s0 is not None),
        grid=(B, nb),
        in_specs=in_specs,
        out_specs=y_specs + [pl.BlockSpec((None, 2, H, HD, HD), lambda b, t: (b, 0, 0, 0, 0))],
        out_shape=[jax.ShapeDtypeStruct((B, T, E), F32)] * 2 + [jax.ShapeDtypeStruct((B, 2, H, HD, HD), F32)],
        scratch_shapes=[pltpu.VMEM((2, NP, HD, LANES), F32)],
        compiler_params=_cparams(("parallel", "arbitrary")),
        name="rwkv_scan",
    )(*args)


def _rwkv_out_kernel(y0_ref, y1_ref, bonus_ref, sg_ref, lnw_ref, lnb_ref, seg_ref, w_ref, x_ref, mod_ref, o_ref):
    y = y0_ref[0] + y1_ref[0]
    inv_n = 1.0 / RWKV_HEAD_DIM
    d = y - _headsum(y, seg_ref) * inv_n
    var = _headsum(d * d, seg_ref) * inv_n
    yn = d * lax.rsqrt(var + RWKV_GN_EPS) * lnw_ref[...] + lnb_ref[...] + bonus_ref[0].astype(F32)
    o_ref[0] = _gated_residual(yn * sg_ref[0].astype(F32), w_ref, x_ref, mod_ref)


def _rwkv_out(y0, y1, bonus, sg, p, x, mods, layer, row_of):
    B, T, D = x.shape
    tm = min(ROW_TILE, T)
    tile = pl.BlockSpec((1, tm, D), lambda b, t: (b, t, 0))
    return pl.pallas_call(
        _rwkv_out_kernel,
        grid=(B, T // tm),
        in_specs=[tile, tile, tile, tile, _const_spec((1, D)), _const_spec((1, D)),
                  _const_spec((MXU_DIM, MXU_DIM)), _const_spec((D, D)), tile, _mod_spec(layer, row_of)],
        out_specs=tile,
        out_shape=jax.ShapeDtypeStruct((B, T, D), F32),
        compiler_params=_cparams(("parallel", "parallel")),
        name="rwkv_out",
    )(y0, y1, bonus, sg, p["ln_w"], p["ln_b"], p["seg"], p["w_out"], x, mods)


def _conv_in_kernel(x_ref, nw_ref, mod_ref, w_ref, z_ref, sg_ref):
    hb = _normmod(x_ref[0], nw_ref[...], mod_ref[...]).astype(BF16)
    E = D_INNER
    for n in range(0, E, N_CHUNK):
        a = _mm(hb, w_ref[:, n:n + N_CHUNK])
        b = _mm(hb, w_ref[:, E + n:E + n + N_CHUNK])
        g = _mm(hb, w_ref[:, 2 * E + n:2 * E + n + N_CHUNK])
        z_ref[0, :, n:n + N_CHUNK] = a * jax.nn.sigmoid(b)
        sg_ref[0, :, n:n + N_CHUNK] = _silu(g).astype(BF16)


def _conv_in(x, nw, mods, layer, row_of, w_bf16):
    B, T, D = x.shape
    E = D_INNER
    tm = min(ROW_TILE, T)
    tile = pl.BlockSpec((1, tm, D), lambda b, t: (b, t, 0))
    return pl.pallas_call(
        _conv_in_kernel,
        grid=(B, T // tm),
        in_specs=[tile, _const_spec((1, D)), _mod_spec(layer, row_of), _const_spec((D, 3 * E))],
        out_specs=[tile, tile],
        out_shape=[jax.ShapeDtypeStruct((B, T, E), F32), jax.ShapeDtypeStruct((B, T, E), BF16)],
        compiler_params=_cparams(("parallel", "parallel")),
        name="conv_in",
    )(x, nw.reshape(1, D), mods, w_bf16)


def _conv_out_kernel(z_ref, zp_ref, zn_ref, sg_ref, dw_ref, dwb_ref, lnw_ref, lnb_ref, w_ref, x_ref, mod_ref,
                     o_ref, zs, cbuf):
    t = pl.program_id(1)
    nt = pl.num_programs(1)
    tm = z_ref.shape[1]
    rows = tm + 2 * HALO
    zs[0, 0:HALO, :] = jnp.where(t > 0, zp_ref[0], 0.0)
    zs[0, HALO:HALO + tm, :] = z_ref[0]
    zs[0, HALO + tm:rows, :] = jnp.where(t < nt - 1, zn_ref[0], 0.0)
    for s in range(1, SUBLANES):
        zs[s, 0:rows - SUBLANES, :] = zs[0, s:s + rows - SUBLANES, :]
    off = HALO - CONV_PAD
    for c0 in range(0, D_INNER, LANES):
        for r0 in range(0, tm, CONV_ROWS):
            acc = None
            for k in range(CONV_WIDTH):
                q8, s = divmod(off + k, SUBLANES)
                a0 = r0 + q8 * SUBLANES
                term = dw_ref[k:k + 1, c0:c0 + LANES] * zs[s, a0:a0 + CONV_ROWS, c0:c0 + LANES]
                acc = term if acc is None else acc + term
            cbuf[r0:r0 + CONV_ROWS, c0:c0 + LANES] = acc
    c = cbuf[...] + dwb_ref[...]
    m = jnp.mean(c, axis=-1, keepdims=True)
    d = c - m
    var = jnp.mean(d * d, axis=-1, keepdims=True)
    y = d * lax.rsqrt(var + 1e-5) * lnw_ref[...] + lnb_ref[...]
    o_ref[0] = _gated_residual(_silu(y) * sg_ref[0].astype(F32), w_ref, x_ref, mod_ref)


def _conv_out(z, sg, p, x, mods, layer, row_of):
    B, T, D = x.shape
    tm = min(CONV_ROW_TILE, T)
    gh = tm // HALO
    lasth = T // HALO - 1
    tile = pl.BlockSpec((1, tm, D), lambda b, t: (b, t, 0))
    prev = pl.BlockSpec((1, HALO, D), lambda b, t: (b, jnp.maximum(t * gh - 1, 0), 0))
    nxt = pl.BlockSpec((1, HALO, D), lambda b, t: (b, jnp.minimum((t + 1) * gh, lasth), 0))
    return pl.pallas_call(
        _conv_out_kernel,
        grid=(B, T // tm),
        in_specs=[tile, prev, nxt, tile, _const_spec((CONV_WIDTH, D)), _const_spec((1, D)),
                  _const_spec((1, D)), _const_spec((1, D)), _const_spec((D, D)), tile,
                  _mod_spec(layer, row_of)],
        out_specs=tile,
        out_shape=jax.ShapeDtypeStruct((B, T, D), F32),
        scratch_shapes=[pltpu.VMEM((SUBLANES, tm + 2 * HALO, D), F32), pltpu.VMEM((tm, D), F32)],
        compiler_params=_cparams(("parallel", "parallel")),
        name="conv_out",
    )(z, z, z, sg, p["dw_w"], p["dw_b"], p["ln_w"], p["ln_b"], p["w_out"], x, mods)


def _rope_tables(n_tok):
    t = jnp.arange(n_tok, dtype=jnp.int32)
    pos = jnp.stack([(t // GRID_W).astype(F32), (t % GRID_W).astype(F32)], axis=1)
    lane = jnp.arange(LANES, dtype=jnp.int32)
    d = lane % ATTN_HEAD_DIM
    axis = d // 32
    second_half = (d % 32) // 16
    axis_dim = ATTN_HEAD_DIM // 2
    inv_freq = ROPE_BASE ** (-(2.0 * (d % 16).astype(F32)) / axis_dim)
    ang = pos[:, axis] * inv_freq[None, :]
    sign = jnp.where(second_half == 1, 1.0, -1.0).astype(F32)
    return jnp.cos(ang), jnp.sin(ang) * sign[None, :]


def kernel(x_prompt, x_sample, cache_attn_k, cache_attn_v, state_rwkv, c, c_ctx, norm_w, ada_w, ada_b, attn_w_in, attn_lambda, attn_subln_w, attn_w_out, rwkv_mu, rwkv_w_in, rwkv_w0, rwkv_w1, rwkv_w2, rwkv_a0, rwkv_a1, rwkv_a2, rwkv_k_k, rwkv_k_a, rwkv_r_k, rwkv_ln_w, rwkv_ln_b, rwkv_w_out, conv_w_in, conv_dw_w, conv_dw_b, conv_ln_w, conv_ln_b, conv_w_out, final_norm_w):
    D, E = D_MODEL, D_INNER
    dec_batch = x_sample.shape[0]
    assert dec_batch < COND_ROWS
    cond = jnp.concatenate([c, c_ctx[None, :], jnp.zeros((COND_ROWS - dec_batch - 1, D), F32)], axis=0)
    mods = _ada(cond, ada_w, ada_b)
    ctx_row = dec_batch
    streams = [(x_prompt, lambda b: ctx_row), (x_sample, lambda b: b)]
    cos, sin = _rope_tables(x_sample.shape[1])
    seg = (jnp.arange(MXU_DIM)[:, None] // RWKV_HEAD_DIM == jnp.arange(MXU_DIM)[None, :] // RWKV_HEAD_DIM).astype(BF16)

    xs = [x_prompt, x_sample]

    def rows(a, s):
        B, T, D = a.shape
        if s == 0 and T < ROW_TILE and (B * T) % ROW_TILE == 0:
            return a.reshape(B * T // ROW_TILE, ROW_TILE, D)
        return a
    new_kv, new_s = (None, None), []
    assert (DEPTH - 1) % N_MIXERS == 0
    for i in range(DEPTH):
        kind, j = i % N_MIXERS, i // N_MIXERS
        last = i == DEPTH - 1
        fw = final_norm_w if last else None
        if kind == 0:
            w_in = attn_w_in[j].astype(BF16)
            w_out = attn_w_out[j].astype(BF16)
            for s, (_, row_of) in enumerate(streams):
                if s == 0:
                    qkvg, *new_kv = _attn_in(xs[s], norm_w[i], mods, i, row_of, w_in, cache=new_kv)
                    o = _attn_core(qkvg, attn_lambda[j], attn_subln_w[j], i)
                else:
                    (qkvg,) = _attn_in(xs[s], norm_w[i], mods, i, row_of, w_in, rope=(cos, sin))
                    o = _attn_core(qkvg, attn_lambda[j], attn_subln_w[j], i, ctx=(cache_attn_k, cache_attn_v, j))
                xs[s] = _out_proj(rows(o, s), w_out, rows(xs[s], s), mods, i, row_of, final_w=fw).reshape(xs[s].shape)
        elif kind == 1:
            p = dict(
                mu=rwkv_mu[j],
                w_in=rwkv_w_in[j].astype(BF16),
                l1=jnp.concatenate([rwkv_w1[j, 0], rwkv_w1[j, 1], rwkv_a1[j, 0], rwkv_a1[j, 1]], axis=1).astype(BF16),
                w2=rwkv_w2[j].reshape(2 * LORA, E).astype(BF16),
                a2=rwkv_a2[j].reshape(2 * LORA, E).astype(BF16),
                w0=rwkv_w0[j], a0=rwkv_a0[j],
                k_k=rwkv_k_k[j].reshape(1, E), k_a=rwkv_k_a[j].reshape(1, E), r_k=rwkv_r_k[j].reshape(1, E),
                ln_w=rwkv_ln_w[j].reshape(1, E), ln_b=rwkv_ln_b[j].reshape(1, E),
                w_out=rwkv_w_out[j].astype(BF16), seg=seg)
            for s, (_, row_of) in enumerate(streams):
                r, kk, v, sg, bonus, lw, a, kz = _rwkv_in(xs[s], norm_w[i], mods, i, row_of, p)
                y_f, y_b, final_states = _rwkv_scan(r, kk, v, lw, a, kz, (state_rwkv, j) if s == 1 else None)
                if s == 0:
                    new_s.append(final_states)
                xs[s] = _rwkv_out(rows(y_f, s), rows(y_b, s), rows(bonus, s), rows(sg, s), p, rows(xs[s], s),
                                  mods, i, row_of).reshape(xs[s].shape)
        else:
            p = dict(dw_w=conv_dw_w[j], dw_b=conv_dw_b[j].reshape(1, E), ln_w=conv_ln_w[j].reshape(1, E),
                     ln_b=conv_ln_b[j].reshape(1, E), w_out=conv_w_out[j].astype(BF16))
            w_in = conv_w_in[j].astype(BF16)
            for s, (_, row_of) in enumerate(streams):
                zz, sg = _conv_in(rows(xs[s], s), norm_w[i], mods, i, row_of, w_in)
                xs[s] = _conv_out(zz.reshape(xs[s].shape), sg.reshape(xs[s].shape), p, xs[s], mods, i, row_of)
    return (xs[0], xs[1], new_kv[0], new_kv[1], jnp.stack(new_s, axis=1))
```
